```python
import math
import jax, jax.numpy as jnp
from jax import lax
import numpy as np

D_MODEL = 1024
BATCH = 8
SEQ = 4096
DEPTH = 1

MEM_LEN = 256
EPS = 1e-6
GLA_HEADS = 4
GLA_DK = 64
GLA_DV = 128
GLA_RANK = 16
GLA_TAU = 16.0
GLA_CHUNK = 64
SWA_HEADS = 8
SWA_KV_HEADS = 2
SWA_DH = 64
WINDOW = 128
REL_BUCKETS = 32
REL_MAX_DIST = 128
CROSS_HEADS = 4
CROSS_DH = D_MODEL // CROSS_HEADS
D_FF = 2816
CONV_WIDTH = 3

GLA_KW = GLA_HEADS * GLA_DK
GLA_VW = GLA_HEADS * GLA_DV
SWA_QW = SWA_HEADS * SWA_DH
SWA_KVW = SWA_KV_HEADS * SWA_DH
D_MIX = GLA_VW + SWA_QW
IN_SPLITS = (GLA_KW, GLA_KW, GLA_VW, GLA_VW, GLA_RANK, SWA_QW, SWA_KVW, SWA_KVW)
D_IN = 2320

kernel_name = "hybrid_gla_swa_parallel_heads"


def rms_norm(x, g):
    xf = x.astype(jnp.float32)
    y = xf * lax.rsqrt(jnp.mean(xf * xf, axis=-1, keepdims=True) + EPS)
    return (y * g.astype(jnp.float32)).astype(x.dtype)


def t5_bucket(dist):
    n = jnp.maximum(dist, 0)
    max_exact = REL_BUCKETS // 2
    nf = jnp.maximum(n, 1).astype(jnp.float32)
    large = max_exact + (jnp.log(nf / max_exact) / math.log(REL_MAX_DIST / max_exact)
                         * (REL_BUCKETS - max_exact)).astype(jnp.int32)
    large = jnp.minimum(large, REL_BUCKETS - 1)
    return jnp.where(n < max_exact, n, large)


def gla_mixer(q, k, v, r, a_low, w_alpha, b_alpha, gla_gain):
    f32 = jnp.float32
    B, T, _ = q.shape
    C = GLA_CHUNK
    N = T // C
    log_a = jax.nn.log_sigmoid((a_low @ w_alpha + b_alpha).astype(f32)) / GLA_TAU

    def heads(t, d):
        return t.astype(f32).reshape(B, N, C, GLA_HEADS, d).transpose(0, 3, 1, 2, 4)

    qh = heads(q, GLA_DK) * (GLA_DK ** -0.5)
    kh = heads(k, GLA_DK)
    vh = heads(v, GLA_DV)
    bcum = jnp.cumsum(heads(log_a, GLA_DK), axis=3)
    b_last = bcum[:, :, :, -1:, :]
    q_dec = qh * jnp.exp(bcum)
    k_inv = kh * jnp.exp(-bcum)
    k_to_end = kh * jnp.exp(b_last - bcum)
    causal = jnp.tril(jnp.ones((C, C), dtype=bool))
    att = jnp.where(causal, jnp.einsum('bhnik,bhnjk->bhnij', q_dec, k_inv), 0.0)
    o_intra = jnp.einsum('bhnij,bhnjv->bhniv', att, vh)
    kv = jnp.einsum('bhnck,bhncv->nbhkv', k_to_end, vh)
    decay = jnp.exp(b_last[:, :, :, 0, :]).transpose(2, 0, 1, 3)

    def step(S, inp):
        d, kv_n = inp
        return d[..., None] * S + kv_n, S

    S0 = jnp.zeros((B, GLA_HEADS, GLA_DK, GLA_DV), f32)
    _, S_prev = lax.scan(step, S0, (decay, kv))
    o_inter = jnp.einsum('bhnck,nbhkv->bhncv', q_dec, S_prev)
    o = (o_intra + o_inter).transpose(0, 2, 3, 1, 4).reshape(B, T, GLA_HEADS, GLA_DV)
    o = o * lax.rsqrt(jnp.mean(o * o, axis=-1, keepdims=True) + EPS) * gla_gain.astype(f32)
    o = o.reshape(B, T, GLA_VW) * jax.nn.silu(r.astype(f32))
    return o.astype(q.dtype)


def swa_mixer(q, k, v, rel_bias, sinks):
    f32 = jnp.float32
    B, T, _ = q.shape
    L = WINDOW
    NB = T // L
    G = SWA_HEADS // SWA_KV_HEADS
    qb = q.astype(f32).reshape(B, NB, L, SWA_KV_HEADS, G, SWA_DH) * (SWA_DH ** -0.5)
    kb = k.astype(f32).reshape(B, NB, L, SWA_KV_HEADS, SWA_DH)
    vb = v.astype(f32).reshape(B, NB, L, SWA_KV_HEADS, SWA_DH)

    def with_prev(t):
        prev = jnp.pad(t, ((0, 0), (1, 0), (0, 0), (0, 0), (0, 0)))[:, :-1]
        return jnp.concatenate([prev, t], axis=2)

    kk = with_prev(kb)
    vv = with_prev(vb)
    s = jnp.einsum('bnqhgd,bnkhd->bnhgqk', qb, kk)
    qpos = jnp.arange(L)[:, None] + L
    kpos = jnp.arange(2 * L)[None, :]
    dist = qpos - kpos
    bias = rel_bias.astype(f32)[t5_bucket(dist)]
    bias = bias.transpose(2, 0, 1).reshape(SWA_KV_HEADS, G, L, 2 * L)
    in_window = (dist >= 0) & (dist < WINDOW)
    has_prev = (jnp.arange(NB)[:, None, None] > 0) | (kpos[None] >= L)
    mask = in_window[None] & has_prev
    s = jnp.where(mask[None, :, None, None], s + bias, -jnp.inf)
    sink = sinks.astype(f32).reshape(SWA_KV_HEADS, G)[None, None, :, :, None, None]
    m = jnp.maximum(jnp.max(s, axis=-1, keepdims=True), sink)
    p = jnp.exp(s - m)
    p = p / (jnp.sum(p, axis=-1, keepdims=True) + jnp.exp(sink - m))
    o = jnp.einsum('bnhgqk,bnkhd->bnqhgd', p, vv)
    return o.reshape(B, T, SWA_QW).astype(q.dtype)


def cross_attn(h, memn, w_q, w_k, w_v, w_o):
    B, T, _ = h.shape
    q = (h @ w_q).reshape(B, T, CROSS_HEADS, CROSS_DH)
    k = (memn @ w_k).reshape(B, MEM_LEN, CROSS_HEADS, CROSS_DH)
    v = (memn @ w_v).reshape(B, MEM_LEN, CROSS_HEADS, CROSS_DH)
    s = jnp.einsum('bqhd,bkhd->bhqk', q, k).astype(jnp.float32) * (CROSS_DH ** -0.5)
    p = jax.nn.softmax(s, axis=-1).astype(v.dtype)
    o = jnp.einsum('bhqk,bkhd->bqhd', p, v).reshape(B, T, D_MODEL)
    return o @ w_o


def conv_ffn(h, w_up, conv_w, conv_b, w_down):
    u = h @ w_up
    u = lax.conv_general_dilated(u, conv_w[:, None, :].astype(u.dtype), window_strides=(1,),
                                 padding=[(CONV_WIDTH - 1, 0)],
                                 dimension_numbers=('NWC', 'WIO', 'NWC'),
                                 feature_group_count=2 * D_FF) + conv_b
    gate, val = jnp.split(u, 2, axis=-1)
    return (jax.nn.silu(gate) * val) @ w_down


def setup_inputs(seed: int = 0) -> dict:
    key = jax.random.key(seed)
    ks = jax.random.split(key, 24)
    f32 = jnp.float32

    def nrm(k, shape, scale):
        return jax.random.normal(k, shape, f32) * scale

    def gain(k, shape):
        return 1.0 + 0.02 * jax.random.normal(k, shape, f32)

    return {
        "x": nrm(ks[0], (BATCH, SEQ, D_MODEL), 1.0),
        "mem": nrm(ks[1], (BATCH, MEM_LEN, D_MODEL), 1.0),
        "norm_mix": gain(ks[2], (DEPTH, D_MODEL)),
        "w_in": nrm(ks[3], (DEPTH, D_MODEL, D_IN), D_MODEL ** -0.5),
        "w_alpha": nrm(ks[4], (DEPTH, GLA_RANK, GLA_KW), GLA_RANK ** -0.5),
        "b_alpha": nrm(ks[5], (DEPTH, GLA_KW), 0.1),
        "gla_gain": gain(ks[6], (DEPTH, GLA_DV)),
        "rel_bias": nrm(ks[7], (REL_BUCKETS, SWA_HEADS), 0.5),
        "sinks": nrm(ks[8], (DEPTH, SWA_HEADS), 0.5),
        "w_out": nrm(ks[9], (DEPTH, D_MIX, D_MODEL), D_MIX ** -0.5),
        "norm_cross": gain(ks[10], (DEPTH, D_MODEL)),
        "norm_mem": gain(ks[11], (DEPTH, D_MODEL)),
        "w_q_c": nrm(ks[12], (DEPTH, D_MODEL, D_MODEL), D_MODEL ** -0.5),
        "w_k_c": nrm(ks[13], (DEPTH, D_MODEL, D_MODEL), D_MODEL ** -0.5),
        "w_v_c": nrm(ks[14], (DEPTH, D_MODEL, D_MODEL), D_MODEL ** -0.5),
        "w_o_c": nrm(ks[15], (DEPTH, D_MODEL, D_MODEL), D_MODEL ** -0.5),
        "norm_ffn": gain(ks[16], (DEPTH, D_MODEL)),
        "w_up": nrm(ks[17], (DEPTH, D_MODEL, 2 * D_FF), D_MODEL ** -0.5),
        "conv_w": nrm(ks[18], (DEPTH, CONV_WIDTH, 2 * D_FF), CONV_WIDTH ** -0.5),
        "conv_b": nrm(ks[19], (DEPTH, 2 * D_FF), 0.01),
        "w_down": nrm(ks[20], (DEPTH, D_FF, D_MODEL), D_FF ** -0.5),
        "norm_final": gain(ks[21], (D_MODEL,)),
    }


def reference(x, mem, norm_mix, w_in, w_alpha, b_alpha, gla_gain, rel_bias, sinks, w_out,
              norm_cross, norm_mem, w_q_c, w_k_c, w_v_c, w_o_c, norm_ffn, w_up, conv_w, conv_b,
              w_down, norm_final):
    split_at = [int(s) for s in np.cumsum(IN_SPLITS)[:-1]]
    h = x
    for l in range(DEPTH):
        hn = rms_norm(h, norm_mix[l])
        proj = hn @ w_in[l]
        gq, gk, gv, gr, ga, sq, sk, sv = jnp.split(proj, split_at, axis=-1)
        o_gla = gla_mixer(gq, gk, gv, gr, ga, w_alpha[l], b_alpha[l], gla_gain[l])
        o_swa = swa_mixer(sq, sk, sv, rel_bias, sinks[l])
        h = h + jnp.concatenate([o_gla, o_swa], axis=-1) @ w_out[l]
        h = h + cross_attn(rms_norm(h, norm_cross[l]), rms_norm(mem, norm_mem[l]),
                           w_q_c[l], w_k_c[l], w_v_c[l], w_o_c[l])
        h = h + conv_ffn(rms_norm(h, norm_ffn[l]), w_up[l], conv_w[l], conv_b[l], w_down[l])
    return rms_norm(h, norm_final)
```

```python
import functools
import math

import numpy as np
import jax
import jax.numpy as jnp
from jax import lax
from jax.experimental import pallas as pl
from jax.experimental.pallas import tpu as pltpu

F32 = jnp.float32
BF16 = jnp.bfloat16

D_MODEL = 1024
MEM_LEN = 256
EPS = 1e-6
GLA_HEADS = 4
GLA_DK = 64
GLA_DV = 128
GLA_RANK = 16
GLA_TAU = 16.0
GLA_CHUNK = 64
SWA_HEADS = 8
SWA_KV_HEADS = 2
SWA_DH = 64
WINDOW = 128
REL_BUCKETS = 32
REL_MAX_DIST = 128
CROSS_HEADS = 4
CROSS_DH = D_MODEL // CROSS_HEADS
D_FF = 2816
CONV_WIDTH = 3

GLA_KW = GLA_HEADS * GLA_DK
GLA_VW = GLA_HEADS * GLA_DV
SWA_QW = SWA_HEADS * SWA_DH
SWA_KVW = SWA_KV_HEADS * SWA_DH
IN_SPLITS = (GLA_KW, GLA_KW, GLA_VW, GLA_VW, GLA_RANK, SWA_QW, SWA_KVW, SWA_KVW)

LANES = 128
RANK_PAD = LANES
_PAD_SPLITS = (GLA_KW, GLA_KW, GLA_VW, GLA_VW, RANK_PAD, SWA_QW, SWA_KVW, SWA_KVW)
_OFF = tuple(int(v) for v in np.cumsum((0,) + _PAD_SPLITS))
D_IN_PAD = _OFF[-1]

TOK_TILE = 512
FFN_CHUNK = 256
N_FFN_CHUNKS = D_FF // FFN_CHUNK
CARRY_ROWS = 8
VMEM_LIMIT = 56 * 1024 * 1024

_NT = (((1,), (1,)), ((), ()))
_TN = (((0,), (0,)), ((), ()))


def _rms(x, g):
    return x * lax.rsqrt(jnp.mean(x * x, axis=-1, keepdims=True) + EPS) * g


def _dot(a, b):
    return jnp.dot(a, b, preferred_element_type=F32)


def _silu(x):
    return x * (1.0 / (1.0 + jnp.exp(-x)))


def _const_spec(shape):
    zeros = (0,) * len(shape)
    return pl.BlockSpec(shape, lambda *_: zeros)


def _params(sem):
    return pltpu.CompilerParams(dimension_semantics=sem, vmem_limit_bytes=VMEM_LIMIT)


def _inproj_kernel(x_ref, g_ref, w_ref, wal_ref, bal_ref,
                   gq_ref, gk_ref, gv_ref, gr_ref, la_ref, sq_ref, sk_ref, sv_ref):
    hn = _rms(x_ref[...], g_ref[...]).astype(BF16)

    def proj(i):
        return _dot(hn, w_ref[:, _OFF[i]:_OFF[i + 1]])

    gq_ref[...] = proj(0).astype(BF16)
    gk_ref[...] = proj(1).astype(BF16)
    gv_ref[...] = proj(2).astype(BF16)
    gr_ref[...] = proj(3).astype(BF16)
    a_low = proj(4)
    z = jnp.dot(a_low, wal_ref[...], preferred_element_type=F32,
                precision=lax.Precision.HIGHEST) + bal_ref[...]
    la_ref[...] = -(jnp.maximum(-z, 0.0) + jnp.log1p(jnp.exp(-jnp.abs(z)))) * (1.0 / GLA_TAU)
    sq_ref[...] = proj(5).astype(BF16)
    sk_ref[...] = proj(6)
    sv_ref[...] = proj(7)


def _in_proj(h, g, w_pad, wal_pad, bal):
    B, T, D = h.shape
    tm = TOK_TILE
    grid = (B, T // tm)

    def tok(width):
        return pl.BlockSpec((None, tm, width), lambda b, i: (b, i, 0))

    def out(width, dt):
        return jax.ShapeDtypeStruct((B, T, width), dt)

    return pl.pallas_call(
        _inproj_kernel,
        grid=grid,
        in_specs=[tok(D), _const_spec((1, D)), _const_spec((D, D_IN_PAD)),
                  _const_spec((RANK_PAD, GLA_KW)), _const_spec((1, GLA_KW))],
        out_specs=[tok(GLA_KW), tok(GLA_KW), tok(GLA_VW), tok(GLA_VW), tok(GLA_KW),
                   tok(SWA_QW), tok(SWA_KVW), tok(SWA_KVW)],
        out_shape=[out(GLA_KW, BF16), out(GLA_KW, BF16), out(GLA_VW, BF16), out(GLA_VW, BF16),
                   out(GLA_KW, F32), out(SWA_QW, BF16), out(SWA_KVW, F32), out(SWA_KVW, F32)],
        compiler_params=_params(("parallel", "parallel")),
        name="in_proj",
    )(h, g, w_pad, wal_pad, bal)


def _gla_kernel(q_ref, k_ref, v_ref, r_ref, la_ref, gain_ref, o_ref, st_ref):
    C = GLA_CHUNK
    H = GLA_HEADS

    @pl.when(pl.program_id(1) == 0)
    def _():
        st_ref[...] = jnp.zeros_like(st_ref)

    tril = (lax.broadcasted_iota(jnp.int32, (C, C), 0)
            >= lax.broadcasted_iota(jnp.int32, (C, C), 1)).astype(BF16)
    causal4 = ((lax.broadcasted_iota(jnp.int32, (H * C, C), 0) & (C - 1))
               >= lax.broadcasted_iota(jnp.int32, (H * C, C), 1))
    head_of_lane = lax.broadcasted_iota(jnp.int32, (C, GLA_KW), 1) // GLA_DK
    gain = gain_ref[...]

    def chunk(c, carry):
        rows = pl.ds(pl.multiple_of(c * C, C), C)
        la = la_ref[rows, :]
        la_hi = la.astype(BF16)
        la_lo = (la - la_hi.astype(F32)).astype(BF16)
        bcum = _dot(tril, la_hi) + _dot(tril, la_lo)
        b_last = bcum[C - 1:C, :]
        q = q_ref[rows, :].astype(F32) * (GLA_DK ** -0.5)
        k = k_ref[rows, :].astype(F32)
        v = v_ref[rows, :]
        q_dec = (q * jnp.exp(bcum)).astype(BF16)
        k_inv = (k * jnp.exp(-bcum)).astype(BF16)
        k_end = (k * jnp.exp(b_last - bcum)).astype(BF16)

        q_heads = [jnp.where(head_of_lane == h, q_dec, jnp.zeros_like(q_dec)) for h in range(H)]
        q4 = jnp.concatenate(q_heads, axis=0)
        att = lax.dot_general(q4, k_inv, _NT, preferred_element_type=F32)
        att = jnp.where(causal4, att, 0.0).astype(BF16)

        st = st_ref[...]
        st_b = st.astype(BF16)
        for h in range(H):
            vh = v[:, h * GLA_DV:(h + 1) * GLA_DV]
            o = _dot(att[h * C:(h + 1) * C, :], vh)
            o = o + lax.dot_general(q_heads[h], st_b[h * GLA_DV:(h + 1) * GLA_DV, :], _NT,
                                    preferred_element_type=F32)
            o = o * lax.rsqrt(jnp.mean(o * o, axis=-1, keepdims=True) + EPS) * gain
            r = r_ref[rows, h * GLA_DV:(h + 1) * GLA_DV].astype(F32)
            o_ref[rows, h * GLA_DV:(h + 1) * GLA_DV] = (o * _silu(r)).astype(o_ref.dtype)

        kv = lax.dot_general(v, k_end, _TN, preferred_element_type=F32)
        st_ref[...] = st * jnp.exp(b_last) + kv
        return carry

    lax.fori_loop(0, q_ref.shape[0] // C, chunk, 0)


def _gla(gq, gk, gv, gr, la, gain):
    B, T, _ = gq.shape
    tm = TOK_TILE

    def tok(width):
        return pl.BlockSpec((None, tm, width), lambda b, i: (b, i, 0))

    return pl.pallas_call(
        _gla_kernel,
        grid=(B, T // tm),
        in_specs=[tok(GLA_KW), tok(GLA_KW), tok(GLA_VW), tok(GLA_VW), tok(GLA_KW),
                  _const_spec((1, GLA_DV))],
        out_specs=tok(GLA_VW),
        out_shape=jax.ShapeDtypeStruct((B, T, GLA_VW), BF16),
        scratch_shapes=[pltpu.VMEM((GLA_VW, GLA_KW), F32)],
        compiler_params=_params(("parallel", "arbitrary")),
        name="gla",
    )(gq, gk, gv, gr, la, gain)


def _t5_bucket_table():
    L = WINDOW
    dist = (jnp.arange(L)[:, None] + L) - jnp.arange(2 * L)[None, :]
    n = jnp.maximum(dist, 0)
    max_exact = REL_BUCKETS // 2
    nf = jnp.maximum(n, 1).astype(F32)
    large = max_exact + (jnp.log(nf / max_exact) / math.log(REL_MAX_DIST / max_exact)
                         * (REL_BUCKETS - max_exact)).astype(jnp.int32)
    large = jnp.minimum(large, REL_BUCKETS - 1)
    bucket = jnp.where(n < max_exact, n, large)
    return jnp.where((dist >= 0) & (dist < WINDOW), bucket, -1).astype(jnp.int32)


def _swa_kernel(sinks_ref, relb_ref, bucket_ref, q_ref, kc_ref, vc_ref, kp_ref, vp_ref,
                o_ref, bias_ref):
    L = WINDOW
    n_blk = q_ref.shape[0] // L
    pair_w = 2 * SWA_DH
    n_pairs = SWA_QW // pair_w
    pairs_per_kv = n_pairs // SWA_KV_HEADS
    neg_inf = float("-inf")

    @pl.when((pl.program_id(0) == 0) & (pl.program_id(1) == 0))
    def _():
        bucket = bucket_ref[...]
        for h in range(SWA_HEADS):
            def body(b, acc):
                return jnp.where(bucket == b, relb_ref[b, h], acc)
            bias_ref[h] = lax.fori_loop(0, REL_BUCKETS, body, jnp.full((L, 2 * L), neg_inf, F32))

    def dup_heads(cat):
        low = lax.broadcasted_iota(jnp.int32, cat.shape, 1) < SWA_DH
        rolled = pltpu.roll(cat, SWA_DH, 1)
        return (jnp.where(low, cat, rolled).astype(BF16), jnp.where(low, rolled, cat).astype(BF16))

    kd = dup_heads(jnp.concatenate([kp_ref[...], kc_ref[...]], axis=0))
    vd = dup_heads(jnp.concatenate([vp_ref[...], vc_ref[...]], axis=0))

    low = lax.broadcasted_iota(jnp.int32, (L, pair_w), 1) < SWA_DH
    no_prev = jnp.where((pl.program_id(1) == 0)
                        & (lax.broadcasted_iota(jnp.int32, (L, 2 * L), 1) < L), neg_inf, 0.0)

    for j in range(n_blk):
        for p in range(n_pairs):
            g = p // pairs_per_kv
            qp = q_ref[j * L:(j + 1) * L, p * pair_w:(p + 1) * pair_w] * (SWA_DH ** -0.5)
            zero = jnp.zeros_like(qp)
            q2 = jnp.concatenate([jnp.where(low, qp, zero), jnp.where(low, zero, qp)], axis=0)
            keys = kd[g][j * L:(j + 2) * L, :]
            vals = vd[g][j * L:(j + 2) * L, :]
            s2 = lax.dot_general(q2, keys, _NT, preferred_element_type=F32)
            outs = []
            for e in range(2):
                h = 2 * p + e
                s = s2[e * L:(e + 1) * L, :] + bias_ref[h]
                if j == 0:
                    s = s + no_prev
                sink = sinks_ref[h]
                m = jnp.maximum(jnp.max(s, axis=-1, keepdims=True), sink)
                pr = jnp.exp(s - m)
                denom = jnp.sum(pr, axis=-1, keepdims=True) + jnp.exp(sink - m)
                outs.append(_dot(pr.astype(BF16), vals) * (1.0 / denom))
            o_ref[j * L:(j + 1) * L, p * pair_w:(p + 1) * pair_w] = (
                jnp.where(low, outs[0], outs[1]).astype(o_ref.dtype))


def _swa(sq, sk, sv, rel_bias, sinks):
    B, T, _ = sq.shape
    tm = TOK_TILE
    L = WINDOW
    per = tm // L

    def tok(width):
        return pl.BlockSpec((None, tm, width), lambda b, i: (b, i, 0))

    prev = pl.BlockSpec((None, L, SWA_KVW), lambda b, i: (b, jnp.maximum(i * per - 1, 0), 0))
    smem = pl.BlockSpec(memory_space=pltpu.SMEM)
    return pl.pallas_call(
        _swa_kernel,
        grid=(B, T // tm),
        in_specs=[smem, smem, _const_spec((L, 2 * L)), tok(SWA_QW), tok(SWA_KVW), tok(SWA_KVW),
                  prev, prev],
        out_specs=tok(SWA_QW),
        out_shape=jax.ShapeDtypeStruct((B, T, SWA_QW), BF16),
        scratch_shapes=[pltpu.VMEM((SWA_HEADS, L, 2 * L), F32)],
        compiler_params=_params(("arbitrary", "arbitrary")),
        name="swa",
    )(sinks, rel_bias, _t5_bucket_table(), sq, sk, sv, sk, sv)


def _memkv_kernel(mem_ref, g_ref, wk_ref, wv_ref, k_ref, v_ref):
    mn = _rms(mem_ref[...], g_ref[...]).astype(BF16)
    k_ref[...] = _dot(mn, wk_ref[...]).astype(BF16)
    v_ref[...] = _dot(mn, wv_ref[...]).astype(BF16)


def _mem_kv(mem, g, wk, wv):
    B, M, D = mem.shape
    blk = pl.BlockSpec((None, M, D), lambda b: (b, 0, 0))
    return pl.pallas_call(
        _memkv_kernel,
        grid=(B,),
        in_specs=[blk, _const_spec((1, D)), _const_spec((D, D)), _const_spec((D, D))],
        out_specs=[blk, blk],
        out_shape=[jax.ShapeDtypeStruct((B, M, D), BF16)] * 2,
        compiler_params=_params(("parallel",)),
        name="mem_kv",
    )(mem, g, wk, wv)


def _outcross_kernel(h_ref, og_ref, os_ref, kc_ref, vc_ref, woa_ref, wob_ref, g_ref,
                     wq_ref, wo_ref, o_ref):
    h1 = h_ref[...] + _dot(og_ref[...], woa_ref[...]) + _dot(os_ref[...], wob_ref[...])
    hn = _rms(h1, g_ref[...]).astype(BF16)
    q = (_dot(hn, wq_ref[...]) * (CROSS_DH ** -0.5)).astype(BF16)
    heads = []
    for hd in range(CROSS_HEADS):
        cols = slice(hd * CROSS_DH, (hd + 1) * CROSS_DH)
        s = lax.dot_general(q[:, cols], kc_ref[:, cols], _NT, preferred_element_type=F32)
        m = jnp.max(s, axis=-1, keepdims=True)
        p = jnp.exp(s - m)
        denom = jnp.sum(p, axis=-1, keepdims=True)
        heads.append((_dot(p.astype(BF16), vc_ref[:, cols]) * (1.0 / denom)).astype(BF16))
    o_ref[...] = h1 + _dot(jnp.concatenate(heads, axis=1), wo_ref[...])


def _out_cross(h, og, osw, kc, vc, woa, wob, g, wq, wo):
    B, T, D = h.shape
    tm = TOK_TILE

    def tok(width):
        return pl.BlockSpec((None, tm, width), lambda b, i: (b, i, 0))

    memblk = pl.BlockSpec((None, MEM_LEN, D), lambda b, i: (b, 0, 0))
    return pl.pallas_call(
        _outcross_kernel,
        grid=(B, T // tm),
        in_specs=[tok(D), tok(GLA_VW), tok(SWA_QW), memblk, memblk,
                  _const_spec((GLA_VW, D)), _const_spec((SWA_QW, D)), _const_spec((1, D)),
                  _const_spec((D, D)), _const_spec((D, D))],
        out_specs=tok(D),
        out_shape=jax.ShapeDtypeStruct((B, T, D), F32),
        compiler_params=_params(("parallel", "parallel")),
        name="out_cross",
    )(h, og, osw, kc, vc, woa, wob, g, wq, wo)


def _causal_conv(u, prev, w, b):
    head_rows = lax.broadcasted_iota(jnp.int32, (CARRY_ROWS, u.shape[1]), 0)
    last = prev[CARRY_ROWS - 1:CARRY_ROWS, :]
    last2 = prev[CARRY_ROWS - 2:CARRY_ROWS - 1, :]
    u1 = pltpu.roll(u, 1, 0)
    u2 = pltpu.roll(u, 2, 0)
    u1 = jnp.concatenate([jnp.where(head_rows == 0, last, u1[:CARRY_ROWS]), u1[CARRY_ROWS:]], axis=0)
    u2_head = jnp.where(head_rows == 0, last2, jnp.where(head_rows == 1, last, u2[:CARRY_ROWS]))
    u2 = jnp.concatenate([u2_head, u2[CARRY_ROWS:]], axis=0)
    return w[2:3, :] * u + w[1:2, :] * u1 + w[0:1, :] * u2 + b


def _ffn_kernel(h_ref, g_ref, wg_ref, wv_ref, wd_ref, cwg_ref, cwv_ref, cbg_ref, cbv_ref, gf_ref,
                o_ref, hn_ref, acc_ref, carry_g_ref, carry_v_ref, *, final_norm):
    tm = h_ref.shape[0]

    @pl.when(pl.program_id(1) == 0)
    def _():
        carry_g_ref[...] = jnp.zeros_like(carry_g_ref)
        carry_v_ref[...] = jnp.zeros_like(carry_v_ref)

    hn_ref[...] = _rms(h_ref[...], g_ref[...]).astype(BF16)
    acc_ref[...] = h_ref[...]

    def chunk(c, carry):
        hn = hn_ref[...]
        ug = _dot(hn, wg_ref[c])
        uv = _dot(hn, wv_ref[c])
        gate = _causal_conv(ug, carry_g_ref[c], cwg_ref[c], cbg_ref[c])
        val = _causal_conv(uv, carry_v_ref[c], cwv_ref[c], cbv_ref[c])
        carry_g_ref[c] = ug[tm - CARRY_ROWS:, :]
        carry_v_ref[c] = uv[tm - CARRY_ROWS:, :]
        act = (_silu(gate) * val).astype(BF16)
        acc_ref[...] += _dot(act, wd_ref[c])
        return carry

    lax.fori_loop(0, N_FFN_CHUNKS, chunk, 0)
    out = acc_ref[...]
    if final_norm:
        out = _rms(out, gf_ref[...])
    o_ref[...] = out


def _ffn(h, g, wg, wv, wd, cwg, cwv, cbg, cbv, gf, final_norm):
    B, T, D = h.shape
    tm = TOK_TILE
    nc, fc = N_FFN_CHUNKS, FFN_CHUNK
    tok = pl.BlockSpec((None, tm, D), lambda b, i: (b, i, 0))

    def once(shape):
        zeros = (0,) * len(shape)
        return pl.BlockSpec(shape, lambda *_: zeros, pipeline_mode=pl.Buffered(1))

    return pl.pallas_call(
        functools.partial(_ffn_kernel, final_norm=final_norm),
        grid=(B, T // tm),
        in_specs=[tok, _const_spec((1, D)), once((nc, D, fc)), once((nc, D, fc)), once((nc, fc, D)),
                  _const_spec((nc, CONV_WIDTH, fc)), _const_spec((nc, CONV_WIDTH, fc)),
                  _const_spec((nc, 1, fc)), _const_spec((nc, 1, fc)), _const_spec((1, D))],
        out_specs=tok,
        out_shape=jax.ShapeDtypeStruct((B, T, D), F32),
        scratch_shapes=[pltpu.VMEM((tm, D), BF16), pltpu.VMEM((tm, D), F32),
                        pltpu.VMEM((nc, CARRY_ROWS, fc), F32), pltpu.VMEM((nc, CARRY_ROWS, fc), F32)],
        compiler_params=_params(("parallel", "arbitrary")),
        name="ffn",
    )(h, g, wg, wv, wd, cwg, cwv, cbg, cbv, gf)


def _pad_in_proj(w_in):
    parts = jnp.split(w_in, [int(s) for s in np.cumsum(IN_SPLITS)[:-1]], axis=-1)
    parts[4] = jnp.pad(parts[4], ((0, 0), (0, RANK_PAD - GLA_RANK)))
    return jnp.concatenate(parts, axis=-1).astype(BF16)


def _chunk_cols(w):
    r = w.shape[0]
    return w.reshape(r, N_FFN_CHUNKS, FFN_CHUNK).transpose(1, 0, 2)


def kernel(x, mem, norm_mix, w_in, w_alpha, b_alpha, gla_gain, rel_bias, sinks, w_out, norm_cross,
           norm_mem, w_q_c, w_k_c, w_v_c, w_o_c, norm_ffn, w_up, conv_w, conv_b, w_down, norm_final):
    depth = w_in.shape[0]
    row = lambda v: v.reshape(1, -1).astype(F32)
    h = x
    for l in range(depth):
        w_pad = _pad_in_proj(w_in[l])
        wal_pad = jnp.pad(w_alpha[l], ((0, RANK_PAD - GLA_RANK), (0, 0)))
        gq, gk, gv, gr, la, sq, sk, sv = _in_proj(h, row(norm_mix[l]), w_pad, wal_pad,
                                                   row(b_alpha[l]))
        o_gla = _gla(gq, gk, gv, gr, la, row(gla_gain[l]))
        o_swa = _swa(sq, sk, sv, rel_bias.astype(F32), sinks[l].astype(F32))
        kc, vc = _mem_kv(mem, row(norm_mem[l]), w_k_c[l].astype(BF16), w_v_c[l].astype(BF16))
        wo = w_out[l].astype(BF16)
        h = _out_cross(h, o_gla, o_swa, kc, vc, wo[:GLA_VW], wo[GLA_VW:], row(norm_cross[l]),
                       w_q_c[l].astype(BF16), w_o_c[l].astype(BF16))
        wup = w_up[l].astype(BF16)
        h = _ffn(h, row(norm_ffn[l]),
                 _chunk_cols(wup[:, :D_FF]), _chunk_cols(wup[:, D_FF:]),
                 w_down[l].astype(BF16).reshape(N_FFN_CHUNKS, FFN_CHUNK, D_MODEL),
                 _chunk_cols(conv_w[l][:, :D_FF]), _chunk_cols(conv_w[l][:, D_FF:]),
                 _chunk_cols(conv_b[l][None, :D_FF]), _chunk_cols(conv_b[l][None, D_FF:]),
                 row(norm_final), final_norm=(l == depth - 1))
    return h
```

```python
import functools
import math

import numpy as np
import jax
import jax.numpy as jnp
from jax import lax
from jax.experimental import pallas as pl
from jax.experimental.pallas import tpu as pltpu

F32 = jnp.float32
BF16 = jnp.bfloat16

D_MODEL = 1024
MEM_LEN = 256
EPS = 1e-6
GLA_HEADS = 4
GLA_DK = 64
GLA_DV = 128
GLA_RANK = 16
GLA_TAU = 16.0
GLA_CHUNK = 64
SWA_HEADS = 8
SWA_KV_HEADS = 2
SWA_DH = 64
WINDOW = 128
REL_BUCKETS = 32
REL_MAX_DIST = 128
CROSS_HEADS = 4
CROSS_DH = D_MODEL // CROSS_HEADS
D_FF = 2816
CONV_WIDTH = 3

GLA_KW = GLA_HEADS * GLA_DK
GLA_VW = GLA_HEADS * GLA_DV
SWA_QW = SWA_HEADS * SWA_DH
SWA_KVW = SWA_KV_HEADS * SWA_DH
IN_SPLITS = (GLA_KW, GLA_KW, GLA_VW, GLA_VW, GLA_RANK, SWA_QW, SWA_KVW, SWA_KVW)

LANES = 128
RANK_PAD = LANES
_PAD_SPLITS = (GLA_KW, GLA_KW, GLA_VW, GLA_VW, RANK_PAD, SWA_QW, SWA_KVW, SWA_KVW)
_OFF = tuple(int(v) for v in np.cumsum((0,) + _PAD_SPLITS))
D_IN_PAD = _OFF[-1]

TOK_TILE = 512
FFN_CHUNK = 256
N_FFN_CHUNKS = D_FF // FFN_CHUNK
CARRY_ROWS = 8
VMEM_LIMIT = 56 * 1024 * 1024

_NT = (((1,), (1,)), ((), ()))
_TN = (((0,), (0,)), ((), ()))


def _rms(x, g):
    return x * lax.rsqrt(jnp.mean(x * x, axis=-1, keepdims=True) + EPS) * g


def _dot(a, b):
    return jnp.dot(a, b, preferred_element_type=F32)


def _split_bf16(x):
    hi = x.astype(BF16)
    return hi, (x - hi.astype(F32)).astype(BF16)


def _silu(x):
    return x * (1.0 / (1.0 + jnp.exp(-x)))


def _const_spec(shape):
    zeros = (0,) * len(shape)
    return pl.BlockSpec(shape, lambda *_: zeros)


def _params(sem):
    return pltpu.CompilerParams(dimension_semantics=sem, vmem_limit_bytes=VMEM_LIMIT)


def _inproj_kernel(x_ref, g_ref, w_ref, wal_ref, bal_ref,
                   gq_ref, gk_ref, gv_ref, gr_ref, la_ref, sq_ref, sk_ref, sv_ref):
    hn = _rms(x_ref[...], g_ref[...]).astype(BF16)
    allp = _dot(hn, w_ref[...])

    def proj(i):
        return allp[:, _OFF[i]:_OFF[i + 1]]

    gq_ref[...] = proj(0).astype(BF16)
    gk_ref[...] = proj(1).astype(BF16)
    gv_ref[...] = proj(2).astype(BF16)
    gr_ref[...] = proj(3).astype(BF16)
    a_low = proj(4)
    a_hi, a_lo = _split_bf16(a_low)
    w_hi, w_lo = _split_bf16(wal_ref[...])
    z = _dot(a_hi, w_hi) + _dot(a_lo, w_hi) + _dot(a_hi, w_lo) + bal_ref[...]
    la_ref[...] = -(jnp.maximum(-z, 0.0) + jnp.log1p(jnp.exp(-jnp.abs(z)))) * (1.0 / GLA_TAU)
    sq_ref[...] = proj(5).astype(BF16)
    sk_ref[...] = proj(6)
    sv_ref[...] = proj(7)


def _in_proj(h, g, w_pad, wal_pad, bal):
    B, T, D = h.shape
    tm = TOK_TILE
    grid = (B, T // tm)

    def tok(width):
        return pl.BlockSpec((None, tm, width), lambda b, i: (b, i, 0))

    def out(width, dt):
        return jax.ShapeDtypeStruct((B, T, width), dt)

    return pl.pallas_call(
        _inproj_kernel,
        grid=grid,
        in_specs=[tok(D), _const_spec((1, D)), _const_spec((D, D_IN_PAD)),
                  _const_spec((RANK_PAD, GLA_KW)), _const_spec((1, GLA_KW))],
        out_specs=[tok(GLA_KW), tok(GLA_KW), tok(GLA_VW), tok(GLA_VW), tok(GLA_KW),
                   tok(SWA_QW), tok(SWA_KVW), tok(SWA_KVW)],
        out_shape=[out(GLA_KW, BF16), out(GLA_KW, BF16), out(GLA_VW, BF16), out(GLA_VW, BF16),
                   out(GLA_KW, F32), out(SWA_QW, BF16), out(SWA_KVW, F32), out(SWA_KVW, F32)],
        compiler_params=_params(("parallel", "parallel")),
        name="in_proj",
    )(h, g, w_pad, wal_pad, bal)


def _gla_kernel(q_ref, k_ref, v_ref, r_ref, la_ref, gain_ref, o_ref, st_ref):
    C = GLA_CHUNK
    H = GLA_HEADS

    @pl.when(pl.program_id(1) == 0)
    def _():
        st_ref[...] = jnp.zeros_like(st_ref)

    tril = (lax.broadcasted_iota(jnp.int32, (C, C), 0)
            >= lax.broadcasted_iota(jnp.int32, (C, C), 1)).astype(BF16)
    causal4 = ((lax.broadcasted_iota(jnp.int32, (H * C, C), 0) & (C - 1))
               >= lax.broadcasted_iota(jnp.int32, (H * C, C), 1))
    head_of_lane = lax.broadcasted_iota(jnp.int32, (C, GLA_KW), 1) // GLA_DK
    head_of_row = lax.broadcasted_iota(jnp.int32, (GLA_KW, GLA_DV), 0) // GLA_DK
    ones_cv = jnp.ones((C, GLA_DV), BF16)
    gain = gain_ref[...]

    for c in range(q_ref.shape[0] // C):
        rows = slice(c * C, (c + 1) * C)
        la_hi, la_lo = _split_bf16(la_ref[rows, :])
        bcum = _dot(tril, la_hi) + _dot(tril, la_lo)
        b_last = bcum[C - 1:C, :]
        b_last_col = (lax.dot_general(la_hi, ones_cv, _TN, preferred_element_type=F32)
                      + lax.dot_general(la_lo, ones_cv, _TN, preferred_element_type=F32))
        q = q_ref[rows, :].astype(F32) * (GLA_DK ** -0.5)
        k = k_ref[rows, :].astype(F32)
        v = v_ref[rows, :]
        q_dec = (q * jnp.exp(bcum)).astype(BF16)
        k_inv = (k * jnp.exp(-bcum)).astype(BF16)
        k_end = (k * jnp.exp(b_last - bcum)).astype(BF16)

        q4 = jnp.concatenate([jnp.where(head_of_lane == h, q_dec, jnp.zeros_like(q_dec))
                              for h in range(H)], axis=0)
        att = lax.dot_general(q4, k_inv, _NT, preferred_element_type=F32)
        att = jnp.where(causal4, att, 0.0).astype(BF16)

        st = st_ref[...]
        inter = _dot(q4, st.astype(BF16))
        for h in range(H):
            cols = slice(h * GLA_DV, (h + 1) * GLA_DV)
            o = _dot(att[h * C:(h + 1) * C, :], v[:, cols]) + inter[h * C:(h + 1) * C, :]
            o = o * lax.rsqrt(jnp.mean(o * o, axis=-1, keepdims=True) + EPS) * gain
            o_ref[rows, cols] = (o * _silu(r_ref[rows, cols].astype(F32))).astype(o_ref.dtype)

        kv_all = lax.dot_general(k_end, v, _TN, preferred_element_type=F32)
        kv = kv_all[:, (H - 1) * GLA_DV:]
        for h in range(H - 2, -1, -1):
            kv = jnp.where(head_of_row == h, kv_all[:, h * GLA_DV:(h + 1) * GLA_DV], kv)
        st_ref[...] = st * jnp.exp(b_last_col) + kv


def _gla(gq, gk, gv, gr, la, gain):
    B, T, _ = gq.shape
    tm = TOK_TILE

    def tok(width):
        return pl.BlockSpec((None, tm, width), lambda b, i: (b, i, 0))

    return pl.pallas_call(
        _gla_kernel,
        grid=(B, T // tm),
        in_specs=[tok(GLA_KW), tok(GLA_KW), tok(GLA_VW), tok(GLA_VW), tok(GLA_KW),
                  _const_spec((1, GLA_DV))],
        out_specs=tok(GLA_VW),
        out_shape=jax.ShapeDtypeStruct((B, T, GLA_VW), BF16),
        scratch_shapes=[pltpu.VMEM((GLA_KW, GLA_DV), F32)],
        compiler_params=_params(("parallel", "arbitrary")),
        name="gla",
    )(gq, gk, gv, gr, la, gain)


def _t5_bucket_table():
    L = WINDOW
    dist = (jnp.arange(L)[:, None] + L) - jnp.arange(2 * L)[None, :]
    n = jnp.maximum(dist, 0)
    max_exact = REL_BUCKETS // 2
    nf = jnp.maximum(n, 1).astype(F32)
    large = max_exact + (jnp.log(nf / max_exact) / math.log(REL_MAX_DIST / max_exact)
                         * (REL_BUCKETS - max_exact)).astype(jnp.int32)
    large = jnp.minimum(large, REL_BUCKETS - 1)
    bucket = jnp.where(n < max_exact, n, large)
    return jnp.where((dist >= 0) & (dist < WINDOW), bucket, -1).astype(jnp.int32)


def _swa_kernel(sinks_ref, relb_ref, bucket_ref, q_ref, kc_ref, vc_ref, kp_ref, vp_ref,
                o_ref, bias_ref):
    L = WINDOW
    n_blk = q_ref.shape[0] // L
    pair_w = 2 * SWA_DH
    n_pairs = SWA_QW // pair_w
    pairs_per_kv = n_pairs // SWA_KV_HEADS
    neg_inf = float("-inf")

    @pl.when((pl.program_id(0) == 0) & (pl.program_id(1) == 0))
    def _():
        bucket = bucket_ref[...]
        for h in range(SWA_HEADS):
            def body(b, acc):
                return jnp.where(bucket == b, relb_ref[b, h], acc)
            bias_ref[h] = lax.fori_loop(0, REL_BUCKETS, body, jnp.full((L, 2 * L), neg_inf, F32))

    def dup_heads(cat):
        low = lax.broadcasted_iota(jnp.int32, cat.shape, 1) < SWA_DH
        rolled = pltpu.roll(cat, SWA_DH, 1)
        return (jnp.where(low, cat, rolled).astype(BF16), jnp.where(low, rolled, cat).astype(BF16))

    kd = dup_heads(jnp.concatenate([kp_ref[...], kc_ref[...]], axis=0))
    vd = dup_heads(jnp.concatenate([vp_ref[...], vc_ref[...]], axis=0))

    low = lax.broadcasted_iota(jnp.int32, (L, pair_w), 1) < SWA_DH
    no_prev = jnp.where((pl.program_id(1) == 0)
                        & (lax.broadcasted_iota(jnp.int32, (L, 2 * L), 1) < L), neg_inf, 0.0)

    for j in range(n_blk):
        for p in range(n_pairs):
            g = p // pairs_per_kv
            qp = q_ref[j * L:(j + 1) * L, p * pair_w:(p + 1) * pair_w] * (SWA_DH ** -0.5)
            zero = jnp.zeros_like(qp)
            q2 = jnp.concatenate([jnp.where(low, qp, zero), jnp.where(low, zero, qp)], axis=0)
            keys = kd[g][j * L:(j + 2) * L, :]
            vals = vd[g][j * L:(j + 2) * L, :]
            s2 = lax.dot_general(q2, keys, _NT, preferred_element_type=F32)
            outs = []
            for e in range(2):
                h = 2 * p + e
                s = s2[e * L:(e + 1) * L, :] + bias_ref[h]
                if j == 0:
                    s = s + no_prev
                sink = sinks_ref[h]
                m = jnp.maximum(jnp.max(s, axis=-1, keepdims=True), sink)
                pr = jnp.exp(s - m)
                denom = jnp.sum(pr, axis=-1, keepdims=True) + jnp.exp(sink - m)
                outs.append(_dot(pr.astype(BF16), vals) * (1.0 / denom))
            o_ref[j * L:(j + 1) * L, p * pair_w:(p + 1) * pair_w] = (
                jnp.where(low, outs[0], outs[1]).astype(o_ref.dtype))


def _swa(sq, sk, sv, rel_bias, sinks):
    B, T, _ = sq.shape
    tm = TOK_TILE
    L = WINDOW
    per = tm // L

    def tok(width):
        return pl.BlockSpec((None, tm, width), lambda b, i: (b, i, 0))

    prev = pl.BlockSpec((None, L, SWA_KVW), lambda b, i: (b, jnp.maximum(i * per - 1, 0), 0))
    smem = pl.BlockSpec(memory_space=pltpu.SMEM)
    return pl.pallas_call(
        _swa_kernel,
        grid=(B, T // tm),
        in_specs=[smem, smem, _const_spec((L, 2 * L)), tok(SWA_QW), tok(SWA_KVW), tok(SWA_KVW),
                  prev, prev],
        out_specs=tok(SWA_QW),
        out_shape=jax.ShapeDtypeStruct((B, T, SWA_QW), BF16),
        scratch_shapes=[pltpu.VMEM((SWA_HEADS, L, 2 * L), F32)],
        compiler_params=_params(("arbitrary", "arbitrary")),
        name="swa",
    )(sinks, rel_bias, _t5_bucket_table(), sq, sk, sv, sk, sv)


def _memkv_kernel(mem_ref, g_ref, wk_ref, wv_ref, k_ref, v_ref):
    mn = _rms(mem_ref[...], g_ref[...]).astype(BF16)
    k_ref[...] = _dot(mn, wk_ref[...]).astype(BF16)
    v_ref[...] = _dot(mn, wv_ref[...]).astype(BF16)


def _mem_kv(mem, g, wk, wv):
    B, M, D = mem.shape
    blk = pl.BlockSpec((None, M, D), lambda b: (b, 0, 0))
    return pl.pallas_call(
        _memkv_kernel,
        grid=(B,),
        in_specs=[blk, _const_spec((1, D)), _const_spec((D, D)), _const_spec((D, D))],
        out_specs=[blk, blk],
        out_shape=[jax.ShapeDtypeStruct((B, M, D), BF16)] * 2,
        compiler_params=_params(("parallel",)),
        name="mem_kv",
    )(mem, g, wk, wv)


def _outcross_kernel(h_ref, og_ref, os_ref, kc_ref, vc_ref, woa_ref, wob_ref, g_ref,
                     wq_ref, wo_ref, o_ref):
    h1 = h_ref[...] + _dot(og_ref[...], woa_ref[...]) + _dot(os_ref[...], wob_ref[...])
    hn = _rms(h1, g_ref[...]).astype(BF16)
    q = (_dot(hn, wq_ref[...]) * (CROSS_DH ** -0.5)).astype(BF16)
    heads = []
    for hd in range(CROSS_HEADS):
        cols = slice(hd * CROSS_DH, (hd + 1) * CROSS_DH)
        s = lax.dot_general(q[:, cols], kc_ref[:, cols], _NT, preferred_element_type=F32)
        m = jnp.max(s, axis=-1, keepdims=True)
        p = jnp.exp(s - m)
        denom = jnp.sum(p, axis=-1, keepdims=True)
        heads.append((_dot(p.astype(BF16), vc_ref[:, cols]) * (1.0 / denom)).astype(BF16))
    o_ref[...] = h1 + _dot(jnp.concatenate(heads, axis=1), wo_ref[...])


def _out_cross(h, og, osw, kc, vc, woa, wob, g, wq, wo):
    B, T, D = h.shape
    tm = TOK_TILE

    def tok(width):
        return pl.BlockSpec((None, tm, width), lambda b, i: (b, i, 0))

    memblk = pl.BlockSpec((None, MEM_LEN, D), lambda b, i: (b, 0, 0))
    return pl.pallas_call(
        _outcross_kernel,
        grid=(B, T // tm),
        in_specs=[tok(D), tok(GLA_VW), tok(SWA_QW), memblk, memblk,
                  _const_spec((GLA_VW, D)), _const_spec((SWA_QW, D)), _const_spec((1, D)),
                  _const_spec((D, D)), _const_spec((D, D))],
        out_specs=tok(D),
        out_shape=jax.ShapeDtypeStruct((B, T, D), F32),
        compiler_params=_params(("parallel", "parallel")),
        name="out_cross",
    )(h, og, osw, kc, vc, woa, wob, g, wq, wo)


def _ffn_kernel(h_ref, g_ref, wu_ref, wd_ref, cw_ref, cb_ref, gf_ref,
                o_ref, hn_ref, acc_ref, ubuf_ref, carry_ref, *, final_norm):
    tm = h_ref.shape[0]
    n_slab = 2 * FFN_CHUNK // LANES
    half = n_slab // 2

    @pl.when(pl.program_id(1) == 0)
    def _():
        carry_ref[...] = jnp.zeros_like(carry_ref)

    hn_ref[...] = _rms(h_ref[...], g_ref[...]).astype(BF16)
    acc_ref[...] = h_ref[...]

    def up(c):
        u = _dot(hn_ref[...], wu_ref[c])
        for s in range(n_slab):
            us = u[:, s * LANES:(s + 1) * LANES]
            ubuf_ref[c % 2, s, 0:CARRY_ROWS, :] = carry_ref[c, s]
            ubuf_ref[c % 2, s, CARRY_ROWS:, :] = us
            carry_ref[c, s] = us[tm - CARRY_ROWS:, :]

    def conv(c, s):
        lanes = slice(s * LANES, (s + 1) * LANES)
        w = cw_ref[c]
        taps = [ubuf_ref[c % 2, s, CARRY_ROWS - d:CARRY_ROWS - d + tm, :] for d in range(CONV_WIDTH)]
        return (w[2:3, lanes] * taps[0]
                + (w[1:2, lanes] * taps[1] + (w[0:1, lanes] * taps[2] + cb_ref[c][:, lanes])))

    up(0)
    for c in range(N_FFN_CHUNKS):
        if c + 1 < N_FFN_CHUNKS:
            up(c + 1)
        act = jnp.concatenate([(_silu(conv(c, s)) * conv(c, half + s)).astype(BF16)
                               for s in range(half)], axis=1)
        acc_ref[...] += _dot(act, wd_ref[c])

    out = acc_ref[...]
    if final_norm:
        out = _rms(out, gf_ref[...])
    o_ref[...] = out


def _ffn(h, g, wu, wd, cw, cb, gf, final_norm):
    B, T, D = h.shape
    tm = TOK_TILE
    nc, fc = N_FFN_CHUNKS, FFN_CHUNK
    n_slab = 2 * fc // LANES
    tok = pl.BlockSpec((None, tm, D), lambda b, i: (b, i, 0))

    def once(shape):
        zeros = (0,) * len(shape)
        return pl.BlockSpec(shape, lambda *_: zeros, pipeline_mode=pl.Buffered(1))

    return pl.pallas_call(
        functools.partial(_ffn_kernel, final_norm=final_norm),
        grid=(B, T // tm),
        in_specs=[tok, _const_spec((1, D)), once((nc, D, 2 * fc)), once((nc, fc, D)),
                  _const_spec((nc, CONV_WIDTH, 2 * fc)), _const_spec((nc, 1, 2 * fc)),
                  _const_spec((1, D))],
        out_specs=tok,
        out_shape=jax.ShapeDtypeStruct((B, T, D), F32),
        scratch_shapes=[pltpu.VMEM((tm, D), BF16), pltpu.VMEM((tm, D), F32),
                        pltpu.VMEM((2, n_slab, CARRY_ROWS + tm, LANES), F32),
                        pltpu.VMEM((nc, n_slab, CARRY_ROWS, LANES), F32)],
        compiler_params=_params(("parallel", "arbitrary")),
        name="ffn",
    )(h, g, wu, wd, cw, cb, gf)


def _pad_in_proj(w_in):
    parts = jnp.split(w_in, [int(s) for s in np.cumsum(IN_SPLITS)[:-1]], axis=-1)
    parts[4] = jnp.pad(parts[4], ((0, 0), (0, RANK_PAD - GLA_RANK)))
    return jnp.concatenate(parts, axis=-1).astype(BF16)


def _chunk_cols(w):
    r = w.shape[0]
    return w.reshape(r, 2, N_FFN_CHUNKS, FFN_CHUNK).transpose(2, 0, 1, 3).reshape(
        N_FFN_CHUNKS, r, 2 * FFN_CHUNK)


def kernel(x, mem, norm_mix, w_in, w_alpha, b_alpha, gla_gain, rel_bias, sinks, w_out, norm_cross,
           norm_mem, w_q_c, w_k_c, w_v_c, w_o_c, norm_ffn, w_up, conv_w, conv_b, w_down, norm_final):
    depth = w_in.shape[0]
    row = lambda v: v.reshape(1, -1).astype(F32)
    h = x
    for l in range(depth):
        w_pad = _pad_in_proj(w_in[l])
        wal_pad = jnp.pad(w_alpha[l], ((0, RANK_PAD - GLA_RANK), (0, 0)))
        gq, gk, gv, gr, la, sq, sk, sv = _in_proj(h, row(norm_mix[l]), w_pad, wal_pad,
                                                   row(b_alpha[l]))
        o_gla = _gla(gq, gk, gv, gr, la, row(gla_gain[l]))
        o_swa = _swa(sq, sk, sv, rel_bias.astype(F32), sinks[l].astype(F32))
        kc, vc = _mem_kv(mem, row(norm_mem[l]), w_k_c[l].astype(BF16), w_v_c[l].astype(BF16))
        wo = w_out[l].astype(BF16)
        h = _out_cross(h, o_gla, o_swa, kc, vc, wo[:GLA_VW], wo[GLA_VW:], row(norm_cross[l]),
                       w_q_c[l].astype(BF16), w_o_c[l].astype(BF16))
        h = _ffn(h, row(norm_ffn[l]),
                 _chunk_cols(w_up[l].astype(BF16)),
                 w_down[l].astype(BF16).reshape(N_FFN_CHUNKS, FFN_CHUNK, D_MODEL),
                 _chunk_cols(conv_w[l].astype(F32)), _chunk_cols(conv_b[l][None, :].astype(F32)),
                 row(norm_final), final_norm=(l == depth - 1))
    return h
```

```python
import functools
import math

import numpy as np
import jax
import jax.numpy as jnp
from jax import lax
from jax.experimental import pallas as pl
from jax.experimental.pallas import tpu as pltpu

F32 = jnp.float32
BF16 = jnp.bfloat16

D_MODEL = 1024
MEM_LEN = 256
EPS = 1e-6
GLA_HEADS = 4
GLA_DK = 64
GLA_DV = 128
GLA_RANK = 16
GLA_TAU = 16.0
GLA_CHUNK = 64
SWA_HEADS = 8
SWA_KV_HEADS = 2
SWA_DH = 64
WINDOW = 128
REL_BUCKETS = 32
REL_MAX_DIST = 128
CROSS_HEADS = 4
CROSS_DH = D_MODEL // CROSS_HEADS
D_FF = 2816
CONV_WIDTH = 3

GLA_KW = GLA_HEADS * GLA_DK
GLA_VW = GLA_HEADS * GLA_DV
SWA_QW = SWA_HEADS * SWA_DH
SWA_KVW = SWA_KV_HEADS * SWA_DH
IN_SPLITS = (GLA_KW, GLA_KW, GLA_VW, GLA_VW, GLA_RANK, SWA_QW, SWA_KVW, SWA_KVW)

LANES = 128
RANK_PAD = LANES
_PAD_SPLITS = (GLA_KW, GLA_KW, GLA_VW, GLA_VW, RANK_PAD, SWA_QW, SWA_KVW, SWA_KVW)
_OFF = tuple(int(v) for v in np.cumsum((0,) + _PAD_SPLITS))
D_IN_PAD = _OFF[-1]

TOK_TILE = 512
FFN_CHUNK = 256
N_FFN_CHUNKS = D_FF // FFN_CHUNK
CARRY_ROWS = 8
VMEM_LIMIT = 56 * 1024 * 1024

_NT = (((1,), (1,)), ((), ()))
_TN = (((0,), (0,)), ((), ()))


def _rms(x, g):
    return x * lax.rsqrt(jnp.mean(x * x, axis=-1, keepdims=True) + EPS) * g


def _dot(a, b):
    return jnp.dot(a, b, preferred_element_type=F32)


def _split_bf16(x):
    hi = x.astype(BF16)
    return hi, (x - hi.astype(F32)).astype(BF16)


def _silu(x):
    return x * (1.0 / (1.0 + jnp.exp(-x)))


def _const_spec(shape):
    zeros = (0,) * len(shape)
    return pl.BlockSpec(shape, lambda *_: zeros)


def _params(sem):
    return pltpu.CompilerParams(dimension_semantics=sem, vmem_limit_bytes=VMEM_LIMIT)


def _inproj_kernel(x_ref, g_ref, w_ref, wal_ref, bal_ref,
                   gq_ref, gk_ref, gv_ref, gr_ref, la_ref, sq_ref, sk_ref, sv_ref):
    hn = _rms(x_ref[...], g_ref[...]).astype(BF16)
    allp = _dot(hn, w_ref[...])

    def proj(i):
        return allp[:, _OFF[i]:_OFF[i + 1]]

    gq_ref[...] = proj(0).astype(BF16)
    gk_ref[...] = proj(1).astype(BF16)
    gv_ref[...] = proj(2).astype(BF16)
    gr_ref[...] = proj(3).astype(BF16)
    a_low = proj(4)
    a_hi, a_lo = _split_bf16(a_low)
    w_hi, w_lo = _split_bf16(wal_ref[...])
    z = _dot(a_hi, w_hi) + _dot(a_lo, w_hi) + _dot(a_hi, w_lo) + bal_ref[...]
    la_ref[...] = -(jnp.maximum(-z, 0.0) + jnp.log1p(jnp.exp(-jnp.abs(z)))) * (1.0 / GLA_TAU)
    sq_ref[...] = proj(5).astype(BF16)
    sk_ref[...] = proj(6)
    sv_ref[...] = proj(7)


def _in_proj(h, g, w_pad, wal_pad, bal):
    B, T, D = h.shape
    tm = TOK_TILE
    grid = (B, T // tm)

    def tok(width):
        return pl.BlockSpec((None, tm, width), lambda b, i: (b, i, 0))

    def out(width, dt):
        return jax.ShapeDtypeStruct((B, T, width), dt)

    return pl.pallas_call(
        _inproj_kernel,
        grid=grid,
        in_specs=[tok(D), _const_spec((1, D)), _const_spec((D, D_IN_PAD)),
                  _const_spec((RANK_PAD, GLA_KW)), _const_spec((1, GLA_KW))],
        out_specs=[tok(GLA_KW), tok(GLA_KW), tok(GLA_VW), tok(GLA_VW), tok(GLA_KW),
                   tok(SWA_QW), tok(SWA_KVW), tok(SWA_KVW)],
        out_shape=[out(GLA_KW, BF16), out(GLA_KW, BF16), out(GLA_VW, BF16), out(GLA_VW, BF16),
                   out(GLA_KW, F32), out(SWA_QW, BF16), out(SWA_KVW, F32), out(SWA_KVW, F32)],
        compiler_params=_params(("parallel", "parallel")),
        name="in_proj",
    )(h, g, w_pad, wal_pad, bal)


def _gla_kernel(q_ref, k_ref, v_ref, r_ref, la_ref, gain_ref, o_ref,
                st_ref, q4_ref, kv_ref, dec_ref, oi_ref):
    C = GLA_CHUNK
    H = GLA_HEADS
    n_chunks = q_ref.shape[0] // C

    @pl.when(pl.program_id(1) == 0)
    def _():
        st_ref[...] = jnp.zeros_like(st_ref)

    tril = (lax.broadcasted_iota(jnp.int32, (C, C), 0)
            >= lax.broadcasted_iota(jnp.int32, (C, C), 1)).astype(BF16)
    tril2 = jnp.concatenate([tril, tril], axis=1)
    causal4 = ((lax.broadcasted_iota(jnp.int32, (H * C, C), 0) & (C - 1))
               >= lax.broadcasted_iota(jnp.int32, (H * C, C), 1))
    head_of_lane = lax.broadcasted_iota(jnp.int32, (C, GLA_KW), 1) // GLA_DK
    head_of_st_lane = lax.broadcasted_iota(jnp.int32, (GLA_DV, GLA_KW), 1) // GLA_DK
    gain = gain_ref[...]

    chunks = range(n_chunks)
    rows_of = [slice(c * C, (c + 1) * C) for c in chunks]

    bcum = []
    for c in chunks:
        la_hi, la_lo = _split_bf16(la_ref[rows_of[c], :])
        bcum.append(_dot(tril2, jnp.concatenate([la_hi, la_lo], axis=0)))

    k_inv, k_end = [], []
    for c in chunks:
        b_last = bcum[c][C - 1:C, :]
        dec_ref[c] = jnp.exp(b_last)
        q = q_ref[rows_of[c], :].astype(F32) * (GLA_DK ** -0.5)
        k = k_ref[rows_of[c], :].astype(F32)
        q_dec = (q * jnp.exp(bcum[c])).astype(BF16)
        k_inv.append((k * jnp.exp(-bcum[c])).astype(BF16))
        k_end.append((k * jnp.exp(b_last - bcum[c])).astype(BF16))
        q4_ref[c] = jnp.concatenate([jnp.where(head_of_lane == h, q_dec, jnp.zeros_like(q_dec))
                                     for h in range(H)], axis=0)

    att = [lax.dot_general(q4_ref[c], k_inv[c], _NT, preferred_element_type=F32) for c in chunks]
    kvt_all = [lax.dot_general(v_ref[rows_of[c], :], k_end[c], _TN, preferred_element_type=F32)
               for c in chunks]

    for c in chunks:
        att_c = jnp.where(causal4, att[c], 0.0).astype(BF16)
        for h in range(H):
            cols = slice(h * GLA_DV, (h + 1) * GLA_DV)
            oi_ref[rows_of[c], cols] = _dot(att_c[h * C:(h + 1) * C, :], v_ref[rows_of[c], cols])
        kvt = kvt_all[c][(H - 1) * GLA_DV:, :]
        for h in range(H - 2, -1, -1):
            kvt = jnp.where(head_of_st_lane == h, kvt_all[c][h * GLA_DV:(h + 1) * GLA_DV, :], kvt)
        kv_ref[c] = kvt

    st = st_ref[...]
    for c in range(n_chunks):
        rows = slice(c * C, (c + 1) * C)
        inter = lax.dot_general(q4_ref[c], st.astype(BF16), _NT, preferred_element_type=F32)
        st = st * dec_ref[c] + kv_ref[c]
        for h in range(H):
            cols = slice(h * GLA_DV, (h + 1) * GLA_DV)
            o = oi_ref[rows, cols] + inter[h * C:(h + 1) * C, :]
            o = o * lax.rsqrt(jnp.mean(o * o, axis=-1, keepdims=True) + EPS) * gain
            o_ref[rows, cols] = (o * _silu(r_ref[rows, cols].astype(F32))).astype(o_ref.dtype)
    st_ref[...] = st


def _gla(gq, gk, gv, gr, la, gain):
    B, T, _ = gq.shape
    tm = TOK_TILE

    def tok(width):
        return pl.BlockSpec((None, tm, width), lambda b, i: (b, i, 0))

    return pl.pallas_call(
        _gla_kernel,
        grid=(B, T // tm),
        in_specs=[tok(GLA_KW), tok(GLA_KW), tok(GLA_VW), tok(GLA_VW), tok(GLA_KW),
                  _const_spec((1, GLA_DV))],
        out_specs=tok(GLA_VW),
        out_shape=jax.ShapeDtypeStruct((B, T, GLA_VW), BF16),
        scratch_shapes=[pltpu.VMEM((GLA_DV, GLA_KW), F32),
                        pltpu.VMEM((tm // GLA_CHUNK, GLA_HEADS * GLA_CHUNK, GLA_KW), BF16),
                        pltpu.VMEM((tm // GLA_CHUNK, GLA_DV, GLA_KW), F32),
                        pltpu.VMEM((tm // GLA_CHUNK, 1, GLA_KW), F32),
                        pltpu.VMEM((tm, GLA_VW), F32)],
        compiler_params=_params(("parallel", "arbitrary")),
        name="gla",
    )(gq, gk, gv, gr, la, gain)


def _t5_bucket_table():
    L = WINDOW
    dist = (jnp.arange(L)[:, None] + L) - jnp.arange(2 * L)[None, :]
    n = jnp.maximum(dist, 0)
    max_exact = REL_BUCKETS // 2
    nf = jnp.maximum(n, 1).astype(F32)
    large = max_exact + (jnp.log(nf / max_exact) / math.log(REL_MAX_DIST / max_exact)
                         * (REL_BUCKETS - max_exact)).astype(jnp.int32)
    large = jnp.minimum(large, REL_BUCKETS - 1)
    bucket = jnp.where(n < max_exact, n, large)
    return jnp.where((dist >= 0) & (dist < WINDOW), bucket, -1).astype(jnp.int32)


def _swa_kernel(sinks_ref, relb_ref, bucket_ref, q_ref, kc_ref, vc_ref, kp_ref, vp_ref,
                o_ref, bias_ref):
    L = WINDOW
    n_blk = q_ref.shape[0] // L
    pair_w = 2 * SWA_DH
    n_pairs = SWA_QW // pair_w
    pairs_per_kv = n_pairs // SWA_KV_HEADS
    neg_inf = float("-inf")

    @pl.when((pl.program_id(0) == 0) & (pl.program_id(1) == 0))
    def _():
        bucket = bucket_ref[...]
        for h in range(SWA_HEADS):
            def body(b, acc):
                return jnp.where(bucket == b, relb_ref[b, h], acc)
            bias_ref[h] = lax.fori_loop(0, REL_BUCKETS, body, jnp.full((L, 2 * L), neg_inf, F32))

    def dup_heads(cat):
        low = lax.broadcasted_iota(jnp.int32, cat.shape, 1) < SWA_DH
        rolled = pltpu.roll(cat, SWA_DH, 1)
        return (jnp.where(low, cat, rolled).astype(BF16), jnp.where(low, rolled, cat).astype(BF16))

    kd = dup_heads(jnp.concatenate([kp_ref[...], kc_ref[...]], axis=0))
    vd = dup_heads(jnp.concatenate([vp_ref[...], vc_ref[...]], axis=0))

    low = lax.broadcasted_iota(jnp.int32, (L, pair_w), 1) < SWA_DH
    no_prev = jnp.where((pl.program_id(1) == 0)
                        & (lax.broadcasted_iota(jnp.int32, (L, 2 * L), 1) < L), neg_inf, 0.0)

    for j in range(n_blk):
        for p in range(n_pairs):
            g = p // pairs_per_kv
            qp = q_ref[j * L:(j + 1) * L, p * pair_w:(p + 1) * pair_w] * (SWA_DH ** -0.5)
            zero = jnp.zeros_like(qp)
            q2 = jnp.concatenate([jnp.where(low, qp, zero), jnp.where(low, zero, qp)], axis=0)
            keys = kd[g][j * L:(j + 2) * L, :]
            vals = vd[g][j * L:(j + 2) * L, :]
            s2 = lax.dot_general(q2, keys, _NT, preferred_element_type=F32)
            outs = []
            for e in range(2):
                h = 2 * p + e
                s = s2[e * L:(e + 1) * L, :] + bias_ref[h]
                if j == 0:
                    s = s + no_prev
                sink = sinks_ref[h]
                m = jnp.maximum(jnp.max(s, axis=-1, keepdims=True), sink)
                pr = jnp.exp(s - m)
                denom = jnp.sum(pr, axis=-1, keepdims=True) + jnp.exp(sink - m)
                outs.append(_dot(pr.astype(BF16), vals) * (1.0 / denom))
            o_ref[j * L:(j + 1) * L, p * pair_w:(p + 1) * pair_w] = (
                jnp.where(low, outs[0], outs[1]).astype(o_ref.dtype))


def _swa(sq, sk, sv, rel_bias, sinks):
    B, T, _ = sq.shape
    tm = TOK_TILE
    L = WINDOW
    per = tm // L

    def tok(width):
        return pl.BlockSpec((None, tm, width), lambda b, i: (b, i, 0))

    prev = pl.BlockSpec((None, L, SWA_KVW), lambda b, i: (b, jnp.maximum(i * per - 1, 0), 0))
    smem = pl.BlockSpec(memory_space=pltpu.SMEM)
    return pl.pallas_call(
        _swa_kernel,
        grid=(B, T // tm),
        in_specs=[smem, smem, _const_spec((L, 2 * L)), tok(SWA_QW), tok(SWA_KVW), tok(SWA_KVW),
                  prev, prev],
        out_specs=tok(SWA_QW),
        out_shape=jax.ShapeDtypeStruct((B, T, SWA_QW), BF16),
        scratch_shapes=[pltpu.VMEM((SWA_HEADS, L, 2 * L), F32)],
        compiler_params=_params(("arbitrary", "arbitrary")),
        name="swa",
    )(sinks, rel_bias, _t5_bucket_table(), sq, sk, sv, sk, sv)


def _memkv_kernel(mem_ref, g_ref, wk_ref, wv_ref, k_ref, v_ref):
    mn = _rms(mem_ref[...], g_ref[...]).astype(BF16)
    k_ref[...] = _dot(mn, wk_ref[...]).astype(BF16)
    v_ref[...] = _dot(mn, wv_ref[...]).astype(BF16)


def _mem_kv(mem, g, wk, wv):
    B, M, D = mem.shape
    blk = pl.BlockSpec((None, M, D), lambda b: (b, 0, 0))
    return pl.pallas_call(
        _memkv_kernel,
        grid=(B,),
        in_specs=[blk, _const_spec((1, D)), _const_spec((D, D)), _const_spec((D, D))],
        out_specs=[blk, blk],
        out_shape=[jax.ShapeDtypeStruct((B, M, D), BF16)] * 2,
        compiler_params=_params(("parallel",)),
        name="mem_kv",
    )(mem, g, wk, wv)


def _outcross_kernel(h_ref, og_ref, os_ref, kc_ref, vc_ref, wout_ref, g_ref,
                     wq_ref, wo_ref, o_ref):
    mix = jnp.concatenate([og_ref[...], os_ref[...]], axis=1)
    h1 = h_ref[...] + _dot(mix, wout_ref[...])
    hn = _rms(h1, g_ref[...]).astype(BF16)
    q = (_dot(hn, wq_ref[...]) * (CROSS_DH ** -0.5)).astype(BF16)
    heads = []
    for hd in range(CROSS_HEADS):
        cols = slice(hd * CROSS_DH, (hd + 1) * CROSS_DH)
        s = lax.dot_general(q[:, cols], kc_ref[:, cols], _NT, preferred_element_type=F32)
        m = jnp.max(s, axis=-1, keepdims=True)
        p = jnp.exp(s - m)
        denom = jnp.sum(p, axis=-1, keepdims=True)
        heads.append((_dot(p.astype(BF16), vc_ref[:, cols]) * (1.0 / denom)).astype(BF16))
    o_ref[...] = h1 + _dot(jnp.concatenate(heads, axis=1), wo_ref[...])


def _out_cross(h, og, osw, kc, vc, wout, g, wq, wo):
    B, T, D = h.shape
    tm = TOK_TILE

    def tok(width):
        return pl.BlockSpec((None, tm, width), lambda b, i: (b, i, 0))

    memblk = pl.BlockSpec((None, MEM_LEN, D), lambda b, i: (b, 0, 0))
    return pl.pallas_call(
        _outcross_kernel,
        grid=(B, T // tm),
        in_specs=[tok(D), tok(GLA_VW), tok(SWA_QW), memblk, memblk,
                  _const_spec((GLA_VW + SWA_QW, D)), _const_spec((1, D)),
                  _const_spec((D, D)), _const_spec((D, D))],
        out_specs=tok(D),
        out_shape=jax.ShapeDtypeStruct((B, T, D), F32),
        compiler_params=_params(("parallel", "parallel")),
        name="out_cross",
    )(h, og, osw, kc, vc, wout, g, wq, wo)


def _ffn_kernel(h_ref, g_ref, wu_ref, wd_ref, cw_ref, cb_ref, gf_ref,
                o_ref, hn_ref, acc_ref, ubuf_ref, carry_ref, *, final_norm):
    tm = h_ref.shape[0]
    n_slab = 2 * FFN_CHUNK // LANES
    half = n_slab // 2

    @pl.when(pl.program_id(1) == 0)
    def _():
        carry_ref[...] = jnp.zeros_like(carry_ref)

    hn_ref[...] = _rms(h_ref[...], g_ref[...]).astype(BF16)
    acc_ref[...] = h_ref[...]

    def cols_of(c, s):
        start = (s // half) * D_FF + c * FFN_CHUNK + (s % half) * LANES
        return slice(start, start + LANES)

    def up(c):
        hn = hn_ref[...]
        for part in range(2):
            start = part * D_FF + c * FFN_CHUNK
            u = _dot(hn, wu_ref[:, start:start + FFN_CHUNK])
            for j in range(half):
                s = part * half + j
                us = u[:, j * LANES:(j + 1) * LANES]
                ubuf_ref[c % 2, s, 0:CARRY_ROWS, :] = carry_ref[c, s]
                ubuf_ref[c % 2, s, CARRY_ROWS:, :] = us
                carry_ref[c, s] = us[tm - CARRY_ROWS:, :]

    def conv(c, s):
        lanes = cols_of(c, s)
        taps = [ubuf_ref[c % 2, s, CARRY_ROWS - d:CARRY_ROWS - d + tm, :] for d in range(CONV_WIDTH)]
        return (cw_ref[2:3, lanes] * taps[0]
                + (cw_ref[1:2, lanes] * taps[1] + (cw_ref[0:1, lanes] * taps[2] + cb_ref[:, lanes])))

    up(0)
    for c in range(N_FFN_CHUNKS):
        if c + 1 < N_FFN_CHUNKS:
            up(c + 1)
        act = jnp.concatenate([(_silu(conv(c, s)) * conv(c, half + s)).astype(BF16)
                               for s in range(half)], axis=1)
        acc_ref[...] += _dot(act, wd_ref[c * FFN_CHUNK:(c + 1) * FFN_CHUNK, :])

    out = acc_ref[...]
    if final_norm:
        out = _rms(out, gf_ref[...])
    o_ref[...] = out


def _ffn(h, g, wu, wd, cw, cb, gf, final_norm):
    B, T, D = h.shape
    tm = TOK_TILE
    nc, fc = N_FFN_CHUNKS, FFN_CHUNK
    n_slab = 2 * fc // LANES
    tok = pl.BlockSpec((None, tm, D), lambda b, i: (b, i, 0))

    def once(shape):
        zeros = (0,) * len(shape)
        return pl.BlockSpec(shape, lambda *_: zeros, pipeline_mode=pl.Buffered(1))

    return pl.pallas_call(
        functools.partial(_ffn_kernel, final_norm=final_norm),
        grid=(B, T // tm),
        in_specs=[tok, _const_spec((1, D)), once((D, 2 * D_FF)), once((D_FF, D)),
                  _const_spec((CONV_WIDTH, 2 * D_FF)), _const_spec((1, 2 * D_FF)),
                  _const_spec((1, D))],
        out_specs=tok,
        out_shape=jax.ShapeDtypeStruct((B, T, D), F32),
        scratch_shapes=[pltpu.VMEM((tm, D), BF16), pltpu.VMEM((tm, D), F32),
                        pltpu.VMEM((2, n_slab, CARRY_ROWS + tm, LANES), F32),
                        pltpu.VMEM((nc, n_slab, CARRY_ROWS, LANES), F32)],
        compiler_params=_params(("parallel", "arbitrary")),
        name="ffn",
    )(h, g, wu, wd, cw, cb, gf)


def _pad_in_proj(w_in):
    parts = jnp.split(w_in, [int(s) for s in np.cumsum(IN_SPLITS)[:-1]], axis=-1)
    parts[4] = jnp.pad(parts[4], ((0, 0), (0, RANK_PAD - GLA_RANK)))
    return jnp.concatenate(parts, axis=-1).astype(BF16)


def kernel(x, mem, norm_mix, w_in, w_alpha, b_alpha, gla_gain, rel_bias, sinks, w_out, norm_cross,
           norm_mem, w_q_c, w_k_c, w_v_c, w_o_c, norm_ffn, w_up, conv_w, conv_b, w_down, norm_final):
    depth = w_in.shape[0]
    row = lambda v: v.reshape(1, -1).astype(F32)
    h = x
    for l in range(depth):
        w_pad = _pad_in_proj(w_in[l])
        wal_pad = jnp.pad(w_alpha[l], ((0, RANK_PAD - GLA_RANK), (0, 0)))
        gq, gk, gv, gr, la, sq, sk, sv = _in_proj(h, row(norm_mix[l]), w_pad, wal_pad,
                                                   row(b_alpha[l]))
        o_gla = _gla(gq, gk, gv, gr, la, row(gla_gain[l]))
        o_swa = _swa(sq, sk, sv, rel_bias.astype(F32), sinks[l].astype(F32))
        kc, vc = _mem_kv(mem, row(norm_mem[l]), w_k_c[l].astype(BF16), w_v_c[l].astype(BF16))
        h = _out_cross(h, o_gla, o_swa, kc, vc, w_out[l].astype(BF16), row(norm_cross[l]),
                       w_q_c[l].astype(BF16), w_o_c[l].astype(BF16))
        h = _ffn(h, row(norm_ffn[l]), w_up[l].astype(BF16), w_down[l].astype(BF16),
                 conv_w[l].astype(F32), row(conv_b[l]), row(norm_final),
                 final_norm=(l == depth - 1))
    return h
```

```python
import functools
import math

import numpy as np
import jax
import jax.numpy as jnp
from jax import lax
from jax.experimental import pallas as pl
from jax.experimental.pallas import tpu as pltpu

F32 = jnp.float32
BF16 = jnp.bfloat16

D_MODEL = 1024
MEM_LEN = 256
EPS = 1e-6
GLA_HEADS = 4
GLA_DK = 64
GLA_DV = 128
GLA_RANK = 16
GLA_TAU = 16.0
GLA_CHUNK = 64
SWA_HEADS = 8
SWA_KV_HEADS = 2
SWA_DH = 64
WINDOW = 128
REL_BUCKETS = 32
REL_MAX_DIST = 128
CROSS_HEADS = 4
CROSS_DH = D_MODEL // CROSS_HEADS
D_FF = 2816
CONV_WIDTH = 3

GLA_KW = GLA_HEADS * GLA_DK
GLA_VW = GLA_HEADS * GLA_DV
SWA_QW = SWA_HEADS * SWA_DH
SWA_KVW = SWA_KV_HEADS * SWA_DH
IN_SPLITS = (GLA_KW, GLA_KW, GLA_VW, GLA_VW, GLA_RANK, SWA_QW, SWA_KVW, SWA_KVW)

LOG2E = math.log2(math.e)
LANES = 128
RANK_PAD = LANES
_PAD_SPLITS = (GLA_KW, GLA_KW, GLA_VW, GLA_VW, RANK_PAD, SWA_QW, SWA_KVW, SWA_KVW)
_OFF = tuple(int(v) for v in np.cumsum((0,) + _PAD_SPLITS))
D_IN_PAD = _OFF[-1]

TOK_TILE = 512
FFN_CHUNK = 256
N_FFN_CHUNKS = D_FF // FFN_CHUNK
CARRY_ROWS = 8
VMEM_LIMIT = 56 * 1024 * 1024

_NT = (((1,), (1,)), ((), ()))
_TN = (((0,), (0,)), ((), ()))


def _rms(x, g):
    return x * lax.rsqrt(jnp.mean(x * x, axis=-1, keepdims=True) + EPS) * g


def _dot(a, b):
    return jnp.dot(a, b, preferred_element_type=F32)


def _split_bf16(x):
    hi = x.astype(BF16)
    return hi, (x - hi.astype(F32)).astype(BF16)


def _silu(x):
    return x * (1.0 / (1.0 + jnp.exp(-x)))


def _const_spec(shape):
    zeros = (0,) * len(shape)
    return pl.BlockSpec(shape, lambda *_: zeros)


def _params(sem):
    return pltpu.CompilerParams(dimension_semantics=sem, vmem_limit_bytes=VMEM_LIMIT)


def _inproj_kernel(x_ref, g_ref, w_ref, wal_ref, bal_ref,
                   gq_ref, gk_ref, gv_ref, gr_ref, la_ref, sq_ref, sk_ref, sv_ref):
    hn = _rms(x_ref[...], g_ref[...]).astype(BF16)
    allp = _dot(hn, w_ref[...])

    def proj(i):
        return allp[:, _OFF[i]:_OFF[i + 1]]

    gq_ref[...] = proj(0).astype(BF16)
    gk_ref[...] = proj(1).astype(BF16)
    gv_ref[...] = proj(2).astype(BF16)
    gr_ref[...] = proj(3).astype(BF16)
    a_low = proj(4)
    a_hi, a_lo = _split_bf16(a_low)
    w_hi, w_lo = _split_bf16(wal_ref[...])
    z = _dot(a_hi, w_hi) + _dot(a_lo, w_hi) + _dot(a_hi, w_lo) + bal_ref[...]
    la_ref[...] = -(jnp.maximum(-z, 0.0) + jnp.log1p(jnp.exp(-jnp.abs(z)))) * (1.0 / GLA_TAU)
    sq_ref[...] = proj(5).astype(BF16)
    sk_ref[...] = proj(6)
    sv_ref[...] = proj(7)


def _in_proj(h, g, w_pad, wal_pad, bal):
    B, T, D = h.shape
    tm = TOK_TILE
    grid = (B, T // tm)

    def tok(width):
        return pl.BlockSpec((None, tm, width), lambda b, i: (b, i, 0))

    def out(width, dt):
        return jax.ShapeDtypeStruct((B, T, width), dt)

    return pl.pallas_call(
        _inproj_kernel,
        grid=grid,
        in_specs=[tok(D), _const_spec((1, D)), _const_spec((D, D_IN_PAD)),
                  _const_spec((RANK_PAD, GLA_KW)), _const_spec((1, GLA_KW))],
        out_specs=[tok(GLA_KW), tok(GLA_KW), tok(GLA_VW), tok(GLA_VW), tok(GLA_KW),
                   tok(SWA_QW), tok(SWA_KVW), tok(SWA_KVW)],
        out_shape=[out(GLA_KW, BF16), out(GLA_KW, BF16), out(GLA_VW, BF16), out(GLA_VW, BF16),
                   out(GLA_KW, F32), out(SWA_QW, BF16), out(SWA_KVW, F32), out(SWA_KVW, F32)],
        compiler_params=_params(("parallel", "parallel")),
        name="in_proj",
    )(h, g, w_pad, wal_pad, bal)


def _gla_kernel(q_ref, k_ref, v_ref, r_ref, la_ref, gain_ref, o_ref,
                st_ref, q4_ref, kv_ref, dec_ref, oi_ref):
    C = GLA_CHUNK
    H = GLA_HEADS
    n_chunks = q_ref.shape[0] // C

    @pl.when(pl.program_id(1) == 0)
    def _():
        st_ref[...] = jnp.zeros_like(st_ref)

    tril = (lax.broadcasted_iota(jnp.int32, (C, C), 0)
            >= lax.broadcasted_iota(jnp.int32, (C, C), 1)).astype(BF16)
    tril2 = jnp.concatenate([tril, tril], axis=1)
    causal4 = ((lax.broadcasted_iota(jnp.int32, (H * C, C), 0) & (C - 1))
               >= lax.broadcasted_iota(jnp.int32, (H * C, C), 1))
    head_of_lane = lax.broadcasted_iota(jnp.int32, (C, GLA_KW), 1) // GLA_DK
    head_of_st_lane = lax.broadcasted_iota(jnp.int32, (GLA_DV, GLA_KW), 1) // GLA_DK
    gain = gain_ref[...]

    chunks = range(n_chunks)
    rows_of = [slice(c * C, (c + 1) * C) for c in chunks]

    bcum = []
    for c in chunks:
        la_hi, la_lo = _split_bf16(la_ref[rows_of[c], :])
        bcum.append(_dot(tril2, jnp.concatenate([la_hi, la_lo], axis=0)))

    k_inv, k_end = [], []
    for c in chunks:
        b_last = bcum[c][C - 1:C, :]
        dec_ref[c] = jnp.exp(b_last)
        q = q_ref[rows_of[c], :].astype(F32) * (GLA_DK ** -0.5)
        k = k_ref[rows_of[c], :].astype(F32)
        q_dec = (q * jnp.exp(bcum[c])).astype(BF16)
        k_inv.append((k * jnp.exp(-bcum[c])).astype(BF16))
        k_end.append((k * jnp.exp(b_last - bcum[c])).astype(BF16))
        q4_ref[c] = jnp.concatenate([jnp.where(head_of_lane == h, q_dec, jnp.zeros_like(q_dec))
                                     for h in range(H)], axis=0)

    att = [lax.dot_general(q4_ref[c], k_inv[c], _NT, preferred_element_type=F32) for c in chunks]
    kvt_all = [lax.dot_general(v_ref[rows_of[c], :], k_end[c], _TN, preferred_element_type=F32)
               for c in chunks]

    for c in chunks:
        att_c = jnp.where(causal4, att[c], 0.0).astype(BF16)
        for h in range(H):
            cols = slice(h * GLA_DV, (h + 1) * GLA_DV)
            oi_ref[rows_of[c], cols] = _dot(att_c[h * C:(h + 1) * C, :], v_ref[rows_of[c], cols])
        kvt = kvt_all[c][(H - 1) * GLA_DV:, :]
        for h in range(H - 2, -1, -1):
            kvt = jnp.where(head_of_st_lane == h, kvt_all[c][h * GLA_DV:(h + 1) * GLA_DV, :], kvt)
        kv_ref[c] = kvt

    st = st_ref[...]
    for c in range(n_chunks):
        rows = slice(c * C, (c + 1) * C)
        inter = lax.dot_general(q4_ref[c], st.astype(BF16), _NT, preferred_element_type=F32)
        st = st * dec_ref[c] + kv_ref[c]
        for h in range(H):
            cols = slice(h * GLA_DV, (h + 1) * GLA_DV)
            o = oi_ref[rows, cols] + inter[h * C:(h + 1) * C, :]
            o = o * lax.rsqrt(jnp.mean(o * o, axis=-1, keepdims=True) + EPS) * gain
            o_ref[rows, cols] = (o * _silu(r_ref[rows, cols].astype(F32))).astype(o_ref.dtype)
    st_ref[...] = st


def _gla(gq, gk, gv, gr, la, gain):
    B, T, _ = gq.shape
    tm = TOK_TILE

    def tok(width):
        return pl.BlockSpec((None, tm, width), lambda b, i: (b, i, 0))

    return pl.pallas_call(
        _gla_kernel,
        grid=(B, T // tm),
        in_specs=[tok(GLA_KW), tok(GLA_KW), tok(GLA_VW), tok(GLA_VW), tok(GLA_KW),
                  _const_spec((1, GLA_DV))],
        out_specs=tok(GLA_VW),
        out_shape=jax.ShapeDtypeStruct((B, T, GLA_VW), BF16),
        scratch_shapes=[pltpu.VMEM((GLA_DV, GLA_KW), F32),
                        pltpu.VMEM((tm // GLA_CHUNK, GLA_HEADS * GLA_CHUNK, GLA_KW), BF16),
                        pltpu.VMEM((tm // GLA_CHUNK, GLA_DV, GLA_KW), F32),
                        pltpu.VMEM((tm // GLA_CHUNK, 1, GLA_KW), F32),
                        pltpu.VMEM((tm, GLA_VW), F32)],
        compiler_params=_params(("parallel", "arbitrary")),
        name="gla",
    )(gq, gk, gv, gr, la, gain)


def _t5_bucket_table():
    L = WINDOW
    dist = (jnp.arange(L)[:, None] + L) - jnp.arange(2 * L)[None, :]
    n = jnp.maximum(dist, 0)
    max_exact = REL_BUCKETS // 2
    nf = jnp.maximum(n, 1).astype(F32)
    large = max_exact + (jnp.log(nf / max_exact) / math.log(REL_MAX_DIST / max_exact)
                         * (REL_BUCKETS - max_exact)).astype(jnp.int32)
    large = jnp.minimum(large, REL_BUCKETS - 1)
    bucket = jnp.where(n < max_exact, n, large)
    return jnp.where((dist >= 0) & (dist < WINDOW), bucket, -1).astype(jnp.int32)


def _swa_kernel(sinks_ref, relb_ref, bucket_ref, q_ref, kc_ref, vc_ref, kp_ref, vp_ref,
                o_ref, bias_ref, s_ref, pr_ref):
    L = WINDOW
    n_blk = q_ref.shape[0] // L
    pair_w = 2 * SWA_DH
    n_pairs = SWA_QW // pair_w
    pairs_per_kv = n_pairs // SWA_KV_HEADS
    neg_inf = float("-inf")

    @pl.when((pl.program_id(0) == 0) & (pl.program_id(1) == 0))
    def _():
        bucket = bucket_ref[...]
        for h in range(SWA_HEADS):
            def body(b, acc):
                return jnp.where(bucket == b, relb_ref[b, h] * LOG2E, acc)
            bias_ref[h] = lax.fori_loop(0, REL_BUCKETS, body, jnp.full((L, 2 * L), neg_inf, F32))

    def dup_heads(cat):
        low = lax.broadcasted_iota(jnp.int32, cat.shape, 1) < SWA_DH
        rolled = pltpu.roll(cat, SWA_DH, 1)
        return low, (jnp.where(low, cat, rolled), jnp.where(low, rolled, cat))

    _, kdup = dup_heads(jnp.concatenate([kp_ref[...], kc_ref[...]], axis=0))
    low_kv, vdup = dup_heads(jnp.concatenate([vp_ref[...], vc_ref[...]], axis=0))
    kd = [kg.astype(BF16) for kg in kdup]
    vd = [[jnp.where(low_kv, vg, 1.0).astype(BF16), jnp.where(low_kv, 1.0, vg).astype(BF16)]
          for vg in vdup]

    low = lax.broadcasted_iota(jnp.int32, (L, pair_w), 1) < SWA_DH
    no_prev = jnp.where((pl.program_id(1) == 0)
                        & (lax.broadcasted_iota(jnp.int32, (L, 2 * L), 1) < L), neg_inf, 0.0)

    items = [(j, p) for j in range(n_blk) for p in range(n_pairs)]

    for t, (j, p) in enumerate(items):
        qp = q_ref[j * L:(j + 1) * L, p * pair_w:(p + 1) * pair_w] * (SWA_DH ** -0.5 * LOG2E)
        zero = jnp.zeros_like(qp)
        q2 = jnp.concatenate([jnp.where(low, qp, zero), jnp.where(low, zero, qp)], axis=0)
        keys = kd[p // pairs_per_kv][j * L:(j + 2) * L, :]
        s_ref[t] = lax.dot_general(q2, keys, _NT, preferred_element_type=F32)

    def scores(t, e):
        j, p = items[t]
        s = s_ref[t, e * L:(e + 1) * L, :] + bias_ref[2 * p + e]
        return s + no_prev if j == 0 else s

    sinks = [sinks_ref[h] * LOG2E for h in range(SWA_HEADS)]
    row_max = {}
    for t, (j, p) in enumerate(items):
        for e in range(2):
            row_max[t, e] = jnp.maximum(jnp.max(scores(t, e), axis=-1, keepdims=True),
                                        sinks[2 * p + e])

    for t in range(len(items)):
        for e in range(2):
            pr_ref[2 * t + e] = jnp.exp2(scores(t, e) - row_max[t, e]).astype(BF16)

    for t, (j, p) in enumerate(items):
        vals = vd[p // pairs_per_kv]
        outs = [_dot(pr_ref[2 * t + e], vals[e][j * L:(j + 2) * L, :]) for e in range(2)]
        sink_terms = [jnp.exp2(sinks[2 * p + e] - row_max[t, e]) for e in range(2)]
        numer = jnp.where(low, outs[0], outs[1])
        denom = (pltpu.roll(jnp.where(low, outs[1], outs[0]), SWA_DH, 1)
                 + jnp.where(low, sink_terms[0], sink_terms[1]))
        o_ref[j * L:(j + 1) * L, p * pair_w:(p + 1) * pair_w] = (
            numer * (1.0 / denom)).astype(o_ref.dtype)


def _swa(sq, sk, sv, rel_bias, sinks):
    B, T, _ = sq.shape
    tm = TOK_TILE
    L = WINDOW
    per = tm // L

    def tok(width):
        return pl.BlockSpec((None, tm, width), lambda b, i: (b, i, 0))

    prev = pl.BlockSpec((None, L, SWA_KVW), lambda b, i: (b, jnp.maximum(i * per - 1, 0), 0))
    smem = pl.BlockSpec(memory_space=pltpu.SMEM)
    return pl.pallas_call(
        _swa_kernel,
        grid=(B, T // tm),
        in_specs=[smem, smem, _const_spec((L, 2 * L)), tok(SWA_QW), tok(SWA_KVW), tok(SWA_KVW),
                  prev, prev],
        out_specs=tok(SWA_QW),
        out_shape=jax.ShapeDtypeStruct((B, T, SWA_QW), BF16),
        scratch_shapes=[pltpu.VMEM((SWA_HEADS, L, 2 * L), F32),
                        pltpu.VMEM((per * SWA_HEADS // 2, 2 * L, 2 * L), F32),
                        pltpu.VMEM((per * SWA_HEADS, L, 2 * L), BF16)],
        compiler_params=_params(("arbitrary", "arbitrary")),
        name="swa",
    )(sinks, rel_bias, _t5_bucket_table(), sq, sk, sv, sk, sv)


def _memkv_kernel(mem_ref, g_ref, wk_ref, wv_ref, k_ref, v_ref):
    mn = _rms(mem_ref[...], g_ref[...]).astype(BF16)
    k_ref[...] = _dot(mn, wk_ref[...]).astype(BF16)
    v_ref[...] = _dot(mn, wv_ref[...]).astype(BF16)


def _mem_kv(mem, g, wk, wv):
    B, M, D = mem.shape
    blk = pl.BlockSpec((None, M, D), lambda b: (b, 0, 0))
    return pl.pallas_call(
        _memkv_kernel,
        grid=(B,),
        in_specs=[blk, _const_spec((1, D)), _const_spec((D, D)), _const_spec((D, D))],
        out_specs=[blk, blk],
        out_shape=[jax.ShapeDtypeStruct((B, M, D), BF16)] * 2,
        compiler_params=_params(("parallel",)),
        name="mem_kv",
    )(mem, g, wk, wv)


def _outcross_kernel(h_ref, og_ref, os_ref, kc_ref, vc_ref, wout_ref, g_ref,
                     wq_ref, wo_ref, o_ref):
    mix = jnp.concatenate([og_ref[...], os_ref[...]], axis=1)
    h1 = h_ref[...] + _dot(mix, wout_ref[...])
    hn = _rms(h1, g_ref[...]).astype(BF16)
    q = (_dot(hn, wq_ref[...]) * (CROSS_DH ** -0.5)).astype(BF16)
    heads = []
    for hd in range(CROSS_HEADS):
        cols = slice(hd * CROSS_DH, (hd + 1) * CROSS_DH)
        s = lax.dot_general(q[:, cols], kc_ref[:, cols], _NT, preferred_element_type=F32)
        m = jnp.max(s, axis=-1, keepdims=True)
        p = jnp.exp(s - m)
        denom = jnp.sum(p, axis=-1, keepdims=True)
        heads.append((_dot(p.astype(BF16), vc_ref[:, cols]) * (1.0 / denom)).astype(BF16))
    o_ref[...] = h1 + _dot(jnp.concatenate(heads, axis=1), wo_ref[...])


def _out_cross(h, og, osw, kc, vc, wout, g, wq, wo):
    B, T, D = h.shape
    tm = TOK_TILE

    def tok(width):
        return pl.BlockSpec((None, tm, width), lambda b, i: (b, i, 0))

    memblk = pl.BlockSpec((None, MEM_LEN, D), lambda b, i: (b, 0, 0))
    return pl.pallas_call(
        _outcross_kernel,
        grid=(B, T // tm),
        in_specs=[tok(D), tok(GLA_VW), tok(SWA_QW), memblk, memblk,
                  _const_spec((GLA_VW + SWA_QW, D)), _const_spec((1, D)),
                  _const_spec((D, D)), _const_spec((D, D))],
        out_specs=tok(D),
        out_shape=jax.ShapeDtypeStruct((B, T, D), F32),
        compiler_params=_params(("parallel", "parallel")),
        name="out_cross",
    )(h, og, osw, kc, vc, wout, g, wq, wo)


def _ffn_kernel(h_ref, g_ref, wu_ref, wd_ref, cw_ref, cb_ref, gf_ref,
                o_ref, hn_ref, acc_ref, ubuf_ref, carry_ref, *, final_norm):
    tm = h_ref.shape[0]
    n_slab = 2 * FFN_CHUNK // LANES
    half = n_slab // 2

    @pl.when(pl.program_id(1) == 0)
    def _():
        carry_ref[...] = jnp.zeros_like(carry_ref)

    hn_ref[...] = _rms(h_ref[...], g_ref[...]).astype(BF16)
    acc_ref[...] = h_ref[...]

    def cols_of(c, s):
        start = (s // half) * D_FF + c * FFN_CHUNK + (s % half) * LANES
        return slice(start, start + LANES)

    def up(c):
        hn = hn_ref[...]
        for part in range(2):
            start = part * D_FF + c * FFN_CHUNK
            u = _dot(hn, wu_ref[:, start:start + FFN_CHUNK])
            for j in range(half):
                s = part * half + j
                us = u[:, j * LANES:(j + 1) * LANES]
                ubuf_ref[c % 2, s, 0:CARRY_ROWS, :] = carry_ref[c, s]
                ubuf_ref[c % 2, s, CARRY_ROWS:, :] = us
                carry_ref[c, s] = us[tm - CARRY_ROWS:, :]

    def conv(c, s):
        lanes = cols_of(c, s)
        taps = [ubuf_ref[c % 2, s, CARRY_ROWS - d:CARRY_ROWS - d + tm, :] for d in range(CONV_WIDTH)]
        return (cw_ref[2:3, lanes] * taps[0]
                + (cw_ref[1:2, lanes] * taps[1] + (cw_ref[0:1, lanes] * taps[2] + cb_ref[:, lanes])))

    up(0)
    for c in range(N_FFN_CHUNKS):
        if c + 1 < N_FFN_CHUNKS:
            up(c + 1)
        act = jnp.concatenate([(_silu(conv(c, s)) * conv(c, half + s)).astype(BF16)
                               for s in range(half)], axis=1)
        acc_ref[...] += _dot(act, wd_ref[c * FFN_CHUNK:(c + 1) * FFN_CHUNK, :])

    out = acc_ref[...]
    if final_norm:
        out = _rms(out, gf_ref[...])
    o_ref[...] = out


def _ffn(h, g, wu, wd, cw, cb, gf, final_norm):
    B, T, D = h.shape
    tm = TOK_TILE
    nc, fc = N_FFN_CHUNKS, FFN_CHUNK
    n_slab = 2 * fc // LANES
    tok = pl.BlockSpec((None, tm, D), lambda b, i: (b, i, 0))

    def once(shape):
        zeros = (0,) * len(shape)
        return pl.BlockSpec(shape, lambda *_: zeros, pipeline_mode=pl.Buffered(1))

    return pl.pallas_call(
        functools.partial(_ffn_kernel, final_norm=final_norm),
        grid=(B, T // tm),
        in_specs=[tok, _const_spec((1, D)), once((D, 2 * D_FF)), once((D_FF, D)),
                  _const_spec((CONV_WIDTH, 2 * D_FF)), _const_spec((1, 2 * D_FF)),
                  _const_spec((1, D))],
        out_specs=tok,
        out_shape=jax.ShapeDtypeStruct((B, T, D), F32),
        scratch_shapes=[pltpu.VMEM((tm, D), BF16), pltpu.VMEM((tm, D), F32),
                        pltpu.VMEM((2, n_slab, CARRY_ROWS + tm, LANES), F32),
                        pltpu.VMEM((nc, n_slab, CARRY_ROWS, LANES), F32)],
        compiler_params=_params(("parallel", "arbitrary")),
        name="ffn",
    )(h, g, wu, wd, cw, cb, gf)


def _pad_in_proj(w_in):
    parts = jnp.split(w_in, [int(s) for s in np.cumsum(IN_SPLITS)[:-1]], axis=-1)
    parts[4] = jnp.pad(parts[4], ((0, 0), (0, RANK_PAD - GLA_RANK)))
    return jnp.concatenate(parts, axis=-1).astype(BF16)


def kernel(x, mem, norm_mix, w_in, w_alpha, b_alpha, gla_gain, rel_bias, sinks, w_out, norm_cross,
           norm_mem, w_q_c, w_k_c, w_v_c, w_o_c, norm_ffn, w_up, conv_w, conv_b, w_down, norm_final):
    depth = w_in.shape[0]
    row = lambda v: v.reshape(1, -1).astype(F32)
    h = x
    for l in range(depth):
        w_pad = _pad_in_proj(w_in[l])
        wal_pad = jnp.pad(w_alpha[l], ((0, RANK_PAD - GLA_RANK), (0, 0)))
        gq, gk, gv, gr, la, sq, sk, sv = _in_proj(h, row(norm_mix[l]), w_pad, wal_pad,
                                                   row(b_alpha[l]))
        o_gla = _gla(gq, gk, gv, gr, la, row(gla_gain[l]))
        o_swa = _swa(sq, sk, sv, rel_bias.astype(F32), sinks[l].astype(F32))
        kc, vc = _mem_kv(mem, row(norm_mem[l]), w_k_c[l].astype(BF16), w_v_c[l].astype(BF16))
        h = _out_cross(h, o_gla, o_swa, kc, vc, w_out[l].astype(BF16), row(norm_cross[l]),
                       w_q_c[l].astype(BF16), w_o_c[l].astype(BF16))
        h = _ffn(h, row(norm_ffn[l]), w_up[l].astype(BF16), w_down[l].astype(BF16),
                 conv_w[l].astype(F32), row(conv_b[l]), row(norm_final),
                 final_norm=(l == depth - 1))
    return h
```

```python
import functools
import math

import numpy as np
import jax
import jax.numpy as jnp
from jax import lax
from jax.experimental import pallas as pl
from jax.experimental.pallas import tpu as pltpu

F32 = jnp.float32
BF16 = jnp.bfloat16

D_MODEL = 1024
MEM_LEN = 256
EPS = 1e-6
GLA_HEADS = 4
GLA_DK = 64
GLA_DV = 128
GLA_RANK = 16
GLA_TAU = 16.0
GLA_CHUNK = 64
SWA_HEADS = 8
SWA_KV_HEADS = 2
SWA_DH = 64
WINDOW = 128
REL_BUCKETS = 32
REL_MAX_DIST = 128
CROSS_HEADS = 4
CROSS_DH = D_MODEL // CROSS_HEADS
D_FF = 2816
CONV_WIDTH = 3

GLA_KW = GLA_HEADS * GLA_DK
GLA_VW = GLA_HEADS * GLA_DV
SWA_QW = SWA_HEADS * SWA_DH
SWA_KVW = SWA_KV_HEADS * SWA_DH
IN_SPLITS = (GLA_KW, GLA_KW, GLA_VW, GLA_VW, GLA_RANK, SWA_QW, SWA_KVW, SWA_KVW)

LOG2E = math.log2(math.e)
LANES = 128
RANK_PAD = LANES
_PAD_SPLITS = (GLA_KW, GLA_KW, GLA_VW, GLA_VW, RANK_PAD, SWA_QW, SWA_KVW, SWA_KVW)
_OFF = tuple(int(v) for v in np.cumsum((0,) + _PAD_SPLITS))
D_IN_PAD = _OFF[-1]

TOK_TILE = 512
FFN_CHUNK = 256
N_FFN_CHUNKS = D_FF // FFN_CHUNK
CARRY_ROWS = 8
VMEM_LIMIT = 56 * 1024 * 1024

_NT = (((1,), (1,)), ((), ()))
_TN = (((0,), (0,)), ((), ()))


def _rms(x, g):
    return x * lax.rsqrt(jnp.mean(x * x, axis=-1, keepdims=True) + EPS) * g


def _dot(a, b):
    return jnp.dot(a, b, preferred_element_type=F32)


def _split_bf16(x):
    hi = x.astype(BF16)
    return hi, (x - hi.astype(F32)).astype(BF16)


def _silu(x):
    return x * (1.0 / (1.0 + jnp.exp(-x)))


def _const_spec(shape):
    zeros = (0,) * len(shape)
    return pl.BlockSpec(shape, lambda *_: zeros)


def _params(sem):
    return pltpu.CompilerParams(dimension_semantics=sem, vmem_limit_bytes=VMEM_LIMIT)


def _t5_bucket_table():
    L = WINDOW
    dist = (jnp.arange(L)[:, None] + L) - jnp.arange(2 * L)[None, :]
    n = jnp.maximum(dist, 0)
    max_exact = REL_BUCKETS // 2
    nf = jnp.maximum(n, 1).astype(F32)
    large = max_exact + (jnp.log(nf / max_exact) / math.log(REL_MAX_DIST / max_exact)
                         * (REL_BUCKETS - max_exact)).astype(jnp.int32)
    large = jnp.minimum(large, REL_BUCKETS - 1)
    bucket = jnp.where(n < max_exact, n, large)
    return jnp.where((dist >= 0) & (dist < WINDOW), bucket, -1).astype(jnp.int32)


def _inproj_swa_kernel(sinks_ref, relb_ref, bucket_ref, x_ref, g_ref, w_ref, wal_ref, bal_ref,
                       gq_ref, gk_ref, gv_ref, gr_ref, la_ref, o_ref,
                       bias_ref, kv_prev_ref, s_ref, pr_ref):
    L = WINDOW
    tm = x_ref.shape[0]
    n_blk = tm // L
    pair_w = 2 * SWA_DH
    n_pairs = SWA_QW // pair_w
    pairs_per_kv = n_pairs // SWA_KV_HEADS
    neg_inf = float("-inf")

    @pl.when((pl.program_id(0) == 0) & (pl.program_id(1) == 0))
    def _():
        bucket = bucket_ref[...]
        for h in range(SWA_HEADS):
            def body(b, acc):
                return jnp.where(bucket == b, relb_ref[b, h] * LOG2E, acc)
            bias_ref[h] = lax.fori_loop(0, REL_BUCKETS, body, jnp.full((L, 2 * L), neg_inf, F32))

    @pl.when(pl.program_id(1) == 0)
    def _():
        kv_prev_ref[...] = jnp.zeros_like(kv_prev_ref)

    x = x_ref[...]
    xg = (x * g_ref[...]).astype(BF16)
    inv_rms = jnp.broadcast_to(lax.rsqrt(jnp.mean(x * x, axis=-1, keepdims=True) + EPS),
                               (tm, LANES))

    def proj(lo, hi):
        res = _dot(xg, w_ref[:, _OFF[lo]:_OFF[hi]])
        return jnp.concatenate([res[:, c:c + LANES] * inv_rms
                                for c in range(0, res.shape[1], LANES)], axis=1)

    swa = proj(5, 8)
    sq = (swa[:, :SWA_QW] * (SWA_DH ** -0.5 * LOG2E)).astype(BF16)
    k_cat = jnp.concatenate([kv_prev_ref[0], swa[:, SWA_QW:SWA_QW + SWA_KVW]], axis=0)
    v_cat = jnp.concatenate([kv_prev_ref[1], swa[:, SWA_QW + SWA_KVW:]], axis=0)
    kv_prev_ref[0] = k_cat[tm:, :]
    kv_prev_ref[1] = v_cat[tm:, :]

    def dup_heads(cat):
        low = lax.broadcasted_iota(jnp.int32, cat.shape, 1) < SWA_DH
        rolled = pltpu.roll(cat, SWA_DH, 1)
        return low, (jnp.where(low, cat, rolled), jnp.where(low, rolled, cat))

    _, kdup = dup_heads(k_cat)
    low_kv, vdup = dup_heads(v_cat)
    kd = [kg.astype(BF16) for kg in kdup]
    vd = [[jnp.where(low_kv, vg, 1.0).astype(BF16), jnp.where(low_kv, 1.0, vg).astype(BF16)]
          for vg in vdup]

    low = lax.broadcasted_iota(jnp.int32, (L, pair_w), 1) < SWA_DH
    no_prev = jnp.where((pl.program_id(1) == 0)
                        & (lax.broadcasted_iota(jnp.int32, (L, 2 * L), 1) < L), neg_inf, 0.0)
    items = [(j, p) for j in range(n_blk) for p in range(n_pairs)]
    sinks = [sinks_ref[h] * LOG2E for h in range(SWA_HEADS)]
    row_max = {}

    def scores(t, e):
        j, p = items[t]
        s = s_ref[t, e * L:(e + 1) * L, :] + bias_ref[2 * p + e]
        return s + no_prev if j == 0 else s

    def swa_score_dots():
        for t, (j, p) in enumerate(items):
            qp = sq[j * L:(j + 1) * L, p * pair_w:(p + 1) * pair_w]
            zero = jnp.zeros_like(qp)
            q2 = jnp.concatenate([jnp.where(low, qp, zero), jnp.where(low, zero, qp)], axis=0)
            keys = kd[p // pairs_per_kv][j * L:(j + 2) * L, :]
            s_ref[t] = lax.dot_general(q2, keys, _NT, preferred_element_type=F32)

    def swa_row_max():
        for t, (j, p) in enumerate(items):
            for e in range(2):
                row_max[t, e] = jnp.maximum(jnp.max(scores(t, e), axis=-1, keepdims=True),
                                            sinks[2 * p + e])

    def swa_exp():
        for t in range(len(items)):
            for e in range(2):
                pr_ref[2 * t + e] = jnp.exp2(scores(t, e) - row_max[t, e]).astype(BF16)

    def swa_out():
        for t, (j, p) in enumerate(items):
            vals = vd[p // pairs_per_kv]
            outs = [_dot(pr_ref[2 * t + e], vals[e][j * L:(j + 2) * L, :]) for e in range(2)]
            sink_terms = [jnp.exp2(sinks[2 * p + e] - row_max[t, e]) for e in range(2)]
            numer = jnp.where(low, outs[0], outs[1])
            denom = (pltpu.roll(jnp.where(low, outs[1], outs[0]), SWA_DH, 1)
                     + jnp.where(low, sink_terms[0], sink_terms[1]))
            o_ref[j * L:(j + 1) * L, p * pair_w:(p + 1) * pair_w] = (
                numer * (1.0 / denom)).astype(o_ref.dtype)

    a_low = proj(4, 5)
    swa_score_dots()
    a_hi, a_lo = _split_bf16(a_low)
    w_hi, w_lo = _split_bf16(wal_ref[...])
    z = _dot(a_hi, w_hi) + _dot(a_lo, w_hi) + _dot(a_hi, w_lo) + bal_ref[...]
    la_ref[...] = -(jnp.maximum(-z, 0.0) + jnp.log1p(jnp.exp(-jnp.abs(z)))) * (1.0 / GLA_TAU)
    gq_ref[...] = proj(0, 1).astype(BF16)
    swa_row_max()
    gk_ref[...] = proj(1, 2).astype(BF16)
    gv_ref[...] = proj(2, 3).astype(BF16)
    swa_exp()
    swa_out()
    gr_ref[...] = proj(3, 4).astype(BF16)


def _in_proj_swa(h, g, w_pad, wal_pad, bal, rel_bias, sinks):
    B, T, D = h.shape
    tm = TOK_TILE
    L = WINDOW
    per = tm // L

    def tok(width):
        return pl.BlockSpec((None, tm, width), lambda b, i: (b, i, 0))

    def out(width, dt):
        return jax.ShapeDtypeStruct((B, T, width), dt)

    smem = pl.BlockSpec(memory_space=pltpu.SMEM)
    return pl.pallas_call(
        _inproj_swa_kernel,
        grid=(B, T // tm),
        in_specs=[smem, smem, _const_spec((L, 2 * L)),
                  tok(D), _const_spec((1, D)), _const_spec((D, D_IN_PAD)),
                  _const_spec((RANK_PAD, GLA_KW)), _const_spec((1, GLA_KW))],
        out_specs=[tok(GLA_KW), tok(GLA_KW), tok(GLA_VW), tok(GLA_VW), tok(GLA_KW), tok(SWA_QW)],
        out_shape=[out(GLA_KW, BF16), out(GLA_KW, BF16), out(GLA_VW, BF16), out(GLA_VW, BF16),
                   out(GLA_KW, F32), out(SWA_QW, BF16)],
        scratch_shapes=[pltpu.VMEM((SWA_HEADS, L, 2 * L), F32),
                        pltpu.VMEM((2, L, SWA_KVW), F32),
                        pltpu.VMEM((per * SWA_HEADS // 2, 2 * L, 2 * L), F32),
                        pltpu.VMEM((per * SWA_HEADS, L, 2 * L), BF16)],
        compiler_params=_params(("arbitrary", "arbitrary")),
        name="in_proj_swa",
    )(sinks, rel_bias, _t5_bucket_table(), h, g, w_pad, wal_pad, bal)


def _gla_kernel(q_ref, k_ref, v_ref, r_ref, la_ref, gain_ref, o_ref,
                st_ref, q4_ref, kv_ref, dec_ref, oi_ref):
    C = GLA_CHUNK
    H = GLA_HEADS
    n_chunks = q_ref.shape[0] // C

    @pl.when(pl.program_id(1) == 0)
    def _():
        st_ref[...] = jnp.zeros_like(st_ref)

    tril = (lax.broadcasted_iota(jnp.int32, (C, C), 0)
            >= lax.broadcasted_iota(jnp.int32, (C, C), 1)).astype(BF16)
    tril2 = jnp.concatenate([tril, tril], axis=1)
    causal4 = ((lax.broadcasted_iota(jnp.int32, (H * C, C), 0) & (C - 1))
               >= lax.broadcasted_iota(jnp.int32, (H * C, C), 1))
    head_of_lane = lax.broadcasted_iota(jnp.int32, (C, GLA_KW), 1) // GLA_DK
    head_of_st_lane = lax.broadcasted_iota(jnp.int32, (GLA_DV, GLA_KW), 1) // GLA_DK
    gain = gain_ref[...]

    chunks = range(n_chunks)
    rows_of = [slice(c * C, (c + 1) * C) for c in chunks]

    bcum = []
    for c in chunks:
        la_hi, la_lo = _split_bf16(la_ref[rows_of[c], :])
        bcum.append(_dot(tril2, jnp.concatenate([la_hi, la_lo], axis=0)))

    k_inv, k_end = [], []
    for c in chunks:
        b_last = bcum[c][C - 1:C, :]
        dec_ref[c] = jnp.exp(b_last)
        q = q_ref[rows_of[c], :].astype(F32) * (GLA_DK ** -0.5)
        k = k_ref[rows_of[c], :].astype(F32)
        q_dec = (q * jnp.exp(bcum[c])).astype(BF16)
        k_inv.append((k * jnp.exp(-bcum[c])).astype(BF16))
        k_end.append((k * jnp.exp(b_last - bcum[c])).astype(BF16))
        q4_ref[c] = jnp.concatenate([jnp.where(head_of_lane == h, q_dec, jnp.zeros_like(q_dec))
                                     for h in range(H)], axis=0)

    att = [lax.dot_general(q4_ref[c], k_inv[c], _NT, preferred_element_type=F32) for c in chunks]
    kvt_all = [lax.dot_general(v_ref[rows_of[c], :], k_end[c], _TN, preferred_element_type=F32)
               for c in chunks]

    for c in chunks:
        att_c = jnp.where(causal4, att[c], 0.0).astype(BF16)
        for h in range(H):
            cols = slice(h * GLA_DV, (h + 1) * GLA_DV)
            oi_ref[rows_of[c], cols] = _dot(att_c[h * C:(h + 1) * C, :], v_ref[rows_of[c], cols])
        kvt = kvt_all[c][(H - 1) * GLA_DV:, :]
        for h in range(H - 2, -1, -1):
            kvt = jnp.where(head_of_st_lane == h, kvt_all[c][h * GLA_DV:(h + 1) * GLA_DV, :], kvt)
        kv_ref[c] = kvt

    st = st_ref[...]
    for c in range(n_chunks):
        rows = slice(c * C, (c + 1) * C)
        inter = lax.dot_general(q4_ref[c], st.astype(BF16), _NT, preferred_element_type=F32)
        st = st * dec_ref[c] + kv_ref[c]
        for h in range(H):
            cols = slice(h * GLA_DV, (h + 1) * GLA_DV)
            o = oi_ref[rows, cols] + inter[h * C:(h + 1) * C, :]
            o = o * lax.rsqrt(jnp.mean(o * o, axis=-1, keepdims=True) + EPS) * gain
            o_ref[rows, cols] = (o * _silu(r_ref[rows, cols].astype(F32))).astype(o_ref.dtype)
    st_ref[...] = st


def _gla(gq, gk, gv, gr, la, gain):
    B, T, _ = gq.shape
    tm = TOK_TILE

    def tok(width):
        return pl.BlockSpec((None, tm, width), lambda b, i: (b, i, 0))

    return pl.pallas_call(
        _gla_kernel,
        grid=(B, T // tm),
        in_specs=[tok(GLA_KW), tok(GLA_KW), tok(GLA_VW), tok(GLA_VW), tok(GLA_KW),
                  _const_spec((1, GLA_DV))],
        out_specs=tok(GLA_VW),
        out_shape=jax.ShapeDtypeStruct((B, T, GLA_VW), BF16),
        scratch_shapes=[pltpu.VMEM((GLA_DV, GLA_KW), F32),
                        pltpu.VMEM((tm // GLA_CHUNK, GLA_HEADS * GLA_CHUNK, GLA_KW), BF16),
                        pltpu.VMEM((tm // GLA_CHUNK, GLA_DV, GLA_KW), F32),
                        pltpu.VMEM((tm // GLA_CHUNK, 1, GLA_KW), F32),
                        pltpu.VMEM((tm, GLA_VW), F32)],
        compiler_params=_params(("parallel", "arbitrary")),
        name="gla",
    )(gq, gk, gv, gr, la, gain)


def _memkv_kernel(mem_ref, g_ref, wk_ref, wv_ref, k_ref, v_ref):
    mn = _rms(mem_ref[...], g_ref[...]).astype(BF16)
    k_ref[...] = _dot(mn, wk_ref[...]).astype(BF16)
    v_ref[...] = _dot(mn, wv_ref[...]).astype(BF16)


def _mem_kv(mem, g, wk, wv):
    B, M, D = mem.shape
    blk = pl.BlockSpec((None, M, D), lambda b: (b, 0, 0))
    return pl.pallas_call(
        _memkv_kernel,
        grid=(B,),
        in_specs=[blk, _const_spec((1, D)), _const_spec((D, D)), _const_spec((D, D))],
        out_specs=[blk, blk],
        out_shape=[jax.ShapeDtypeStruct((B, M, D), BF16)] * 2,
        compiler_params=_params(("parallel",)),
        name="mem_kv",
    )(mem, g, wk, wv)


def _outcross_kernel(h_ref, og_ref, os_ref, kc_ref, vc_ref, wout_ref, g_ref,
                     wq_ref, wo_ref, o_ref):
    mix = jnp.concatenate([og_ref[...], os_ref[...]], axis=1)
    h1 = h_ref[...] + _dot(mix, wout_ref[...])
    hn = _rms(h1, g_ref[...]).astype(BF16)
    q = (_dot(hn, wq_ref[...]) * (CROSS_DH ** -0.5)).astype(BF16)
    heads = []
    for hd in range(CROSS_HEADS):
        cols = slice(hd * CROSS_DH, (hd + 1) * CROSS_DH)
        s = lax.dot_general(q[:, cols], kc_ref[:, cols], _NT, preferred_element_type=F32)
        m = jnp.max(s, axis=-1, keepdims=True)
        p = jnp.exp(s - m)
        denom = jnp.sum(p, axis=-1, keepdims=True)
        heads.append((_dot(p.astype(BF16), vc_ref[:, cols]) * (1.0 / denom)).astype(BF16))
    o_ref[...] = h1 + _dot(jnp.concatenate(heads, axis=1), wo_ref[...])


def _out_cross(h, og, osw, kc, vc, wout, g, wq, wo):
    B, T, D = h.shape
    tm = TOK_TILE

    def tok(width):
        return pl.BlockSpec((None, tm, width), lambda b, i: (b, i, 0))

    memblk = pl.BlockSpec((None, MEM_LEN, D), lambda b, i: (b, 0, 0))
    return pl.pallas_call(
        _outcross_kernel,
        grid=(B, T // tm),
        in_specs=[tok(D), tok(GLA_VW), tok(SWA_QW), memblk, memblk,
                  _const_spec((GLA_VW + SWA_QW, D)), _const_spec((1, D)),
                  _const_spec((D, D)), _const_spec((D, D))],
        out_specs=tok(D),
        out_shape=jax.ShapeDtypeStruct((B, T, D), F32),
        compiler_params=_params(("parallel", "parallel")),
        name="out_cross",
    )(h, og, osw, kc, vc, wout, g, wq, wo)


def _ffn_kernel(h_ref, g_ref, wu_ref, wd_ref, cw_ref, cb_ref, gf_ref,
                o_ref, hn_ref, acc_ref, ubuf_ref, carry_ref, *, final_norm):
    tm = h_ref.shape[0]
    n_slab = 2 * FFN_CHUNK // LANES
    half = n_slab // 2

    @pl.when(pl.program_id(1) == 0)
    def _():
        carry_ref[...] = jnp.zeros_like(carry_ref)

    hn_ref[...] = _rms(h_ref[...], g_ref[...]).astype(BF16)
    acc_ref[...] = h_ref[...]

    def cols_of(c, s):
        start = (s // half) * D_FF + c * FFN_CHUNK + (s % half) * LANES
        return slice(start, start + LANES)

    def up(c):
        hn = hn_ref[...]
        for part in range(2):
            start = part * D_FF + c * FFN_CHUNK
            u = _dot(hn, wu_ref[:, start:start + FFN_CHUNK])
            for j in range(half):
                s = part * half + j
                us = u[:, j * LANES:(j + 1) * LANES]
                ubuf_ref[c % 2, s, 0:CARRY_ROWS, :] = carry_ref[c, s]
                ubuf_ref[c % 2, s, CARRY_ROWS:, :] = us
                carry_ref[c, s] = us[tm - CARRY_ROWS:, :]

    def conv(c, s):
        lanes = cols_of(c, s)
        taps = [ubuf_ref[c % 2, s, CARRY_ROWS - d:CARRY_ROWS - d + tm, :] for d in range(CONV_WIDTH)]
        return (cw_ref[2:3, lanes] * taps[0]
                + (cw_ref[1:2, lanes] * taps[1] + (cw_ref[0:1, lanes] * taps[2] + cb_ref[:, lanes])))

    up(0)
    for c in range(N_FFN_CHUNKS):
        if c + 1 < N_FFN_CHUNKS:
            up(c + 1)
        act = jnp.concatenate([(_silu(conv(c, s)) * conv(c, half + s)).astype(BF16)
                               for s in range(half)], axis=1)
        acc_ref[...] += _dot(act, wd_ref[c * FFN_CHUNK:(c + 1) * FFN_CHUNK, :])

    out = acc_ref[...]
    if final_norm:
        out = _rms(out, gf_ref[...])
    o_ref[...] = out


def _ffn(h, g, wu, wd, cw, cb, gf, final_norm):
    B, T, D = h.shape
    tm = TOK_TILE
    nc, fc = N_FFN_CHUNKS, FFN_CHUNK
    n_slab = 2 * fc // LANES
    tok = pl.BlockSpec((None, tm, D), lambda b, i: (b, i, 0))

    def once(shape):
        zeros = (0,) * len(shape)
        return pl.BlockSpec(shape, lambda *_: zeros, pipeline_mode=pl.Buffered(1))

    return pl.pallas_call(
        functools.partial(_ffn_kernel, final_norm=final_norm),
        grid=(B, T // tm),
        in_specs=[tok, _const_spec((1, D)), once((D, 2 * D_FF)), once((D_FF, D)),
                  _const_spec((CONV_WIDTH, 2 * D_FF)), _const_spec((1, 2 * D_FF)),
                  _const_spec((1, D))],
        out_specs=tok,
        out_shape=jax.ShapeDtypeStruct((B, T, D), F32),
        scratch_shapes=[pltpu.VMEM((tm, D), BF16), pltpu.VMEM((tm, D), F32),
                        pltpu.VMEM((2, n_slab, CARRY_ROWS + tm, LANES), F32),
                        pltpu.VMEM((nc, n_slab, CARRY_ROWS, LANES), F32)],
        compiler_params=_params(("parallel", "arbitrary")),
        name="ffn",
    )(h, g, wu, wd, cw, cb, gf)


def _pad_in_proj(w_in):
    parts = jnp.split(w_in, [int(s) for s in np.cumsum(IN_SPLITS)[:-1]], axis=-1)
    parts[4] = jnp.pad(parts[4], ((0, 0), (0, RANK_PAD - GLA_RANK)))
    return jnp.concatenate(parts, axis=-1).astype(BF16)


def kernel(x, mem, norm_mix, w_in, w_alpha, b_alpha, gla_gain, rel_bias, sinks, w_out, norm_cross,
           norm_mem, w_q_c, w_k_c, w_v_c, w_o_c, norm_ffn, w_up, conv_w, conv_b, w_down, norm_final):
    depth = w_in.shape[0]
    row = lambda v: v.reshape(1, -1).astype(F32)
    h = x
    for l in range(depth):
        w_pad = _pad_in_proj(w_in[l])
        wal_pad = jnp.pad(w_alpha[l], ((0, RANK_PAD - GLA_RANK), (0, 0)))
        gq, gk, gv, gr, la, o_swa = _in_proj_swa(h, row(norm_mix[l]), w_pad, wal_pad,
                                                 row(b_alpha[l]), rel_bias.astype(F32),
                                                 sinks[l].astype(F32))
        o_gla = _gla(gq, gk, gv, gr, la, row(gla_gain[l]))
        kc, vc = _mem_kv(mem, row(norm_mem[l]), w_k_c[l].astype(BF16), w_v_c[l].astype(BF16))
        h = _out_cross(h, o_gla, o_swa, kc, vc, w_out[l].astype(BF16), row(norm_cross[l]),
                       w_q_c[l].astype(BF16), w_o_c[l].astype(BF16))
        h = _ffn(h, row(norm_ffn[l]), w_up[l].astype(BF16), w_down[l].astype(BF16),
                 conv_w[l].astype(F32), row(conv_b[l]), row(norm_final),
                 final_norm=(l == depth - 1))
    return h
```

```python
import functools
import math

import numpy as np
import jax
import jax.numpy as jnp
from jax import lax
from jax.experimental import pallas as pl
from jax.experimental.pallas import tpu as pltpu

F32 = jnp.float32
BF16 = jnp.bfloat16

D_MODEL = 1024
MEM_LEN = 256
EPS = 1e-6
GLA_HEADS = 4
GLA_DK = 64
GLA_DV = 128
GLA_RANK = 16
GLA_TAU = 16.0
GLA_CHUNK = 64
SWA_HEADS = 8
SWA_KV_HEADS = 2
SWA_DH = 64
WINDOW = 128
REL_BUCKETS = 32
REL_MAX_DIST = 128
CROSS_HEADS = 4
CROSS_DH = D_MODEL // CROSS_HEADS
D_FF = 2816
CONV_WIDTH = 3

GLA_KW = GLA_HEADS * GLA_DK
GLA_VW = GLA_HEADS * GLA_DV
SWA_QW = SWA_HEADS * SWA_DH
SWA_KVW = SWA_KV_HEADS * SWA_DH
IN_SPLITS = (GLA_KW, GLA_KW, GLA_VW, GLA_VW, GLA_RANK, SWA_QW, SWA_KVW, SWA_KVW)

LOG2E = math.log2(math.e)
LANES = 128
RANK_PAD = LANES
_PAD_SPLITS = (GLA_KW, GLA_KW, GLA_VW, GLA_VW, RANK_PAD, SWA_QW, SWA_KVW, SWA_KVW)
_OFF = tuple(int(v) for v in np.cumsum((0,) + _PAD_SPLITS))
D_IN_PAD = _OFF[-1]

TOK_TILE = 512
FFN_CHUNK = 256
N_FFN_CHUNKS = D_FF // FFN_CHUNK
CARRY_ROWS = 8
VMEM_LIMIT = 56 * 1024 * 1024

_NT = (((1,), (1,)), ((), ()))
_TN = (((0,), (0,)), ((), ()))


def _rms(x, g):
    return x * lax.rsqrt(jnp.mean(x * x, axis=-1, keepdims=True) + EPS) * g


def _dot(a, b):
    return jnp.dot(a, b, preferred_element_type=F32)


def _split_bf16(x):
    hi = x.astype(BF16)
    return hi, (x - hi.astype(F32)).astype(BF16)


def _silu(x):
    return x * (1.0 / (1.0 + jnp.exp2(x * -LOG2E)))


def _const_spec(shape):
    zeros = (0,) * len(shape)
    return pl.BlockSpec(shape, lambda *_: zeros)


def _params(sem):
    return pltpu.CompilerParams(dimension_semantics=sem, vmem_limit_bytes=VMEM_LIMIT)


def _t5_bucket_table():
    L = WINDOW
    dist = (jnp.arange(L)[:, None] + L) - jnp.arange(2 * L)[None, :]
    n = jnp.maximum(dist, 0)
    max_exact = REL_BUCKETS // 2
    nf = jnp.maximum(n, 1).astype(F32)
    large = max_exact + (jnp.log(nf / max_exact) / math.log(REL_MAX_DIST / max_exact)
                         * (REL_BUCKETS - max_exact)).astype(jnp.int32)
    large = jnp.minimum(large, REL_BUCKETS - 1)
    bucket = jnp.where(n < max_exact, n, large)
    return jnp.where((dist >= 0) & (dist < WINDOW), bucket, -1).astype(jnp.int32)


def _inproj_swa_kernel(sinks_ref, relb_ref, bucket_ref, x_ref, g_ref, w_ref, wal_ref, bal_ref,
                       gq_ref, gk_ref, gv_ref, gr_ref, la_ref, o_ref,
                       bias_ref, kv_prev_ref, s_ref, pr_ref):
    L = WINDOW
    tm = x_ref.shape[0]
    n_blk = tm // L
    pair_w = 2 * SWA_DH
    n_pairs = SWA_QW // pair_w
    pairs_per_kv = n_pairs // SWA_KV_HEADS
    neg_inf = float("-inf")

    @pl.when((pl.program_id(0) == 0) & (pl.program_id(1) == 0))
    def _():
        bucket = bucket_ref[...]
        for h in range(SWA_HEADS):
            def body(b, acc):
                return jnp.where(bucket == b, relb_ref[b, h] * LOG2E, acc)
            bias_ref[h] = lax.fori_loop(0, REL_BUCKETS, body, jnp.full((L, 2 * L), neg_inf, F32))

    @pl.when(pl.program_id(1) == 0)
    def _():
        kv_prev_ref[...] = jnp.zeros_like(kv_prev_ref)

    x = x_ref[...]
    xg = (x * g_ref[...]).astype(BF16)
    inv_rms = jnp.broadcast_to(lax.rsqrt(jnp.mean(x * x, axis=-1, keepdims=True) + EPS),
                               (tm, LANES))

    def proj(lo, hi):
        res = _dot(xg, w_ref[:, _OFF[lo]:_OFF[hi]])
        return jnp.concatenate([res[:, c:c + LANES] * inv_rms
                                for c in range(0, res.shape[1], LANES)], axis=1)

    swa = proj(5, 8)
    sq = (swa[:, :SWA_QW] * (SWA_DH ** -0.5 * LOG2E)).astype(BF16)
    k_cat = jnp.concatenate([kv_prev_ref[0], swa[:, SWA_QW:SWA_QW + SWA_KVW]], axis=0)
    v_cat = jnp.concatenate([kv_prev_ref[1], swa[:, SWA_QW + SWA_KVW:]], axis=0)
    kv_prev_ref[0] = k_cat[tm:, :]
    kv_prev_ref[1] = v_cat[tm:, :]

    def dup_heads(cat):
        low = lax.broadcasted_iota(jnp.int32, cat.shape, 1) < SWA_DH
        rolled = pltpu.roll(cat, SWA_DH, 1)
        return low, (jnp.where(low, cat, rolled), jnp.where(low, rolled, cat))

    _, kdup = dup_heads(k_cat)
    low_kv, vdup = dup_heads(v_cat)
    kd = [kg.astype(BF16) for kg in kdup]
    vd = [[jnp.where(low_kv, vg, 1.0).astype(BF16), jnp.where(low_kv, 1.0, vg).astype(BF16)]
          for vg in vdup]

    low = lax.broadcasted_iota(jnp.int32, (L, pair_w), 1) < SWA_DH
    no_prev = jnp.where((pl.program_id(1) == 0)
                        & (lax.broadcasted_iota(jnp.int32, (L, 2 * L), 1) < L), neg_inf, 0.0)
    items = [(j, p) for j in range(n_blk) for p in range(n_pairs)]
    sinks = [sinks_ref[h] * LOG2E for h in range(SWA_HEADS)]
    row_max = {}

    def scores(t, e):
        j, p = items[t]
        s = s_ref[t, e * L:(e + 1) * L, :] + bias_ref[2 * p + e]
        return s + no_prev if j == 0 else s

    def swa_score_dots():
        for t, (j, p) in enumerate(items):
            qp = sq[j * L:(j + 1) * L, p * pair_w:(p + 1) * pair_w]
            zero = jnp.zeros_like(qp)
            q2 = jnp.concatenate([jnp.where(low, qp, zero), jnp.where(low, zero, qp)], axis=0)
            keys = kd[p // pairs_per_kv][j * L:(j + 2) * L, :]
            s_ref[t] = lax.dot_general(q2, keys, _NT, preferred_element_type=F32)

    def swa_row_max():
        for t, (j, p) in enumerate(items):
            for e in range(2):
                row_max[t, e] = jnp.maximum(jnp.max(scores(t, e), axis=-1, keepdims=True),
                                            sinks[2 * p + e])

    def swa_exp():
        for t in range(len(items)):
            for e in range(2):
                pr_ref[2 * t + e] = jnp.exp2(scores(t, e) - row_max[t, e]).astype(BF16)

    def swa_out():
        for t, (j, p) in enumerate(items):
            vals = vd[p // pairs_per_kv]
            outs = [_dot(pr_ref[2 * t + e], vals[e][j * L:(j + 2) * L, :]) for e in range(2)]
            sink_terms = [jnp.exp2(sinks[2 * p + e] - row_max[t, e]) for e in range(2)]
            numer = jnp.where(low, outs[0], outs[1])
            denom = (pltpu.roll(jnp.where(low, outs[1], outs[0]), SWA_DH, 1)
                     + jnp.where(low, sink_terms[0], sink_terms[1]))
            o_ref[j * L:(j + 1) * L, p * pair_w:(p + 1) * pair_w] = (
                numer * (1.0 / denom)).astype(o_ref.dtype)

    a_low = proj(4, 5)
    swa_score_dots()
    a_hi, a_lo = _split_bf16(a_low)
    w_hi, w_lo = _split_bf16(wal_ref[...])
    z = _dot(a_hi, w_hi) + _dot(a_lo, w_hi) + _dot(a_hi, w_lo) + bal_ref[...]
    la_ref[...] = -(jnp.maximum(-z, 0.0) + jnp.log1p(jnp.exp(-jnp.abs(z)))) * (1.0 / GLA_TAU)
    gq_ref[...] = proj(0, 1).astype(BF16)
    swa_row_max()
    gk_ref[...] = proj(1, 2).astype(BF16)
    gv_ref[...] = proj(2, 3).astype(BF16)
    swa_exp()
    swa_out()
    gr_ref[...] = proj(3, 4).astype(BF16)


def _in_proj_swa(h, g, w_pad, wal_pad, bal, rel_bias, sinks):
    B, T, D = h.shape
    tm = TOK_TILE
    L = WINDOW
    per = tm // L

    def tok(width):
        return pl.BlockSpec((None, tm, width), lambda b, i: (b, i, 0))

    def out(width, dt):
        return jax.ShapeDtypeStruct((B, T, width), dt)

    smem = pl.BlockSpec(memory_space=pltpu.SMEM)
    return pl.pallas_call(
        _inproj_swa_kernel,
        grid=(B, T // tm),
        in_specs=[smem, smem, _const_spec((L, 2 * L)),
                  tok(D), _const_spec((1, D)), _const_spec((D, D_IN_PAD)),
                  _const_spec((RANK_PAD, GLA_KW)), _const_spec((1, GLA_KW))],
        out_specs=[tok(GLA_KW), tok(GLA_KW), tok(GLA_VW), tok(GLA_VW), tok(GLA_KW), tok(SWA_QW)],
        out_shape=[out(GLA_KW, BF16), out(GLA_KW, BF16), out(GLA_VW, BF16), out(GLA_VW, BF16),
                   out(GLA_KW, F32), out(SWA_QW, BF16)],
        scratch_shapes=[pltpu.VMEM((SWA_HEADS, L, 2 * L), F32),
                        pltpu.VMEM((2, L, SWA_KVW), F32),
                        pltpu.VMEM((per * SWA_HEADS // 2, 2 * L, 2 * L), F32),
                        pltpu.VMEM((per * SWA_HEADS, L, 2 * L), BF16)],
        compiler_params=_params(("arbitrary", "arbitrary")),
        name="in_proj_swa",
    )(sinks, rel_bias, _t5_bucket_table(), h, g, w_pad, wal_pad, bal)


def _gla_kernel(q_ref, k_ref, v_ref, r_ref, la_ref, gain_ref, o_ref,
                st_ref, q4_ref, kv_ref, dec_ref, oi_ref):
    C = GLA_CHUNK
    H = GLA_HEADS
    n_chunks = q_ref.shape[0] // C

    @pl.when(pl.program_id(1) == 0)
    def _():
        st_ref[...] = jnp.zeros_like(st_ref)

    tril = (lax.broadcasted_iota(jnp.int32, (C, C), 0)
            >= lax.broadcasted_iota(jnp.int32, (C, C), 1)).astype(BF16)
    tril2 = jnp.concatenate([tril, tril], axis=1)
    causal4 = ((lax.broadcasted_iota(jnp.int32, (H * C, C), 0) & (C - 1))
               >= lax.broadcasted_iota(jnp.int32, (H * C, C), 1))
    head_of_lane = lax.broadcasted_iota(jnp.int32, (C, GLA_KW), 1) // GLA_DK
    head_of_st_lane = lax.broadcasted_iota(jnp.int32, (GLA_DV, GLA_KW), 1) // GLA_DK
    gain = gain_ref[...]

    rows_of = [slice(c * C, (c + 1) * C) for c in range(n_chunks)]

    def state_free(chunks):
        bcum = {}
        for c in chunks:
            la_hi, la_lo = _split_bf16(la_ref[rows_of[c], :])
            bcum[c] = _dot(tril2, jnp.concatenate([la_hi, la_lo], axis=0))

        k_inv, k_end = {}, {}
        for c in chunks:
            b_last = bcum[c][C - 1:C, :]
            dec_ref[c] = jnp.exp(b_last)
            q = q_ref[rows_of[c], :].astype(F32) * (GLA_DK ** -0.5)
            k = k_ref[rows_of[c], :].astype(F32)
            q_dec = (q * jnp.exp(bcum[c])).astype(BF16)
            k_inv[c] = (k * jnp.exp(-bcum[c])).astype(BF16)
            k_end[c] = (k * jnp.exp(b_last - bcum[c])).astype(BF16)
            q4_ref[c] = jnp.concatenate(
                [jnp.where(head_of_lane == h, q_dec, jnp.zeros_like(q_dec)) for h in range(H)],
                axis=0)

        att = {c: lax.dot_general(q4_ref[c], k_inv[c], _NT, preferred_element_type=F32)
               for c in chunks}
        kvt_all = {c: lax.dot_general(v_ref[rows_of[c], :], k_end[c], _TN,
                                      preferred_element_type=F32) for c in chunks}

        for c in chunks:
            att_c = jnp.where(causal4, att[c], 0.0).astype(BF16)
            for h in range(H):
                cols = slice(h * GLA_DV, (h + 1) * GLA_DV)
                oi_ref[rows_of[c], cols] = _dot(att_c[h * C:(h + 1) * C, :],
                                                v_ref[rows_of[c], cols])
            kvt = kvt_all[c][(H - 1) * GLA_DV:, :]
            for h in range(H - 2, -1, -1):
                kvt = jnp.where(head_of_st_lane == h, kvt_all[c][h * GLA_DV:(h + 1) * GLA_DV, :],
                                kvt)
            kv_ref[c] = kvt

    def recurrence(chunks, st):
        inter = {}
        for c in chunks:
            inter[c] = lax.dot_general(q4_ref[c], st.astype(BF16), _NT,
                                       preferred_element_type=F32)
            st = st * dec_ref[c] + kv_ref[c]
        return inter, st

    def finish(chunks, inter):
        for c in chunks:
            for h in range(H):
                cols = slice(h * GLA_DV, (h + 1) * GLA_DV)
                o = oi_ref[rows_of[c], cols] + inter[c][h * C:(h + 1) * C, :]
                o = o * lax.rsqrt(jnp.mean(o * o, axis=-1, keepdims=True) + EPS) * gain
                o_ref[rows_of[c], cols] = (
                    o * _silu(r_ref[rows_of[c], cols].astype(F32))).astype(o_ref.dtype)

    chunks = range(n_chunks)
    state_free(chunks)
    inter, st = recurrence(chunks, st_ref[...])
    st_ref[...] = st
    finish(chunks, inter)


def _gla(gq, gk, gv, gr, la, gain):
    B, T, _ = gq.shape
    tm = TOK_TILE

    def tok(width):
        return pl.BlockSpec((None, tm, width), lambda b, i: (b, i, 0))

    return pl.pallas_call(
        _gla_kernel,
        grid=(B, T // tm),
        in_specs=[tok(GLA_KW), tok(GLA_KW), tok(GLA_VW), tok(GLA_VW), tok(GLA_KW),
                  _const_spec((1, GLA_DV))],
        out_specs=tok(GLA_VW),
        out_shape=jax.ShapeDtypeStruct((B, T, GLA_VW), BF16),
        scratch_shapes=[pltpu.VMEM((GLA_DV, GLA_KW), F32),
                        pltpu.VMEM((tm // GLA_CHUNK, GLA_HEADS * GLA_CHUNK, GLA_KW), BF16),
                        pltpu.VMEM((tm // GLA_CHUNK, GLA_DV, GLA_KW), F32),
                        pltpu.VMEM((tm // GLA_CHUNK, 1, GLA_KW), F32),
                        pltpu.VMEM((tm, GLA_VW), F32)],
        compiler_params=_params(("parallel", "arbitrary")),
        name="gla",
    )(gq, gk, gv, gr, la, gain)


def _memkv_kernel(mem_ref, g_ref, wk_ref, wv_ref, k_ref, v_ref):
    mn = _rms(mem_ref[...], g_ref[...]).astype(BF16)
    k_ref[...] = _dot(mn, wk_ref[...]).astype(BF16)
    v_ref[...] = _dot(mn, wv_ref[...]).astype(BF16)


def _mem_kv(mem, g, wk, wv):
    B, M, D = mem.shape
    blk = pl.BlockSpec((None, M, D), lambda b: (b, 0, 0))
    return pl.pallas_call(
        _memkv_kernel,
        grid=(B,),
        in_specs=[blk, _const_spec((1, D)), _const_spec((D, D)), _const_spec((D, D))],
        out_specs=[blk, blk],
        out_shape=[jax.ShapeDtypeStruct((B, M, D), BF16)] * 2,
        compiler_params=_params(("parallel",)),
        name="mem_kv",
    )(mem, g, wk, wv)


def _outcross_kernel(h_ref, og_ref, os_ref, kc_ref, vc_ref, wout_ref, g_ref,
                     wq_ref, wo_ref, o_ref):
    mix = jnp.concatenate([og_ref[...], os_ref[...]], axis=1)
    h1 = h_ref[...] + _dot(mix, wout_ref[...])
    inv_rms = lax.rsqrt(jnp.mean(h1 * h1, axis=-1, keepdims=True) + EPS)
    hg = (h1 * g_ref[...]).astype(BF16)
    q = (_dot(hg, wq_ref[...]) * (inv_rms * (CROSS_DH ** -0.5))).astype(BF16)
    heads = []
    for hd in range(CROSS_HEADS):
        cols = slice(hd * CROSS_DH, (hd + 1) * CROSS_DH)
        s = lax.dot_general(q[:, cols], kc_ref[:, cols], _NT, preferred_element_type=F32)
        m = jnp.max(s, axis=-1, keepdims=True)
        p = jnp.exp(s - m)
        denom = jnp.sum(p, axis=-1, keepdims=True)
        heads.append((_dot(p.astype(BF16), vc_ref[:, cols]) * (1.0 / denom)).astype(BF16))
    o_ref[...] = h1 + _dot(jnp.concatenate(heads, axis=1), wo_ref[...])


def _out_cross(h, og, osw, kc, vc, wout, g, wq, wo):
    B, T, D = h.shape
    tm = TOK_TILE

    def tok(width):
        return pl.BlockSpec((None, tm, width), lambda b, i: (b, i, 0))

    memblk = pl.BlockSpec((None, MEM_LEN, D), lambda b, i: (b, 0, 0))
    return pl.pallas_call(
        _outcross_kernel,
        grid=(B, T // tm),
        in_specs=[tok(D), tok(GLA_VW), tok(SWA_QW), memblk, memblk,
                  _const_spec((GLA_VW + SWA_QW, D)), _const_spec((1, D)),
                  _const_spec((D, D)), _const_spec((D, D))],
        out_specs=tok(D),
        out_shape=jax.ShapeDtypeStruct((B, T, D), F32),
        compiler_params=_params(("parallel", "parallel")),
        name="out_cross",
    )(h, og, osw, kc, vc, wout, g, wq, wo)


def _ffn_kernel(h_ref, g_ref, wu_ref, wd_ref, cw_ref, cb_ref, gf_ref,
                o_ref, hn_ref, acc_ref, ubuf_ref, carry_ref, *, final_norm):
    tm = h_ref.shape[0]
    n_slab = 2 * FFN_CHUNK // LANES
    half = n_slab // 2

    @pl.when(pl.program_id(1) == 0)
    def _():
        carry_ref[...] = jnp.zeros_like(carry_ref)

    hn_ref[...] = _rms(h_ref[...], g_ref[...]).astype(BF16)

    def cols_of(c, s):
        start = (s // half) * D_FF + c * FFN_CHUNK + (s % half) * LANES
        return slice(start, start + LANES)

    def up(c):
        hn = hn_ref[...]
        for part in range(2):
            start = part * D_FF + c * FFN_CHUNK
            u = _dot(hn, wu_ref[:, start:start + FFN_CHUNK])
            for j in range(half):
                s = part * half + j
                us = u[:, j * LANES:(j + 1) * LANES]
                ubuf_ref[c % 2, s, 0:CARRY_ROWS, :] = carry_ref[c, s]
                ubuf_ref[c % 2, s, CARRY_ROWS:, :] = us
                carry_ref[c, s] = us[tm - CARRY_ROWS:, :]

    def conv(c, s):
        lanes = cols_of(c, s)
        taps = [ubuf_ref[c % 2, s, CARRY_ROWS - d:CARRY_ROWS - d + tm, :] for d in range(CONV_WIDTH)]
        return (cw_ref[2:3, lanes] * taps[0]
                + (cw_ref[1:2, lanes] * taps[1] + (cw_ref[0:1, lanes] * taps[2] + cb_ref[:, lanes])))

    up(0)
    last = N_FFN_CHUNKS - 1
    pending = []
    for c in range(N_FFN_CHUNKS):
        if c < last:
            up(c + 1)
        pending.append([(_silu(conv(c, s)) * conv(c, half + s)).astype(BF16)
                        for s in range(half)])
        if c % 2 == 1 or c == last:
            first_chunk = c + 1 - len(pending)
            act = jnp.concatenate([a for chunk_acts in pending for a in chunk_acts], axis=1)
            down = _dot(act, wd_ref[first_chunk * FFN_CHUNK:(c + 1) * FFN_CHUNK, :])
            pending = []
            base = h_ref[...] if first_chunk == 0 else acc_ref[...]
            if c < last:
                acc_ref[...] = base + down
            else:
                out = base + down
                if final_norm:
                    out = _rms(out, gf_ref[...])
                o_ref[...] = out


def _ffn(h, g, wu, wd, cw, cb, gf, final_norm):
    B, T, D = h.shape
    tm = TOK_TILE
    nc, fc = N_FFN_CHUNKS, FFN_CHUNK
    n_slab = 2 * fc // LANES
    tok = pl.BlockSpec((None, tm, D), lambda b, i: (b, i, 0))

    def once(shape):
        zeros = (0,) * len(shape)
        return pl.BlockSpec(shape, lambda *_: zeros, pipeline_mode=pl.Buffered(1))

    return pl.pallas_call(
        functools.partial(_ffn_kernel, final_norm=final_norm),
        grid=(B, T // tm),
        in_specs=[tok, _const_spec((1, D)), once((D, 2 * D_FF)), once((D_FF, D)),
                  _const_spec((CONV_WIDTH, 2 * D_FF)), _const_spec((1, 2 * D_FF)),
                  _const_spec((1, D))],
        out_specs=tok,
        out_shape=jax.ShapeDtypeStruct((B, T, D), F32),
        scratch_shapes=[pltpu.VMEM((tm, D), BF16), pltpu.VMEM((tm, D), F32),
                        pltpu.VMEM((2, n_slab, CARRY_ROWS + tm, LANES), F32),
                        pltpu.VMEM((nc, n_slab, CARRY_ROWS, LANES), F32)],
        compiler_params=_params(("parallel", "arbitrary")),
        name="ffn",
    )(h, g, wu, wd, cw, cb, gf)


def _pad_in_proj(w_in):
    parts = jnp.split(w_in, [int(s) for s in np.cumsum(IN_SPLITS)[:-1]], axis=-1)
    parts[4] = jnp.pad(parts[4], ((0, 0), (0, RANK_PAD - GLA_RANK)))
    return jnp.concatenate(parts, axis=-1).astype(BF16)


def kernel(x, mem, norm_mix, w_in, w_alpha, b_alpha, gla_gain, rel_bias, sinks, w_out, norm_cross,
           norm_mem, w_q_c, w_k_c, w_v_c, w_o_c, norm_ffn, w_up, conv_w, conv_b, w_down, norm_final):
    depth = w_in.shape[0]
    row = lambda v: v.reshape(1, -1).astype(F32)
    h = x
    for l in range(depth):
        w_pad = _pad_in_proj(w_in[l])
        wal_pad = jnp.pad(w_alpha[l], ((0, RANK_PAD - GLA_RANK), (0, 0)))
        gq, gk, gv, gr, la, o_swa = _in_proj_swa(h, row(norm_mix[l]), w_pad, wal_pad,
                                                 row(b_alpha[l]), rel_bias.astype(F32),
                                                 sinks[l].astype(F32))
        o_gla = _gla(gq, gk, gv, gr, la, row(gla_gain[l]))
        kc, vc = _mem_kv(mem, row(norm_mem[l]), w_k_c[l].astype(BF16), w_v_c[l].astype(BF16))
        h = _out_cross(h, o_gla, o_swa, kc, vc, w_out[l].astype(BF16), row(norm_cross[l]),
                       w_q_c[l].astype(BF16), w_o_c[l].astype(BF16))
        h = _ffn(h, row(norm_ffn[l]), w_up[l].astype(BF16), w_down[l].astype(BF16),
                 conv_w[l].astype(F32), row(conv_b[l]), row(norm_final),
                 final_norm=(l == depth - 1))
    return h
```

```python
import functools
import math

import numpy as np
import jax
import jax.numpy as jnp
from jax import lax
from jax.experimental import pallas as pl
from jax.experimental.pallas import tpu as pltpu

F32 = jnp.float32
BF16 = jnp.bfloat16

D_MODEL = 1024
MEM_LEN = 256
EPS = 1e-6
GLA_HEADS = 4
GLA_DK = 64
GLA_DV = 128
GLA_RANK = 16
GLA_TAU = 16.0
GLA_CHUNK = 64
SWA_HEADS = 8
SWA_KV_HEADS = 2
SWA_DH = 64
WINDOW = 128
REL_BUCKETS = 32
REL_MAX_DIST = 128
CROSS_HEADS = 4
CROSS_DH = D_MODEL // CROSS_HEADS
D_FF = 2816
CONV_WIDTH = 3

GLA_KW = GLA_HEADS * GLA_DK
GLA_VW = GLA_HEADS * GLA_DV
SWA_QW = SWA_HEADS * SWA_DH
SWA_KVW = SWA_KV_HEADS * SWA_DH
IN_SPLITS = (GLA_KW, GLA_KW, GLA_VW, GLA_VW, GLA_RANK, SWA_QW, SWA_KVW, SWA_KVW)

LOG2E = math.log2(math.e)
LANES = 128
RANK_PAD = LANES
_PAD_SPLITS = (GLA_KW, GLA_KW, GLA_VW, GLA_VW, RANK_PAD, SWA_QW, SWA_KVW, SWA_KVW)
_OFF = tuple(int(v) for v in np.cumsum((0,) + _PAD_SPLITS))
D_IN_PAD = _OFF[-1]

TOK_TILE = 512
FFN_CHUNK = 256
N_FFN_CHUNKS = D_FF // FFN_CHUNK
FFN_DOWN_GROUP = 2
FFN_UBUFS = 4
CARRY_ROWS = 8
VMEM_LIMIT = 56 * 1024 * 1024

_NT = (((1,), (1,)), ((), ()))
_TN = (((0,), (0,)), ((), ()))


def _rms(x, g):
    return x * lax.rsqrt(jnp.mean(x * x, axis=-1, keepdims=True) + EPS) * g


def _dot(a, b):
    return jnp.dot(a, b, preferred_element_type=F32)


def _split_bf16(x):
    hi = x.astype(BF16)
    return hi, (x - hi.astype(F32)).astype(BF16)


def _silu(x):
    return x * (1.0 / (1.0 + jnp.exp2(x * -LOG2E)))


def _const_spec(shape):
    zeros = (0,) * len(shape)
    return pl.BlockSpec(shape, lambda *_: zeros)


def _params(sem):
    return pltpu.CompilerParams(dimension_semantics=sem, vmem_limit_bytes=VMEM_LIMIT)


def _t5_bucket_table():
    L = WINDOW
    dist = (jnp.arange(L)[:, None] + L) - jnp.arange(2 * L)[None, :]
    n = jnp.maximum(dist, 0)
    max_exact = REL_BUCKETS // 2
    nf = jnp.maximum(n, 1).astype(F32)
    large = max_exact + (jnp.log(nf / max_exact) / math.log(REL_MAX_DIST / max_exact)
                         * (REL_BUCKETS - max_exact)).astype(jnp.int32)
    large = jnp.minimum(large, REL_BUCKETS - 1)
    bucket = jnp.where(n < max_exact, n, large)
    return jnp.where((dist >= 0) & (dist < WINDOW), bucket, -1).astype(jnp.int32)


def _inproj_swa_kernel(sinks_ref, relb_ref, bucket_ref, x_ref, g_ref, w_ref, wal_ref, bal_ref,
                       gq_ref, gk_ref, gv_ref, gr_ref, la_ref, o_ref,
                       bias_ref, kv_prev_ref, s_ref, pr_ref):
    L = WINDOW
    tm = x_ref.shape[0]
    n_blk = tm // L
    pair_w = 2 * SWA_DH
    n_pairs = SWA_QW // pair_w
    pairs_per_kv = n_pairs // SWA_KV_HEADS
    neg_inf = float("-inf")

    @pl.when((pl.program_id(0) == 0) & (pl.program_id(1) == 0))
    def _():
        bucket = bucket_ref[...]
        for h in range(SWA_HEADS):
            def body(b, acc):
                return jnp.where(bucket == b, relb_ref[b, h] * LOG2E, acc)
            bias_ref[h] = lax.fori_loop(0, REL_BUCKETS, body, jnp.full((L, 2 * L), neg_inf, F32))

    @pl.when(pl.program_id(1) == 0)
    def _():
        kv_prev_ref[...] = jnp.zeros_like(kv_prev_ref)

    x = x_ref[...]
    xg = (x * g_ref[...]).astype(BF16)
    inv_rms = jnp.broadcast_to(lax.rsqrt(jnp.mean(x * x, axis=-1, keepdims=True) + EPS),
                               (tm, LANES))

    def proj_cols(start, stop):
        res = _dot(xg, w_ref[:, start:stop])
        return jnp.concatenate([res[:, c:c + LANES] * inv_rms
                                for c in range(0, res.shape[1], LANES)], axis=1)

    def proj(lo, hi):
        return proj_cols(_OFF[lo], _OFF[hi])

    swa = proj(5, 8)
    sq = (swa[:, :SWA_QW] * (SWA_DH ** -0.5 * LOG2E)).astype(BF16)
    k_cat = jnp.concatenate([kv_prev_ref[0], swa[:, SWA_QW:SWA_QW + SWA_KVW]], axis=0)
    v_cat = jnp.concatenate([kv_prev_ref[1], swa[:, SWA_QW + SWA_KVW:]], axis=0)
    kv_prev_ref[0] = k_cat[tm:, :]
    kv_prev_ref[1] = v_cat[tm:, :]

    def dup_heads(cat):
        low = lax.broadcasted_iota(jnp.int32, cat.shape, 1) < SWA_DH
        rolled = pltpu.roll(cat, SWA_DH, 1)
        return low, (jnp.where(low, cat, rolled), jnp.where(low, rolled, cat))

    _, kdup = dup_heads(k_cat)
    low_kv, vdup = dup_heads(v_cat)
    kd = [kg.astype(BF16) for kg in kdup]
    vd = [[jnp.where(low_kv, vg, 1.0).astype(BF16), jnp.where(low_kv, 1.0, vg).astype(BF16)]
          for vg in vdup]

    low = lax.broadcasted_iota(jnp.int32, (L, pair_w), 1) < SWA_DH
    no_prev = jnp.where((pl.program_id(1) == 0)
                        & (lax.broadcasted_iota(jnp.int32, (L, 2 * L), 1) < L), neg_inf, 0.0)
    items = [(j, p) for j in range(n_blk) for p in range(n_pairs)]
    sinks = [sinks_ref[h] * LOG2E for h in range(SWA_HEADS)]
    row_max = {}

    def scores(t, e):
        j, p = items[t]
        s = s_ref[t, e * L:(e + 1) * L, :] + bias_ref[2 * p + e]
        return s + no_prev if j == 0 else s

    def swa_score_dots(ts):
        for t in ts:
            j, p = items[t]
            qp = sq[j * L:(j + 1) * L, p * pair_w:(p + 1) * pair_w]
            zero = jnp.zeros_like(qp)
            q2 = jnp.concatenate([jnp.where(low, qp, zero), jnp.where(low, zero, qp)], axis=0)
            keys = kd[p // pairs_per_kv][j * L:(j + 2) * L, :]
            s_ref[t] = lax.dot_general(q2, keys, _NT, preferred_element_type=F32)

    def swa_row_max(ts):
        for t in ts:
            for e in range(2):
                row_max[t, e] = jnp.maximum(jnp.max(scores(t, e), axis=-1, keepdims=True),
                                            sinks[2 * items[t][1] + e])

    def swa_exp(ts):
        for t in ts:
            for e in range(2):
                pr_ref[2 * t + e] = jnp.exp2(scores(t, e) - row_max[t, e]).astype(BF16)

    def swa_out(ts):
        for t in ts:
            j, p = items[t]
            vals = vd[p // pairs_per_kv]
            outs = [_dot(pr_ref[2 * t + e], vals[e][j * L:(j + 2) * L, :]) for e in range(2)]
            sink_terms = [jnp.exp2(sinks[2 * p + e] - row_max[t, e]) for e in range(2)]
            numer = jnp.where(low, outs[0], outs[1])
            denom = (pltpu.roll(jnp.where(low, outs[1], outs[0]), SWA_DH, 1)
                     + jnp.where(low, sink_terms[0], sink_terms[1]))
            o_ref[j * L:(j + 1) * L, p * pair_w:(p + 1) * pair_w] = (
                numer * (1.0 / denom)).astype(o_ref.dtype)

    first, second = range(0, len(items) // 2), range(len(items) // 2, len(items))
    a_low = proj(4, 5)
    swa_score_dots(range(len(items)))
    a_hi, a_lo = _split_bf16(a_low)
    w_hi, w_lo = _split_bf16(wal_ref[...])
    z = _dot(a_hi, w_hi) + _dot(a_lo, w_hi) + _dot(a_hi, w_lo) + bal_ref[...]
    la_ref[...] = -(jnp.maximum(-z, 0.0) + jnp.log1p(jnp.exp(-jnp.abs(z)))) * (1.0 / GLA_TAU)
    half_v = GLA_VW // 2
    gq_ref[...] = proj(0, 1).astype(BF16)
    swa_row_max(first)
    gk_ref[...] = proj(1, 2).astype(BF16)
    swa_row_max(second)
    gv_ref[:, :half_v] = proj_cols(_OFF[2], _OFF[2] + half_v).astype(BF16)
    swa_exp(first)
    gv_ref[:, half_v:] = proj_cols(_OFF[2] + half_v, _OFF[3]).astype(BF16)
    swa_exp(second)
    gr_ref[:, :half_v] = proj_cols(_OFF[3], _OFF[3] + half_v).astype(BF16)
    swa_out(first)
    swa_out(second)
    gr_ref[:, half_v:] = proj_cols(_OFF[3] + half_v, _OFF[4]).astype(BF16)


def _in_proj_swa(h, g, w_pad, wal_pad, bal, rel_bias, sinks):
    B, T, D = h.shape
    tm = TOK_TILE
    L = WINDOW
    per = tm // L

    def tok(width):
        return pl.BlockSpec((None, tm, width), lambda b, i: (b, i, 0))

    def out(width, dt):
        return jax.ShapeDtypeStruct((B, T, width), dt)

    smem = pl.BlockSpec(memory_space=pltpu.SMEM)
    return pl.pallas_call(
        _inproj_swa_kernel,
        grid=(B, T // tm),
        in_specs=[smem, smem, _const_spec((L, 2 * L)),
                  tok(D), _const_spec((1, D)), _const_spec((D, D_IN_PAD)),
                  _const_spec((RANK_PAD, GLA_KW)), _const_spec((1, GLA_KW))],
        out_specs=[tok(GLA_KW), tok(GLA_KW), tok(GLA_VW), tok(GLA_VW), tok(GLA_KW), tok(SWA_QW)],
        out_shape=[out(GLA_KW, BF16), out(GLA_KW, BF16), out(GLA_VW, BF16), out(GLA_VW, BF16),
                   out(GLA_KW, F32), out(SWA_QW, BF16)],
        scratch_shapes=[pltpu.VMEM((SWA_HEADS, L, 2 * L), F32),
                        pltpu.VMEM((2, L, SWA_KVW), F32),
                        pltpu.VMEM((per * SWA_HEADS // 2, 2 * L, 2 * L), F32),
                        pltpu.VMEM((per * SWA_HEADS, L, 2 * L), BF16)],
        compiler_params=_params(("arbitrary", "arbitrary")),
        name="in_proj_swa",
    )(sinks, rel_bias, _t5_bucket_table(), h, g, w_pad, wal_pad, bal)


def _gla_kernel(q_ref, k_ref, v_ref, r_ref, la_ref, gain_ref, o_ref,
                st_ref, q4_ref, kv_ref, dec_ref, oi_ref):
    C = GLA_CHUNK
    H = GLA_HEADS
    n_chunks = q_ref.shape[0] // C

    @pl.when(pl.program_id(1) == 0)
    def _():
        st_ref[...] = jnp.zeros_like(st_ref)

    tril = (lax.broadcasted_iota(jnp.int32, (C, C), 0)
            >= lax.broadcasted_iota(jnp.int32, (C, C), 1)).astype(BF16)
    tril2 = jnp.concatenate([tril, tril], axis=1)
    causal4 = ((lax.broadcasted_iota(jnp.int32, (H * C, C), 0) & (C - 1))
               >= lax.broadcasted_iota(jnp.int32, (H * C, C), 1))
    head_of_lane = lax.broadcasted_iota(jnp.int32, (C, GLA_KW), 1) // GLA_DK
    head_of_st_lane = lax.broadcasted_iota(jnp.int32, (GLA_DV, GLA_KW), 1) // GLA_DK
    gain = gain_ref[...]

    rows_of = [slice(c * C, (c + 1) * C) for c in range(n_chunks)]

    def state_free(chunks):
        bcum = {}
        for c in chunks:
            la_hi, la_lo = _split_bf16(la_ref[rows_of[c], :])
            bcum[c] = _dot(tril2, jnp.concatenate([la_hi, la_lo], axis=0))

        k_inv, k_end = {}, {}
        for c in chunks:
            b_last = bcum[c][C - 1:C, :]
            dec_ref[c] = jnp.exp(b_last)
            q = q_ref[rows_of[c], :].astype(F32) * (GLA_DK ** -0.5)
            k = k_ref[rows_of[c], :].astype(F32)
            q_dec = (q * jnp.exp(bcum[c])).astype(BF16)
            k_inv[c] = (k * jnp.exp(-bcum[c])).astype(BF16)
            k_end[c] = (k * jnp.exp(b_last - bcum[c])).astype(BF16)
            q4_ref[c] = jnp.concatenate(
                [jnp.where(head_of_lane == h, q_dec, jnp.zeros_like(q_dec)) for h in range(H)],
                axis=0)

        att = {c: lax.dot_general(q4_ref[c], k_inv[c], _NT, preferred_element_type=F32)
               for c in chunks}
        kvt_all = {c: lax.dot_general(v_ref[rows_of[c], :], k_end[c], _TN,
                                      preferred_element_type=F32) for c in chunks}

        for c in chunks:
            att_c = jnp.where(causal4, att[c], 0.0).astype(BF16)
            for h in range(H):
                cols = slice(h * GLA_DV, (h + 1) * GLA_DV)
                oi_ref[rows_of[c], cols] = _dot(att_c[h * C:(h + 1) * C, :],
                                                v_ref[rows_of[c], cols])
            kvt = kvt_all[c][(H - 1) * GLA_DV:, :]
            for h in range(H - 2, -1, -1):
                kvt = jnp.where(head_of_st_lane == h, kvt_all[c][h * GLA_DV:(h + 1) * GLA_DV, :],
                                kvt)
            kv_ref[c] = kvt

    def recurrence(chunks, st):
        inter = {}
        for c in chunks:
            inter[c] = lax.dot_general(q4_ref[c], st.astype(BF16), _NT,
                                       preferred_element_type=F32)
            st = st * dec_ref[c] + kv_ref[c]
        return inter, st

    def finish(chunks, inter):
        for c in chunks:
            for h in range(H):
                cols = slice(h * GLA_DV, (h + 1) * GLA_DV)
                o = oi_ref[rows_of[c], cols] + inter[c][h * C:(h + 1) * C, :]
                o = o * lax.rsqrt(jnp.mean(o * o, axis=-1, keepdims=True) + EPS) * gain
                o_ref[rows_of[c], cols] = (
                    o * _silu(r_ref[rows_of[c], cols].astype(F32))).astype(o_ref.dtype)

    chunks = range(n_chunks)
    state_free(chunks)
    inter, st = recurrence(chunks, st_ref[...])
    st_ref[...] = st
    finish(chunks, inter)


def _gla(gq, gk, gv, gr, la, gain):
    B, T, _ = gq.shape
    tm = TOK_TILE

    def tok(width):
        return pl.BlockSpec((None, tm, width), lambda b, i: (b, i, 0))

    return pl.pallas_call(
        _gla_kernel,
        grid=(B, T // tm),
        in_specs=[tok(GLA_KW), tok(GLA_KW), tok(GLA_VW), tok(GLA_VW), tok(GLA_KW),
                  _const_spec((1, GLA_DV))],
        out_specs=tok(GLA_VW),
        out_shape=jax.ShapeDtypeStruct((B, T, GLA_VW), BF16),
        scratch_shapes=[pltpu.VMEM((GLA_DV, GLA_KW), F32),
                        pltpu.VMEM((tm // GLA_CHUNK, GLA_HEADS * GLA_CHUNK, GLA_KW), BF16),
                        pltpu.VMEM((tm // GLA_CHUNK, GLA_DV, GLA_KW), F32),
                        pltpu.VMEM((tm // GLA_CHUNK, 1, GLA_KW), F32),
                        pltpu.VMEM((tm, GLA_VW), F32)],
        compiler_params=_params(("parallel", "arbitrary")),
        name="gla",
    )(gq, gk, gv, gr, la, gain)


def _memkv_kernel(mem_ref, g_ref, wk_ref, wv_ref, k_ref, v_ref):
    mn = _rms(mem_ref[...], g_ref[...]).astype(BF16)
    k_ref[...] = _dot(mn, wk_ref[...]).astype(BF16)
    v_ref[...] = _dot(mn, wv_ref[...]).astype(BF16)


def _mem_kv(mem, g, wk, wv):
    B, M, D = mem.shape
    blk = pl.BlockSpec((None, M, D), lambda b: (b, 0, 0))
    return pl.pallas_call(
        _memkv_kernel,
        grid=(B,),
        in_specs=[blk, _const_spec((1, D)), _const_spec((D, D)), _const_spec((D, D))],
        out_specs=[blk, blk],
        out_shape=[jax.ShapeDtypeStruct((B, M, D), BF16)] * 2,
        compiler_params=_params(("parallel",)),
        name="mem_kv",
    )(mem, g, wk, wv)


def _outcross_kernel(h_ref, og_ref, os_ref, kc_ref, vc_ref, wout_ref, g_ref,
                     wq_ref, wo_ref, o_ref):
    mix = jnp.concatenate([og_ref[...], os_ref[...]], axis=1)
    h1 = h_ref[...] + _dot(mix, wout_ref[...])
    inv_rms = lax.rsqrt(jnp.mean(h1 * h1, axis=-1, keepdims=True) + EPS)
    hg = (h1 * g_ref[...]).astype(BF16)
    q = (_dot(hg, wq_ref[...]) * (inv_rms * (CROSS_DH ** -0.5))).astype(BF16)
    heads = []
    for hd in range(CROSS_HEADS):
        cols = slice(hd * CROSS_DH, (hd + 1) * CROSS_DH)
        s = lax.dot_general(q[:, cols], kc_ref[:, cols], _NT, preferred_element_type=F32)
        m = jnp.max(s, axis=-1, keepdims=True)
        p = jnp.exp(s - m)
        denom = jnp.sum(p, axis=-1, keepdims=True)
        heads.append((_dot(p.astype(BF16), vc_ref[:, cols]) * (1.0 / denom)).astype(BF16))
    o_ref[...] = h1 + _dot(jnp.concatenate(heads, axis=1), wo_ref[...])


def _out_cross(h, og, osw, kc, vc, wout, g, wq, wo):
    B, T, D = h.shape
    tm = TOK_TILE

    def tok(width):
        return pl.BlockSpec((None, tm, width), lambda b, i: (b, i, 0))

    memblk = pl.BlockSpec((None, MEM_LEN, D), lambda b, i: (b, 0, 0))
    return pl.pallas_call(
        _outcross_kernel,
        grid=(B, T // tm),
        in_specs=[tok(D), tok(GLA_VW), tok(SWA_QW), memblk, memblk,
                  _const_spec((GLA_VW + SWA_QW, D)), _const_spec((1, D)),
                  _const_spec((D, D)), _const_spec((D, D))],
        out_specs=tok(D),
        out_shape=jax.ShapeDtypeStruct((B, T, D), F32),
        compiler_params=_params(("parallel", "parallel")),
        name="out_cross",
    )(h, og, osw, kc, vc, wout, g, wq, wo)


def _ffn_kernel(h_ref, g_ref, wu_ref, wd_ref, cw_ref, cb_ref, gf_ref,
                o_ref, hn_ref, acc_ref, ubuf_ref, carry_ref, *, final_norm):
    tm = h_ref.shape[0]
    n_slab = 2 * FFN_CHUNK // LANES
    half = n_slab // 2

    @pl.when(pl.program_id(1) == 0)
    def _():
        carry_ref[...] = jnp.zeros_like(carry_ref)

    hn_ref[...] = _rms(h_ref[...], g_ref[...]).astype(BF16)

    def cols_of(c, s):
        start = (s // half) * D_FF + c * FFN_CHUNK + (s % half) * LANES
        return slice(start, start + LANES)

    def up(c):
        hn = hn_ref[...]
        for part in range(2):
            start = part * D_FF + c * FFN_CHUNK
            u = _dot(hn, wu_ref[:, start:start + FFN_CHUNK])
            for j in range(half):
                s = part * half + j
                us = u[:, j * LANES:(j + 1) * LANES]
                ubuf_ref[c % FFN_UBUFS, s, 0:CARRY_ROWS, :] = carry_ref[c, s]
                ubuf_ref[c % FFN_UBUFS, s, CARRY_ROWS:, :] = us
                carry_ref[c, s] = us[tm - CARRY_ROWS:, :]

    def conv(c, s):
        lanes = cols_of(c, s)
        taps = [ubuf_ref[c % FFN_UBUFS, s, CARRY_ROWS - d:CARRY_ROWS - d + tm, :] for d in range(CONV_WIDTH)]
        return (cw_ref[2:3, lanes] * taps[0]
                + (cw_ref[1:2, lanes] * taps[1] + (cw_ref[0:1, lanes] * taps[2] + cb_ref[:, lanes])))

    lookahead = FFN_UBUFS - 1
    for c in range(lookahead):
        up(c)
    last = N_FFN_CHUNKS - 1
    pending = []
    for c in range(N_FFN_CHUNKS):
        if c + lookahead <= last:
            up(c + lookahead)
        pending.append([(_silu(conv(c, s)) * conv(c, half + s)).astype(BF16)
                        for s in range(half)])
        if len(pending) == FFN_DOWN_GROUP or c == last:
            first_chunk = c + 1 - len(pending)
            act = jnp.concatenate([a for chunk_acts in pending for a in chunk_acts], axis=1)
            down = _dot(act, wd_ref[first_chunk * FFN_CHUNK:(c + 1) * FFN_CHUNK, :])
            pending = []
            base = h_ref[...] if first_chunk == 0 else acc_ref[...]
            if c < last:
                acc_ref[...] = base + down
            else:
                out = base + down
                if final_norm:
                    out = _rms(out, gf_ref[...])
                o_ref[...] = out


def _ffn(h, g, wu, wd, cw, cb, gf, final_norm):
    B, T, D = h.shape
    tm = TOK_TILE
    nc, fc = N_FFN_CHUNKS, FFN_CHUNK
    n_slab = 2 * fc // LANES
    tok = pl.BlockSpec((None, tm, D), lambda b, i: (b, i, 0))

    def once(shape):
        zeros = (0,) * len(shape)
        return pl.BlockSpec(shape, lambda *_: zeros, pipeline_mode=pl.Buffered(1))

    return pl.pallas_call(
        functools.partial(_ffn_kernel, final_norm=final_norm),
        grid=(B, T // tm),
        in_specs=[tok, _const_spec((1, D)), once((D, 2 * D_FF)), once((D_FF, D)),
                  _const_spec((CONV_WIDTH, 2 * D_FF)), _const_spec((1, 2 * D_FF)),
                  _const_spec((1, D))],
        out_specs=tok,
        out_shape=jax.ShapeDtypeStruct((B, T, D), F32),
        scratch_shapes=[pltpu.VMEM((tm, D), BF16), pltpu.VMEM((tm, D), F32),
                        pltpu.VMEM((FFN_UBUFS, n_slab, CARRY_ROWS + tm, LANES), F32),
                        pltpu.VMEM((nc, n_slab, CARRY_ROWS, LANES), F32)],
        compiler_params=_params(("parallel", "arbitrary")),
        name="ffn",
    )(h, g, wu, wd, cw, cb, gf)


def _pad_in_proj(w_in):
    parts = jnp.split(w_in, [int(s) for s in np.cumsum(IN_SPLITS)[:-1]], axis=-1)
    parts[4] = jnp.pad(parts[4], ((0, 0), (0, RANK_PAD - GLA_RANK)))
    return jnp.concatenate(parts, axis=-1).astype(BF16)


def kernel(x, mem, norm_mix, w_in, w_alpha, b_alpha, gla_gain, rel_bias, sinks, w_out, norm_cross,
           norm_mem, w_q_c, w_k_c, w_v_c, w_o_c, norm_ffn, w_up, conv_w, conv_b, w_down, norm_final):
    depth = w_in.shape[0]
    row = lambda v: v.reshape(1, -1).astype(F32)
    h = x
    for l in range(depth):
        w_pad = _pad_in_proj(w_in[l])
        wal_pad = jnp.pad(w_alpha[l], ((0, RANK_PAD - GLA_RANK), (0, 0)))
        gq, gk, gv, gr, la, o_swa = _in_proj_swa(h, row(norm_mix[l]), w_pad, wal_pad,
                                                 row(b_alpha[l]), rel_bias.astype(F32),
                                                 sinks[l].astype(F32))
        o_gla = _gla(gq, gk, gv, gr, la, row(gla_gain[l]))
        kc, vc = _mem_kv(mem, row(norm_mem[l]), w_k_c[l].astype(BF16), w_v_c[l].astype(BF16))
        h = _out_cross(h, o_gla, o_swa, kc, vc, w_out[l].astype(BF16), row(norm_cross[l]),
                       w_q_c[l].astype(BF16), w_o_c[l].astype(BF16))
        h = _ffn(h, row(norm_ffn[l]), w_up[l].astype(BF16), w_down[l].astype(BF16),
                 conv_w[l].astype(F32), row(conv_b[l]), row(norm_final),
                 final_norm=(l == depth - 1))
    return h
```

```python
import functools
import math

import numpy as np
import jax
import jax.numpy as jnp
from jax import lax
from jax.experimental import pallas as pl
from jax.experimental.pallas import tpu as pltpu

F32 = jnp.float32
BF16 = jnp.bfloat16

D_MODEL = 1024
MEM_LEN = 256
EPS = 1e-6
GLA_HEADS = 4
GLA_DK = 64
GLA_DV = 128
GLA_RANK = 16
GLA_TAU = 16.0
GLA_CHUNK = 64
SWA_HEADS = 8
SWA_KV_HEADS = 2
SWA_DH = 64
WINDOW = 128
REL_BUCKETS = 32
REL_MAX_DIST = 128
CROSS_HEADS = 4
CROSS_DH = D_MODEL // CROSS_HEADS
D_FF = 2816
CONV_WIDTH = 3

GLA_KW = GLA_HEADS * GLA_DK
GLA_VW = GLA_HEADS * GLA_DV
SWA_QW = SWA_HEADS * SWA_DH
SWA_KVW = SWA_KV_HEADS * SWA_DH
IN_SPLITS = (GLA_KW, GLA_KW, GLA_VW, GLA_VW, GLA_RANK, SWA_QW, SWA_KVW, SWA_KVW)

LOG2E = math.log2(math.e)
LANES = 128
RANK_PAD = LANES
_PAD_SPLITS = (GLA_KW, GLA_KW, GLA_VW, GLA_VW, RANK_PAD, SWA_QW, SWA_KVW, SWA_KVW)
_OFF = tuple(int(v) for v in np.cumsum((0,) + _PAD_SPLITS))
D_IN_PAD = _OFF[-1]

TOK_TILE = 512
GLA_TILE = 1024
FFN_CHUNK = 256
N_FFN_CHUNKS = D_FF // FFN_CHUNK
FFN_DOWN_GROUP = 2
FFN_UBUFS = 4
CARRY_ROWS = 8
VMEM_LIMIT = 56 * 1024 * 1024

_NT = (((1,), (1,)), ((), ()))
_TN = (((0,), (0,)), ((), ()))


def _rms(x, g):
    return x * lax.rsqrt(jnp.mean(x * x, axis=-1, keepdims=True) + EPS) * g


def _dot(a, b):
    return jnp.dot(a, b, preferred_element_type=F32)


def _split_bf16(x):
    hi = x.astype(BF16)
    return hi, (x - hi.astype(F32)).astype(BF16)


def _silu(x):
    return x * (1.0 / (1.0 + jnp.exp2(x * -LOG2E)))


def _const_spec(shape):
    zeros = (0,) * len(shape)
    return pl.BlockSpec(shape, lambda *_: zeros)


def _params(sem):
    return pltpu.CompilerParams(dimension_semantics=sem, vmem_limit_bytes=VMEM_LIMIT)


def _t5_bucket_table():
    L = WINDOW
    dist = (jnp.arange(L)[:, None] + L) - jnp.arange(2 * L)[None, :]
    n = jnp.maximum(dist, 0)
    max_exact = REL_BUCKETS // 2
    nf = jnp.maximum(n, 1).astype(F32)
    large = max_exact + (jnp.log(nf / max_exact) / math.log(REL_MAX_DIST / max_exact)
                         * (REL_BUCKETS - max_exact)).astype(jnp.int32)
    large = jnp.minimum(large, REL_BUCKETS - 1)
    bucket = jnp.where(n < max_exact, n, large)
    return jnp.where((dist >= 0) & (dist < WINDOW), bucket, -1).astype(jnp.int32)


def _inproj_swa_kernel(sinks_ref, relb_ref, bucket_ref, x_ref, g_ref, w_ref, wal_ref, bal_ref,
                       gq_ref, gk_ref, gv_ref, gr_ref, la_ref, o_ref,
                       bias_ref, kv_prev_ref, s_ref, pr_ref):
    L = WINDOW
    tm = x_ref.shape[0]
    n_blk = tm // L
    pair_w = 2 * SWA_DH
    n_pairs = SWA_QW // pair_w
    pairs_per_kv = n_pairs // SWA_KV_HEADS
    neg_inf = float("-inf")

    @pl.when((pl.program_id(0) == 0) & (pl.program_id(1) == 0))
    def _():
        bucket = bucket_ref[...]
        for h in range(SWA_HEADS):
            def body(b, acc):
                return jnp.where(bucket == b, relb_ref[b, h] * LOG2E, acc)
            bias_ref[h] = lax.fori_loop(0, REL_BUCKETS, body, jnp.full((L, 2 * L), neg_inf, F32))

    @pl.when(pl.program_id(1) == 0)
    def _():
        kv_prev_ref[...] = jnp.zeros_like(kv_prev_ref)

    x = x_ref[...]
    xg = (x * g_ref[...]).astype(BF16)
    inv_rms = jnp.broadcast_to(lax.rsqrt(jnp.mean(x * x, axis=-1, keepdims=True) + EPS),
                               (tm, LANES))

    def proj_cols(start, stop):
        res = _dot(xg, w_ref[:, start:stop])
        return jnp.concatenate([res[:, c:c + LANES] * inv_rms
                                for c in range(0, res.shape[1], LANES)], axis=1)

    def proj(lo, hi):
        return proj_cols(_OFF[lo], _OFF[hi])

    swa = proj(5, 8)
    sq = (swa[:, :SWA_QW] * (SWA_DH ** -0.5 * LOG2E)).astype(BF16)
    k_cat = jnp.concatenate([kv_prev_ref[0], swa[:, SWA_QW:SWA_QW + SWA_KVW]], axis=0)
    v_cat = jnp.concatenate([kv_prev_ref[1], swa[:, SWA_QW + SWA_KVW:]], axis=0)
    kv_prev_ref[0] = k_cat[tm:, :]
    kv_prev_ref[1] = v_cat[tm:, :]

    def dup_heads(cat):
        low = lax.broadcasted_iota(jnp.int32, cat.shape, 1) < SWA_DH
        rolled = pltpu.roll(cat, SWA_DH, 1)
        return low, (jnp.where(low, cat, rolled), jnp.where(low, rolled, cat))

    _, kdup = dup_heads(k_cat)
    low_kv, vdup = dup_heads(v_cat)
    kd = [kg.astype(BF16) for kg in kdup]
    vd = [[jnp.where(low_kv, vg, 1.0).astype(BF16), jnp.where(low_kv, 1.0, vg).astype(BF16)]
          for vg in vdup]

    low = lax.broadcasted_iota(jnp.int32, (L, pair_w), 1) < SWA_DH
    no_prev = jnp.where((pl.program_id(1) == 0)
                        & (lax.broadcasted_iota(jnp.int32, (L, 2 * L), 1) < L), neg_inf, 0.0)
    items = [(j, p) for j in range(n_blk) for p in range(n_pairs)]
    sinks = [sinks_ref[h] * LOG2E for h in range(SWA_HEADS)]
    row_max = {}

    def scores(t, e):
        j, p = items[t]
        s = s_ref[t, e * L:(e + 1) * L, :] + bias_ref[2 * p + e]
        return s + no_prev if j == 0 else s

    def swa_score_dots(ts):
        for t in ts:
            j, p = items[t]
            qp = sq[j * L:(j + 1) * L, p * pair_w:(p + 1) * pair_w]
            zero = jnp.zeros_like(qp)
            q2 = jnp.concatenate([jnp.where(low, qp, zero), jnp.where(low, zero, qp)], axis=0)
            keys = kd[p // pairs_per_kv][j * L:(j + 2) * L, :]
            s_ref[t] = lax.dot_general(q2, keys, _NT, preferred_element_type=F32)

    def swa_row_max(ts):
        for t in ts:
            for e in range(2):
                row_max[t, e] = jnp.maximum(jnp.max(scores(t, e), axis=-1, keepdims=True),
                                            sinks[2 * items[t][1] + e])

    def swa_exp(ts):
        for t in ts:
            for e in range(2):
                pr_ref[2 * t + e] = jnp.exp2(scores(t, e) - row_max[t, e]).astype(BF16)

    def swa_out(ts):
        for t in ts:
            j, p = items[t]
            vals = vd[p // pairs_per_kv]
            outs = [_dot(pr_ref[2 * t + e], vals[e][j * L:(j + 2) * L, :]) for e in range(2)]
            sink_terms = [jnp.exp2(sinks[2 * p + e] - row_max[t, e]) for e in range(2)]
            numer = jnp.where(low, outs[0], outs[1])
            denom = (pltpu.roll(jnp.where(low, outs[1], outs[0]), SWA_DH, 1)
                     + jnp.where(low, sink_terms[0], sink_terms[1]))
            o_ref[j * L:(j + 1) * L, p * pair_w:(p + 1) * pair_w] = (
                numer * (1.0 / denom)).astype(o_ref.dtype)

    first, second = range(0, len(items) // 2), range(len(items) // 2, len(items))
    a_low = proj(4, 5)
    swa_score_dots(range(len(items)))
    a_hi, a_lo = _split_bf16(a_low)
    w_hi, w_lo = _split_bf16(wal_ref[...])
    z = _dot(a_hi, w_hi) + _dot(a_lo, w_hi) + _dot(a_hi, w_lo) + bal_ref[...]
    la_ref[...] = -(jnp.maximum(-z, 0.0) + jnp.log1p(jnp.exp(-jnp.abs(z)))) * (1.0 / GLA_TAU)
    half_v = GLA_VW // 2
    gq_ref[...] = proj(0, 1).astype(BF16)
    swa_row_max(first)
    gk_ref[...] = proj(1, 2).astype(BF16)
    swa_row_max(second)
    gv_ref[:, :half_v] = proj_cols(_OFF[2], _OFF[2] + half_v).astype(BF16)
    swa_exp(first)
    gv_ref[:, half_v:] = proj_cols(_OFF[2] + half_v, _OFF[3]).astype(BF16)
    swa_exp(second)
    gr_ref[:, :half_v] = proj_cols(_OFF[3], _OFF[3] + half_v).astype(BF16)
    swa_out(first)
    swa_out(second)
    gr_ref[:, half_v:] = proj_cols(_OFF[3] + half_v, _OFF[4]).astype(BF16)


def _in_proj_swa(h, g, w_pad, wal_pad, bal, rel_bias, sinks):
    B, T, D = h.shape
    tm = TOK_TILE
    L = WINDOW
    per = tm // L

    def tok(width):
        return pl.BlockSpec((None, tm, width), lambda b, i: (b, i, 0))

    def out(width, dt):
        return jax.ShapeDtypeStruct((B, T, width), dt)

    smem = pl.BlockSpec(memory_space=pltpu.SMEM)
    return pl.pallas_call(
        _inproj_swa_kernel,
        grid=(B, T // tm),
        in_specs=[smem, smem, _const_spec((L, 2 * L)),
                  tok(D), _const_spec((1, D)), _const_spec((D, D_IN_PAD)),
                  _const_spec((RANK_PAD, GLA_KW)), _const_spec((1, GLA_KW))],
        out_specs=[tok(GLA_KW), tok(GLA_KW), tok(GLA_VW), tok(GLA_VW), tok(GLA_KW), tok(SWA_QW)],
        out_shape=[out(GLA_KW, BF16), out(GLA_KW, BF16), out(GLA_VW, BF16), out(GLA_VW, BF16),
                   out(GLA_KW, F32), out(SWA_QW, BF16)],
        scratch_shapes=[pltpu.VMEM((SWA_HEADS, L, 2 * L), F32),
                        pltpu.VMEM((2, L, SWA_KVW), F32),
                        pltpu.VMEM((per * SWA_HEADS // 2, 2 * L, 2 * L), F32),
                        pltpu.VMEM((per * SWA_HEADS, L, 2 * L), BF16)],
        compiler_params=_params(("arbitrary", "arbitrary")),
        name="in_proj_swa",
    )(sinks, rel_bias, _t5_bucket_table(), h, g, w_pad, wal_pad, bal)


def _gla_kernel(q_ref, k_ref, v_ref, r_ref, la_ref, gain_ref, o_ref,
                st_ref, q4_ref, kv_ref, dec_ref, oi_ref):
    C = GLA_CHUNK
    H = GLA_HEADS
    n_chunks = q_ref.shape[0] // C

    @pl.when(pl.program_id(1) == 0)
    def _():
        st_ref[...] = jnp.zeros_like(st_ref)

    tril = (lax.broadcasted_iota(jnp.int32, (C, C), 0)
            >= lax.broadcasted_iota(jnp.int32, (C, C), 1)).astype(BF16)
    tril2 = jnp.concatenate([tril, tril], axis=1)
    causal4 = ((lax.broadcasted_iota(jnp.int32, (H * C, C), 0) & (C - 1))
               >= lax.broadcasted_iota(jnp.int32, (H * C, C), 1))
    head_of_lane = lax.broadcasted_iota(jnp.int32, (C, GLA_KW), 1) // GLA_DK
    head_of_st_lane = lax.broadcasted_iota(jnp.int32, (GLA_DV, GLA_KW), 1) // GLA_DK
    gain = gain_ref[...]

    rows_of = [slice(c * C, (c + 1) * C) for c in range(n_chunks)]

    def state_free(chunks):
        bcum = {}
        for c in chunks:
            la_hi, la_lo = _split_bf16(la_ref[rows_of[c], :])
            bcum[c] = _dot(tril2, jnp.concatenate([la_hi, la_lo], axis=0))

        k_inv, k_end = {}, {}
        for c in chunks:
            b_last = bcum[c][C - 1:C, :]
            dec_ref[c] = jnp.exp(b_last)
            q = q_ref[rows_of[c], :].astype(F32) * (GLA_DK ** -0.5)
            k = k_ref[rows_of[c], :].astype(F32)
            q_dec = (q * jnp.exp(bcum[c])).astype(BF16)
            k_inv[c] = (k * jnp.exp(-bcum[c])).astype(BF16)
            k_end[c] = (k * jnp.exp(b_last - bcum[c])).astype(BF16)
            q4_ref[c] = jnp.concatenate(
                [jnp.where(head_of_lane == h, q_dec, jnp.zeros_like(q_dec)) for h in range(H)],
                axis=0)

        att = {c: lax.dot_general(q4_ref[c], k_inv[c], _NT, preferred_element_type=F32)
               for c in chunks}
        kvt_all = {c: lax.dot_general(v_ref[rows_of[c], :], k_end[c], _TN,
                                      preferred_element_type=F32) for c in chunks}

        for c in chunks:
            att_c = jnp.where(causal4, att[c], 0.0).astype(BF16)
            for h in range(H):
                cols = slice(h * GLA_DV, (h + 1) * GLA_DV)
                oi_ref[rows_of[c], cols] = _dot(att_c[h * C:(h + 1) * C, :],
                                                v_ref[rows_of[c], cols])
            kvt = kvt_all[c][(H - 1) * GLA_DV:, :]
            for h in range(H - 2, -1, -1):
                kvt = jnp.where(head_of_st_lane == h, kvt_all[c][h * GLA_DV:(h + 1) * GLA_DV, :],
                                kvt)
            kv_ref[c] = kvt

    def recurrence(chunks, st):
        inter = {}
        for c in chunks:
            inter[c] = lax.dot_general(q4_ref[c], st.astype(BF16), _NT,
                                       preferred_element_type=F32)
            st = st * dec_ref[c] + kv_ref[c]
        return inter, st

    def finish(chunks, inter):
        for c in chunks:
            for h in range(H):
                cols = slice(h * GLA_DV, (h + 1) * GLA_DV)
                o = oi_ref[rows_of[c], cols] + inter[c][h * C:(h + 1) * C, :]
                o = o * lax.rsqrt(jnp.mean(o * o, axis=-1, keepdims=True) + EPS) * gain
                o_ref[rows_of[c], cols] = (
                    o * _silu(r_ref[rows_of[c], cols].astype(F32))).astype(o_ref.dtype)

    chunks = range(n_chunks)
    state_free(chunks)
    inter, st = recurrence(chunks, st_ref[...])
    st_ref[...] = st
    finish(chunks, inter)


def _gla(gq, gk, gv, gr, la, gain):
    B, T, _ = gq.shape
    tm = GLA_TILE

    def tok(width):
        return pl.BlockSpec((None, tm, width), lambda b, i: (b, i, 0))

    return pl.pallas_call(
        _gla_kernel,
        grid=(B, T // tm),
        in_specs=[tok(GLA_KW), tok(GLA_KW), tok(GLA_VW), tok(GLA_VW), tok(GLA_KW),
                  _const_spec((1, GLA_DV))],
        out_specs=tok(GLA_VW),
        out_shape=jax.ShapeDtypeStruct((B, T, GLA_VW), BF16),
        scratch_shapes=[pltpu.VMEM((GLA_DV, GLA_KW), F32),
                        pltpu.VMEM((tm // GLA_CHUNK, GLA_HEADS * GLA_CHUNK, GLA_KW), BF16),
                        pltpu.VMEM((tm // GLA_CHUNK, GLA_DV, GLA_KW), F32),
                        pltpu.VMEM((tm // GLA_CHUNK, 1, GLA_KW), F32),
                        pltpu.VMEM((tm, GLA_VW), F32)],
        compiler_params=_params(("parallel", "arbitrary")),
        name="gla",
    )(gq, gk, gv, gr, la, gain)


def _memkv_kernel(mem_ref, g_ref, wk_ref, wv_ref, k_ref, v_ref):
    mn = _rms(mem_ref[...], g_ref[...]).astype(BF16)
    k_ref[...] = _dot(mn, wk_ref[...]).astype(BF16)
    v_ref[...] = _dot(mn, wv_ref[...]).astype(BF16)


def _mem_kv(mem, g, wk, wv):
    B, M, D = mem.shape
    blk = pl.BlockSpec((None, M, D), lambda b: (b, 0, 0))
    return pl.pallas_call(
        _memkv_kernel,
        grid=(B,),
        in_specs=[blk, _const_spec((1, D)), _const_spec((D, D)), _const_spec((D, D))],
        out_specs=[blk, blk],
        out_shape=[jax.ShapeDtypeStruct((B, M, D), BF16)] * 2,
        compiler_params=_params(("parallel",)),
        name="mem_kv",
    )(mem, g, wk, wv)


def _outcross_kernel(h_ref, og_ref, os_ref, kc_ref, vc_ref, wout_ref, g_ref,
                     wq_ref, wo_ref, o_ref):
    mix = jnp.concatenate([og_ref[...], os_ref[...]], axis=1)
    h1 = h_ref[...] + _dot(mix, wout_ref[...])
    inv_rms = lax.rsqrt(jnp.mean(h1 * h1, axis=-1, keepdims=True) + EPS)
    hg = (h1 * g_ref[...]).astype(BF16)
    q = (_dot(hg, wq_ref[...]) * (inv_rms * (CROSS_DH ** -0.5 * LOG2E))).astype(BF16)
    cols = [slice(hd * CROSS_DH, (hd + 1) * CROSS_DH) for hd in range(CROSS_HEADS)]
    scores = [lax.dot_general(q[:, c], kc_ref[:, c], _NT, preferred_element_type=F32)
              for c in cols]
    parts = []
    for s, c in zip(scores, cols):
        m = jnp.max(s, axis=-1, keepdims=True)
        p = jnp.exp2(s - m)
        denom = jnp.sum(p, axis=-1, keepdims=True)
        o = (_dot(p.astype(BF16), vc_ref[:, c]) * (1.0 / denom)).astype(BF16)
        parts.append(_dot(o, wo_ref[c, :]))
    attn = parts[0]
    for part in parts[1:]:
        attn = attn + part
    o_ref[...] = h1 + attn


def _out_cross(h, og, osw, kc, vc, wout, g, wq, wo):
    B, T, D = h.shape
    tm = TOK_TILE

    def tok(width):
        return pl.BlockSpec((None, tm, width), lambda b, i: (b, i, 0))

    memblk = pl.BlockSpec((None, MEM_LEN, D), lambda b, i: (b, 0, 0))
    return pl.pallas_call(
        _outcross_kernel,
        grid=(B, T // tm),
        in_specs=[tok(D), tok(GLA_VW), tok(SWA_QW), memblk, memblk,
                  _const_spec((GLA_VW + SWA_QW, D)), _const_spec((1, D)),
                  _const_spec((D, D)), _const_spec((D, D))],
        out_specs=tok(D),
        out_shape=jax.ShapeDtypeStruct((B, T, D), F32),
        compiler_params=_params(("parallel", "parallel")),
        name="out_cross",
    )(h, og, osw, kc, vc, wout, g, wq, wo)


def _ffn_kernel(h_ref, g_ref, wu_ref, wd_ref, cw_ref, cb_ref, gf_ref,
                o_ref, hn_ref, acc_ref, ubuf_ref, carry_ref, *, final_norm):
    tm = h_ref.shape[0]
    n_slab = 2 * FFN_CHUNK // LANES
    half = n_slab // 2

    @pl.when(pl.program_id(1) == 0)
    def _():
        carry_ref[...] = jnp.zeros_like(carry_ref)

    hn_ref[...] = _rms(h_ref[...], g_ref[...]).astype(BF16)

    def cols_of(c, s):
        start = (s // half) * D_FF + c * FFN_CHUNK + (s % half) * LANES
        return slice(start, start + LANES)

    def up(c):
        hn = hn_ref[...]
        for part in range(2):
            start = part * D_FF + c * FFN_CHUNK
            u = _dot(hn, wu_ref[:, start:start + FFN_CHUNK])
            for j in range(half):
                s = part * half + j
                us = u[:, j * LANES:(j + 1) * LANES]
                ubuf_ref[c % FFN_UBUFS, s, 0:CARRY_ROWS, :] = carry_ref[c, s]
                ubuf_ref[c % FFN_UBUFS, s, CARRY_ROWS:, :] = us
                carry_ref[c, s] = us[tm - CARRY_ROWS:, :]

    def conv(c, s):
        lanes = cols_of(c, s)
        taps = [ubuf_ref[c % FFN_UBUFS, s, CARRY_ROWS - d:CARRY_ROWS - d + tm, :] for d in range(CONV_WIDTH)]
        return (cw_ref[2:3, lanes] * taps[0]
                + (cw_ref[1:2, lanes] * taps[1] + (cw_ref[0:1, lanes] * taps[2] + cb_ref[:, lanes])))

    lookahead = FFN_UBUFS - 1
    for c in range(lookahead):
        up(c)
    last = N_FFN_CHUNKS - 1
    pending = []
    for c in range(N_FFN_CHUNKS):
        if c + lookahead <= last:
            up(c + lookahead)
        pending.append([(_silu(conv(c, s)) * conv(c, half + s)).astype(BF16)
                        for s in range(half)])
        if len(pending) == FFN_DOWN_GROUP or c == last:
            first_chunk = c + 1 - len(pending)
            act = jnp.concatenate([a for chunk_acts in pending for a in chunk_acts], axis=1)
            down = _dot(act, wd_ref[first_chunk * FFN_CHUNK:(c + 1) * FFN_CHUNK, :])
            pending = []
            base = h_ref[...] if first_chunk == 0 else acc_ref[...]
            if c < last:
                acc_ref[...] = base + down
            else:
                out = base + down
                if final_norm:
                    out = _rms(out, gf_ref[...])
                o_ref[...] = out


def _ffn(h, g, wu, wd, cw, cb, gf, final_norm):
    B, T, D = h.shape
    tm = TOK_TILE
    nc, fc = N_FFN_CHUNKS, FFN_CHUNK
    n_slab = 2 * fc // LANES
    tok = pl.BlockSpec((None, tm, D), lambda b, i: (b, i, 0))

    def once(shape):
        zeros = (0,) * len(shape)
        return pl.BlockSpec(shape, lambda *_: zeros, pipeline_mode=pl.Buffered(1))

    return pl.pallas_call(
        functools.partial(_ffn_kernel, final_norm=final_norm),
        grid=(B, T // tm),
        in_specs=[tok, _const_spec((1, D)), once((D, 2 * D_FF)), once((D_FF, D)),
                  _const_spec((CONV_WIDTH, 2 * D_FF)), _const_spec((1, 2 * D_FF)),
                  _const_spec((1, D))],
        out_specs=tok,
        out_shape=jax.ShapeDtypeStruct((B, T, D), F32),
        scratch_shapes=[pltpu.VMEM((tm, D), BF16), pltpu.VMEM((tm, D), F32),
                        pltpu.VMEM((FFN_UBUFS, n_slab, CARRY_ROWS + tm, LANES), F32),
                        pltpu.VMEM((nc, n_slab, CARRY_ROWS, LANES), F32)],
        compiler_params=_params(("parallel", "arbitrary")),
        name="ffn",
    )(h, g, wu, wd, cw, cb, gf)


def _pad_in_proj(w_in):
    parts = jnp.split(w_in, [int(s) for s in np.cumsum(IN_SPLITS)[:-1]], axis=-1)
    parts[4] = jnp.pad(parts[4], ((0, 0), (0, RANK_PAD - GLA_RANK)))
    return jnp.concatenate(parts, axis=-1).astype(BF16)


def kernel(x, mem, norm_mix, w_in, w_alpha, b_alpha, gla_gain, rel_bias, sinks, w_out, norm_cross,
           norm_mem, w_q_c, w_k_c, w_v_c, w_o_c, norm_ffn, w_up, conv_w, conv_b, w_down, norm_final):
    depth = w_in.shape[0]
    row = lambda v: v.reshape(1, -1).astype(F32)
    h = x
    for l in range(depth):
        w_pad = _pad_in_proj(w_in[l])
        wal_pad = jnp.pad(w_alpha[l], ((0, RANK_PAD - GLA_RANK), (0, 0)))
        gq, gk, gv, gr, la, o_swa = _in_proj_swa(h, row(norm_mix[l]), w_pad, wal_pad,
                                                 row(b_alpha[l]), rel_bias.astype(F32),
                                                 sinks[l].astype(F32))
        o_gla = _gla(gq, gk, gv, gr, la, row(gla_gain[l]))
        kc, vc = _mem_kv(mem, row(norm_mem[l]), w_k_c[l].astype(BF16), w_v_c[l].astype(BF16))
        h = _out_cross(h, o_gla, o_swa, kc, vc, w_out[l].astype(BF16), row(norm_cross[l]),
                       w_q_c[l].astype(BF16), w_o_c[l].astype(BF16))
        h = _ffn(h, row(norm_ffn[l]), w_up[l].astype(BF16), w_down[l].astype(BF16),
                 conv_w[l].astype(F32), row(conv_b[l]), row(norm_final),
                 final_norm=(l == depth - 1))
    return h
```

```python
import functools
import math

import numpy as np
import jax
import jax.numpy as jnp
from jax import lax
from jax.experimental import pallas as pl
from jax.experimental.pallas import tpu as pltpu

F32 = jnp.float32
BF16 = jnp.bfloat16

D_MODEL = 1024
MEM_LEN = 256
EPS = 1e-6
GLA_HEADS = 4
GLA_DK = 64
GLA_DV = 128
GLA_RANK = 16
GLA_TAU = 16.0
GLA_CHUNK = 64
SWA_HEADS = 8
SWA_KV_HEADS = 2
SWA_DH = 64
WINDOW = 128
REL_BUCKETS = 32
REL_MAX_DIST = 128
CROSS_HEADS = 4
CROSS_DH = D_MODEL // CROSS_HEADS
D_FF = 2816
CONV_WIDTH = 3

GLA_KW = GLA_HEADS * GLA_DK
GLA_VW = GLA_HEADS * GLA_DV
SWA_QW = SWA_HEADS * SWA_DH
SWA_KVW = SWA_KV_HEADS * SWA_DH
IN_SPLITS = (GLA_KW, GLA_KW, GLA_VW, GLA_VW, GLA_RANK, SWA_QW, SWA_KVW, SWA_KVW)

LOG2E = math.log2(math.e)
LANES = 128
RANK_PAD = LANES
_PAD_SPLITS = (GLA_KW, GLA_KW, GLA_VW, GLA_VW, RANK_PAD, SWA_QW, SWA_KVW, SWA_KVW)
_OFF = tuple(int(v) for v in np.cumsum((0,) + _PAD_SPLITS))
D_IN_PAD = _OFF[-1]

TOK_TILE = 1024
FFN_TILE = 512
GLA_TILE = 1024
FFN_CHUNK = 256
N_FFN_CHUNKS = D_FF // FFN_CHUNK
FFN_DOWN_GROUP = 2
FFN_UBUFS = 4
CARRY_ROWS = 8
VMEM_LIMIT = 56 * 1024 * 1024

_NT = (((1,), (1,)), ((), ()))
_TN = (((0,), (0,)), ((), ()))


def _rms(x, g):
    return x * lax.rsqrt(jnp.mean(x * x, axis=-1, keepdims=True) + EPS) * g


def _dot(a, b):
    return jnp.dot(a, b, preferred_element_type=F32)


def _split_bf16(x):
    hi = x.astype(BF16)
    return hi, (x - hi.astype(F32)).astype(BF16)


def _silu(x):
    return x * (1.0 / (1.0 + jnp.exp2(x * -LOG2E)))


def _const_spec(shape):
    zeros = (0,) * len(shape)
    return pl.BlockSpec(shape, lambda *_: zeros)


def _params(sem):
    return pltpu.CompilerParams(dimension_semantics=sem, vmem_limit_bytes=VMEM_LIMIT)


def _t5_bucket_table():
    L = WINDOW
    dist = (jnp.arange(L)[:, None] + L) - jnp.arange(2 * L)[None, :]
    n = jnp.maximum(dist, 0)
    max_exact = REL_BUCKETS // 2
    nf = jnp.maximum(n, 1).astype(F32)
    large = max_exact + (jnp.log(nf / max_exact) / math.log(REL_MAX_DIST / max_exact)
                         * (REL_BUCKETS - max_exact)).astype(jnp.int32)
    large = jnp.minimum(large, REL_BUCKETS - 1)
    bucket = jnp.where(n < max_exact, n, large)
    return jnp.where((dist >= 0) & (dist < WINDOW), bucket, -1).astype(jnp.int32)


def _inproj_swa_kernel(sinks_ref, relb_ref, bucket_ref, x_ref, g_ref, w_ref, wal_ref, bal_ref,
                       gq_ref, gk_ref, gv_ref, gr_ref, la_ref, o_ref,
                       bias_ref, kv_prev_ref, s_ref, pr_ref):
    L = WINDOW
    tm = x_ref.shape[0]
    n_blk = tm // L
    pair_w = 2 * SWA_DH
    n_pairs = SWA_QW // pair_w
    pairs_per_kv = n_pairs // SWA_KV_HEADS
    neg_inf = float("-inf")

    @pl.when((pl.program_id(0) == 0) & (pl.program_id(1) == 0))
    def _():
        bucket = bucket_ref[...]
        for h in range(SWA_HEADS):
            def body(b, acc):
                return jnp.where(bucket == b, relb_ref[b, h] * LOG2E, acc)
            bias_ref[h] = lax.fori_loop(0, REL_BUCKETS, body, jnp.full((L, 2 * L), neg_inf, F32))

    @pl.when(pl.program_id(1) == 0)
    def _():
        kv_prev_ref[...] = jnp.zeros_like(kv_prev_ref)

    x = x_ref[...]
    xg = (x * g_ref[...]).astype(BF16)
    inv_rms = jnp.broadcast_to(lax.rsqrt(jnp.mean(x * x, axis=-1, keepdims=True) + EPS),
                               (tm, LANES))

    def proj_cols(start, stop):
        res = _dot(xg, w_ref[:, start:stop])
        return jnp.concatenate([res[:, c:c + LANES] * inv_rms
                                for c in range(0, res.shape[1], LANES)], axis=1)

    def proj(lo, hi):
        return proj_cols(_OFF[lo], _OFF[hi])

    swa = proj(5, 8)
    sq = (swa[:, :SWA_QW] * (SWA_DH ** -0.5 * LOG2E)).astype(BF16)
    k_cat = jnp.concatenate([kv_prev_ref[0], swa[:, SWA_QW:SWA_QW + SWA_KVW]], axis=0)
    v_cat = jnp.concatenate([kv_prev_ref[1], swa[:, SWA_QW + SWA_KVW:]], axis=0)
    kv_prev_ref[0] = k_cat[tm:, :]
    kv_prev_ref[1] = v_cat[tm:, :]

    def dup_heads(cat):
        low = lax.broadcasted_iota(jnp.int32, cat.shape, 1) < SWA_DH
        rolled = pltpu.roll(cat, SWA_DH, 1)
        return low, (jnp.where(low, cat, rolled), jnp.where(low, rolled, cat))

    _, kdup = dup_heads(k_cat)
    low_kv, vdup = dup_heads(v_cat)
    kd = [kg.astype(BF16) for kg in kdup]
    vd = [[jnp.where(low_kv, vg, 1.0).astype(BF16), jnp.where(low_kv, 1.0, vg).astype(BF16)]
          for vg in vdup]

    low = lax.broadcasted_iota(jnp.int32, (L, pair_w), 1) < SWA_DH
    no_prev = jnp.where((pl.program_id(1) == 0)
                        & (lax.broadcasted_iota(jnp.int32, (L, 2 * L), 1) < L), neg_inf, 0.0)
    items = [(j, p) for j in range(n_blk) for p in range(n_pairs)]
    sinks = [sinks_ref[h] * LOG2E for h in range(SWA_HEADS)]
    row_max = {}

    def scores(t, e):
        j, p = items[t]
        s = s_ref[t, e * L:(e + 1) * L, :] + bias_ref[2 * p + e]
        return s + no_prev if j == 0 else s

    def swa_score_dots(ts):
        for t in ts:
            j, p = items[t]
            qp = sq[j * L:(j + 1) * L, p * pair_w:(p + 1) * pair_w]
            zero = jnp.zeros_like(qp)
            q2 = jnp.concatenate([jnp.where(low, qp, zero), jnp.where(low, zero, qp)], axis=0)
            keys = kd[p // pairs_per_kv][j * L:(j + 2) * L, :]
            s_ref[t] = lax.dot_general(q2, keys, _NT, preferred_element_type=F32)

    def swa_row_max(ts):
        for t in ts:
            for e in range(2):
                row_max[t, e] = jnp.maximum(jnp.max(scores(t, e), axis=-1, keepdims=True),
                                            sinks[2 * items[t][1] + e])

    def swa_exp(ts):
        for t in ts:
            for e in range(2):
                pr_ref[2 * t + e] = jnp.exp2(scores(t, e) - row_max[t, e]).astype(BF16)

    def swa_out(ts):
        for t in ts:
            j, p = items[t]
            vals = vd[p // pairs_per_kv]
            outs = [_dot(pr_ref[2 * t + e], vals[e][j * L:(j + 2) * L, :]) for e in range(2)]
            sink_terms = [jnp.exp2(sinks[2 * p + e] - row_max[t, e]) for e in range(2)]
            numer = jnp.where(low, outs[0], outs[1])
            denom = (pltpu.roll(jnp.where(low, outs[1], outs[0]), SWA_DH, 1)
                     + jnp.where(low, sink_terms[0], sink_terms[1]))
            o_ref[j * L:(j + 1) * L, p * pair_w:(p + 1) * pair_w] = (
                numer * (1.0 / denom)).astype(o_ref.dtype)

    first, second = range(0, len(items) // 2), range(len(items) // 2, len(items))
    a_low = proj(4, 5)
    swa_score_dots(range(len(items)))
    a_hi, a_lo = _split_bf16(a_low)
    w_hi, w_lo = _split_bf16(wal_ref[...])
    z = _dot(a_hi, w_hi) + _dot(a_lo, w_hi) + _dot(a_hi, w_lo) + bal_ref[...]
    la_ref[...] = -(jnp.maximum(-z, 0.0) + jnp.log1p(jnp.exp(-jnp.abs(z)))) * (1.0 / GLA_TAU)
    half_v = GLA_VW // 2
    gq_ref[...] = proj(0, 1).astype(BF16)
    swa_row_max(first)
    gk_ref[...] = proj(1, 2).astype(BF16)
    swa_row_max(second)
    gv_ref[:, :half_v] = proj_cols(_OFF[2], _OFF[2] + half_v).astype(BF16)
    swa_exp(first)
    gv_ref[:, half_v:] = proj_cols(_OFF[2] + half_v, _OFF[3]).astype(BF16)
    swa_exp(second)
    gr_ref[:, :half_v] = proj_cols(_OFF[3], _OFF[3] + half_v).astype(BF16)
    swa_out(first)
    swa_out(second)
    gr_ref[:, half_v:] = proj_cols(_OFF[3] + half_v, _OFF[4]).astype(BF16)


def _in_proj_swa(h, g, w_pad, wal_pad, bal, rel_bias, sinks):
    B, T, D = h.shape
    tm = TOK_TILE
    L = WINDOW
    per = tm // L

    def tok(width):
        return pl.BlockSpec((None, tm, width), lambda b, i: (b, i, 0))

    def out(width, dt):
        return jax.ShapeDtypeStruct((B, T, width), dt)

    smem = pl.BlockSpec(memory_space=pltpu.SMEM)
    return pl.pallas_call(
        _inproj_swa_kernel,
        grid=(B, T // tm),
        in_specs=[smem, smem, _const_spec((L, 2 * L)),
                  tok(D), _const_spec((1, D)), _const_spec((D, D_IN_PAD)),
                  _const_spec((RANK_PAD, GLA_KW)), _const_spec((1, GLA_KW))],
        out_specs=[tok(GLA_KW), tok(GLA_KW), tok(GLA_VW), tok(GLA_VW), tok(GLA_KW), tok(SWA_QW)],
        out_shape=[out(GLA_KW, BF16), out(GLA_KW, BF16), out(GLA_VW, BF16), out(GLA_VW, BF16),
                   out(GLA_KW, F32), out(SWA_QW, BF16)],
        scratch_shapes=[pltpu.VMEM((SWA_HEADS, L, 2 * L), F32),
                        pltpu.VMEM((2, L, SWA_KVW), F32),
                        pltpu.VMEM((per * SWA_HEADS // 2, 2 * L, 2 * L), F32),
                        pltpu.VMEM((per * SWA_HEADS, L, 2 * L), BF16)],
        compiler_params=_params(("arbitrary", "arbitrary")),
        name="in_proj_swa",
    )(sinks, rel_bias, _t5_bucket_table(), h, g, w_pad, wal_pad, bal)


def _gla_kernel(q_ref, k_ref, v_ref, r_ref, la_ref, gain_ref, o_ref,
                st_ref, q4_ref, kv_ref, dec_ref, oi_ref):
    C = GLA_CHUNK
    H = GLA_HEADS
    n_chunks = q_ref.shape[0] // C

    @pl.when(pl.program_id(1) == 0)
    def _():
        st_ref[...] = jnp.zeros_like(st_ref)

    tril = (lax.broadcasted_iota(jnp.int32, (C, C), 0)
            >= lax.broadcasted_iota(jnp.int32, (C, C), 1)).astype(BF16)
    tril2 = jnp.concatenate([tril, tril], axis=1)
    causal4 = ((lax.broadcasted_iota(jnp.int32, (H * C, C), 0) & (C - 1))
               >= lax.broadcasted_iota(jnp.int32, (H * C, C), 1))
    head_of_lane = lax.broadcasted_iota(jnp.int32, (C, GLA_KW), 1) // GLA_DK
    head_of_st_lane = lax.broadcasted_iota(jnp.int32, (GLA_DV, GLA_KW), 1) // GLA_DK
    gain = gain_ref[...]

    rows_of = [slice(c * C, (c + 1) * C) for c in range(n_chunks)]

    def state_free(chunks):
        bcum = {}
        for c in chunks:
            la_hi, la_lo = _split_bf16(la_ref[rows_of[c], :])
            bcum[c] = _dot(tril2, jnp.concatenate([la_hi, la_lo], axis=0))

        k_inv, k_end = {}, {}
        for c in chunks:
            b_last = bcum[c][C - 1:C, :]
            dec_ref[c] = jnp.exp(b_last)
            q = q_ref[rows_of[c], :].astype(F32) * (GLA_DK ** -0.5)
            k = k_ref[rows_of[c], :].astype(F32)
            q_dec = (q * jnp.exp(bcum[c])).astype(BF16)
            k_inv[c] = (k * jnp.exp(-bcum[c])).astype(BF16)
            k_end[c] = (k * jnp.exp(b_last - bcum[c])).astype(BF16)
            q4_ref[c] = jnp.concatenate(
                [jnp.where(head_of_lane == h, q_dec, jnp.zeros_like(q_dec)) for h in range(H)],
                axis=0)

        att = {c: lax.dot_general(q4_ref[c], k_inv[c], _NT, preferred_element_type=F32)
               for c in chunks}
        kvt_all = {c: lax.dot_general(v_ref[rows_of[c], :], k_end[c], _TN,
                                      preferred_element_type=F32) for c in chunks}

        for c in chunks:
            att_c = jnp.where(causal4, att[c], 0.0).astype(BF16)
            for h in range(H):
                cols = slice(h * GLA_DV, (h + 1) * GLA_DV)
                oi_ref[rows_of[c], cols] = _dot(att_c[h * C:(h + 1) * C, :],
                                                v_ref[rows_of[c], cols])
            kvt = kvt_all[c][(H - 1) * GLA_DV:, :]
            for h in range(H - 2, -1, -1):
                kvt = jnp.where(head_of_st_lane == h, kvt_all[c][h * GLA_DV:(h + 1) * GLA_DV, :],
                                kvt)
            kv_ref[c] = kvt

    def recurrence(chunks, st):
        inter = {}
        for c in chunks:
            inter[c] = lax.dot_general(q4_ref[c], st.astype(BF16), _NT,
                                       preferred_element_type=F32)
            st = st * dec_ref[c] + kv_ref[c]
        return inter, st

    def finish(chunks, inter):
        for c in chunks:
            for h in range(H):
                cols = slice(h * GLA_DV, (h + 1) * GLA_DV)
                o = oi_ref[rows_of[c], cols] + inter[c][h * C:(h + 1) * C, :]
                o = o * lax.rsqrt(jnp.mean(o * o, axis=-1, keepdims=True) + EPS) * gain
                o_ref[rows_of[c], cols] = (
                    o * _silu(r_ref[rows_of[c], cols].astype(F32))).astype(o_ref.dtype)

    chunks = range(n_chunks)
    state_free(chunks)
    inter, st = recurrence(chunks, st_ref[...])
    st_ref[...] = st
    finish(chunks, inter)


def _gla(gq, gk, gv, gr, la, gain):
    B, T, _ = gq.shape
    tm = GLA_TILE

    def tok(width):
        return pl.BlockSpec((None, tm, width), lambda b, i: (b, i, 0))

    return pl.pallas_call(
        _gla_kernel,
        grid=(B, T // tm),
        in_specs=[tok(GLA_KW), tok(GLA_KW), tok(GLA_VW), tok(GLA_VW), tok(GLA_KW),
                  _const_spec((1, GLA_DV))],
        out_specs=tok(GLA_VW),
        out_shape=jax.ShapeDtypeStruct((B, T, GLA_VW), BF16),
        scratch_shapes=[pltpu.VMEM((GLA_DV, GLA_KW), F32),
                        pltpu.VMEM((tm // GLA_CHUNK, GLA_HEADS * GLA_CHUNK, GLA_KW), BF16),
                        pltpu.VMEM((tm // GLA_CHUNK, GLA_DV, GLA_KW), F32),
                        pltpu.VMEM((tm // GLA_CHUNK, 1, GLA_KW), F32),
                        pltpu.VMEM((tm, GLA_VW), F32)],
        compiler_params=_params(("parallel", "arbitrary")),
        name="gla",
    )(gq, gk, gv, gr, la, gain)


def _memkv_kernel(mem_ref, g_ref, wk_ref, wv_ref, k_ref, v_ref):
    mn = _rms(mem_ref[...], g_ref[...]).astype(BF16)
    k_ref[...] = _dot(mn, wk_ref[...]).astype(BF16)
    v_ref[...] = _dot(mn, wv_ref[...]).astype(BF16)


def _mem_kv(mem, g, wk, wv):
    B, M, D = mem.shape
    blk = pl.BlockSpec((None, M, D), lambda b: (b, 0, 0))
    return pl.pallas_call(
        _memkv_kernel,
        grid=(B,),
        in_specs=[blk, _const_spec((1, D)), _const_spec((D, D)), _const_spec((D, D))],
        out_specs=[blk, blk],
        out_shape=[jax.ShapeDtypeStruct((B, M, D), BF16)] * 2,
        compiler_params=_params(("parallel",)),
        name="mem_kv",
    )(mem, g, wk, wv)


def _outcross_kernel(h_ref, og_ref, os_ref, kc_ref, vc_ref, wout_ref, g_ref,
                     wq_ref, wo_ref, o_ref):
    mix = jnp.concatenate([og_ref[...], os_ref[...]], axis=1)
    h1 = h_ref[...] + _dot(mix, wout_ref[...])
    inv_rms = lax.rsqrt(jnp.mean(h1 * h1, axis=-1, keepdims=True) + EPS)
    hg = (h1 * g_ref[...]).astype(BF16)
    q = (_dot(hg, wq_ref[...]) * (inv_rms * (CROSS_DH ** -0.5 * LOG2E))).astype(BF16)
    cols = [slice(hd * CROSS_DH, (hd + 1) * CROSS_DH) for hd in range(CROSS_HEADS)]
    scores = [lax.dot_general(q[:, c], kc_ref[:, c], _NT, preferred_element_type=F32)
              for c in cols]
    parts = []
    for s, c in zip(scores, cols):
        m = jnp.max(s, axis=-1, keepdims=True)
        p = jnp.exp2(s - m)
        denom = jnp.sum(p, axis=-1, keepdims=True)
        o = (_dot(p.astype(BF16), vc_ref[:, c]) * (1.0 / denom)).astype(BF16)
        parts.append(_dot(o, wo_ref[c, :]))
    attn = parts[0]
    for part in parts[1:]:
        attn = attn + part
    o_ref[...] = h1 + attn


def _out_cross(h, og, osw, kc, vc, wout, g, wq, wo):
    B, T, D = h.shape
    tm = TOK_TILE

    def tok(width):
        return pl.BlockSpec((None, tm, width), lambda b, i: (b, i, 0))

    memblk = pl.BlockSpec((None, MEM_LEN, D), lambda b, i: (b, 0, 0))
    return pl.pallas_call(
        _outcross_kernel,
        grid=(B, T // tm),
        in_specs=[tok(D), tok(GLA_VW), tok(SWA_QW), memblk, memblk,
                  _const_spec((GLA_VW + SWA_QW, D)), _const_spec((1, D)),
                  _const_spec((D, D)), _const_spec((D, D))],
        out_specs=tok(D),
        out_shape=jax.ShapeDtypeStruct((B, T, D), F32),
        compiler_params=_params(("parallel", "parallel")),
        name="out_cross",
    )(h, og, osw, kc, vc, wout, g, wq, wo)


def _ffn_kernel(h_ref, g_ref, wu_ref, wd_ref, cw_ref, cb_ref, gf_ref,
                o_ref, hn_ref, acc_ref, ubuf_ref, carry_ref, *, final_norm):
    tm = h_ref.shape[0]
    n_slab = 2 * FFN_CHUNK // LANES
    half = n_slab // 2

    @pl.when(pl.program_id(1) == 0)
    def _():
        carry_ref[...] = jnp.zeros_like(carry_ref)

    hn_ref[...] = _rms(h_ref[...], g_ref[...]).astype(BF16)

    def cols_of(c, s):
        start = (s // half) * D_FF + c * FFN_CHUNK + (s % half) * LANES
        return slice(start, start + LANES)

    def up(c):
        hn = hn_ref[...]
        for part in range(2):
            start = part * D_FF + c * FFN_CHUNK
            u = _dot(hn, wu_ref[:, start:start + FFN_CHUNK])
            for j in range(half):
                s = part * half + j
                us = u[:, j * LANES:(j + 1) * LANES]
                ubuf_ref[c % FFN_UBUFS, s, 0:CARRY_ROWS, :] = carry_ref[c, s]
                ubuf_ref[c % FFN_UBUFS, s, CARRY_ROWS:, :] = us
                carry_ref[c, s] = us[tm - CARRY_ROWS:, :]

    def conv(c, s):
        lanes = cols_of(c, s)
        taps = [ubuf_ref[c % FFN_UBUFS, s, CARRY_ROWS - d:CARRY_ROWS - d + tm, :] for d in range(CONV_WIDTH)]
        return (cw_ref[2:3, lanes] * taps[0]
                + (cw_ref[1:2, lanes] * taps[1] + (cw_ref[0:1, lanes] * taps[2] + cb_ref[:, lanes])))

    lookahead = FFN_UBUFS - 1
    for c in range(lookahead):
        up(c)
    last = N_FFN_CHUNKS - 1
    pending = []
    for c in range(N_FFN_CHUNKS):
        if c + lookahead <= last:
            up(c + lookahead)
        pending.append([(_silu(conv(c, s)) * conv(c, half + s)).astype(BF16)
                        for s in range(half)])
        if len(pending) == FFN_DOWN_GROUP or c == last:
            first_chunk = c + 1 - len(pending)
            act = jnp.concatenate([a for chunk_acts in pending for a in chunk_acts], axis=1)
            down = _dot(act, wd_ref[first_chunk * FFN_CHUNK:(c + 1) * FFN_CHUNK, :])
            pending = []
            base = h_ref[...] if first_chunk == 0 else acc_ref[...]
            if c < last:
                acc_ref[...] = base + down
            else:
                out = base + down
                if final_norm:
                    out = _rms(out, gf_ref[...])
                o_ref[...] = out


def _ffn(h, g, wu, wd, cw, cb, gf, final_norm):
    B, T, D = h.shape
    tm = FFN_TILE
    nc, fc = N_FFN_CHUNKS, FFN_CHUNK
    n_slab = 2 * fc // LANES
    tok = pl.BlockSpec((None, tm, D), lambda b, i: (b, i, 0))

    def once(shape):
        zeros = (0,) * len(shape)
        return pl.BlockSpec(shape, lambda *_: zeros, pipeline_mode=pl.Buffered(1))

    return pl.pallas_call(
        functools.partial(_ffn_kernel, final_norm=final_norm),
        grid=(B, T // tm),
        in_specs=[tok, _const_spec((1, D)), once((D, 2 * D_FF)), once((D_FF, D)),
                  _const_spec((CONV_WIDTH, 2 * D_FF)), _const_spec((1, 2 * D_FF)),
                  _const_spec((1, D))],
        out_specs=tok,
        out_shape=jax.ShapeDtypeStruct((B, T, D), F32),
        scratch_shapes=[pltpu.VMEM((tm, D), BF16), pltpu.VMEM((tm, D), F32),
                        pltpu.VMEM((FFN_UBUFS, n_slab, CARRY_ROWS + tm, LANES), F32),
                        pltpu.VMEM((nc, n_slab, CARRY_ROWS, LANES), F32)],
        compiler_params=_params(("parallel", "arbitrary")),
        name="ffn",
    )(h, g, wu, wd, cw, cb, gf)


def _pad_in_proj(w_in):
    parts = jnp.split(w_in, [int(s) for s in np.cumsum(IN_SPLITS)[:-1]], axis=-1)
    parts[4] = jnp.pad(parts[4], ((0, 0), (0, RANK_PAD - GLA_RANK)))
    return jnp.concatenate(parts, axis=-1).astype(BF16)


def kernel(x, mem, norm_mix, w_in, w_alpha, b_alpha, gla_gain, rel_bias, sinks, w_out, norm_cross,
           norm_mem, w_q_c, w_k_c, w_v_c, w_o_c, norm_ffn, w_up, conv_w, conv_b, w_down, norm_final):
    depth = w_in.shape[0]
    row = lambda v: v.reshape(1, -1).astype(F32)
    h = x
    for l in range(depth):
        w_pad = _pad_in_proj(w_in[l])
        wal_pad = jnp.pad(w_alpha[l], ((0, RANK_PAD - GLA_RANK), (0, 0)))
        gq, gk, gv, gr, la, o_swa = _in_proj_swa(h, row(norm_mix[l]), w_pad, wal_pad,
                                                 row(b_alpha[l]), rel_bias.astype(F32),
                                                 sinks[l].astype(F32))
        o_gla = _gla(gq, gk, gv, gr, la, row(gla_gain[l]))
        kc, vc = _mem_kv(mem, row(norm_mem[l]), w_k_c[l].astype(BF16), w_v_c[l].astype(BF16))
        h = _out_cross(h, o_gla, o_swa, kc, vc, w_out[l].astype(BF16), row(norm_cross[l]),
                       w_q_c[l].astype(BF16), w_o_c[l].astype(BF16))
        h = _ffn(h, row(norm_ffn[l]), w_up[l].astype(BF16), w_down[l].astype(BF16),
                 conv_w[l].astype(F32), row(conv_b[l]), row(norm_final),
                 final_norm=(l == depth - 1))
    return h
```

```python
import functools
import math

import numpy as np
import jax
import jax.numpy as jnp
from jax import lax
from jax.experimental import pallas as pl
from jax.experimental.pallas import tpu as pltpu

F32 = jnp.float32
BF16 = jnp.bfloat16

D_MODEL = 1024
MEM_LEN = 256
EPS = 1e-6
GLA_HEADS = 4
GLA_DK = 64
GLA_DV = 128
GLA_RANK = 16
GLA_TAU = 16.0
GLA_CHUNK = 64
SWA_HEADS = 8
SWA_KV_HEADS = 2
SWA_DH = 64
WINDOW = 128
REL_BUCKETS = 32
REL_MAX_DIST = 128
CROSS_HEADS = 4
CROSS_DH = D_MODEL // CROSS_HEADS
D_FF = 2816
CONV_WIDTH = 3

GLA_KW = GLA_HEADS * GLA_DK
GLA_VW = GLA_HEADS * GLA_DV
SWA_QW = SWA_HEADS * SWA_DH
SWA_KVW = SWA_KV_HEADS * SWA_DH
IN_SPLITS = (GLA_KW, GLA_KW, GLA_VW, GLA_VW, GLA_RANK, SWA_QW, SWA_KVW, SWA_KVW)

LOG2E = math.log2(math.e)
LANES = 128
RANK_PAD = LANES
_IN_OFF = tuple(int(v) for v in np.cumsum((0,) + IN_SPLITS))
_OFF = _IN_OFF[:5]
D_SWA = SWA_QW + 2 * SWA_KVW

TOK_TILE = 1024
FFN_TILE = 512
GLA_TILE = 2048
FFN_CHUNK = 256
N_FFN_CHUNKS = D_FF // FFN_CHUNK
FFN_DOWN_GROUP = 2
FFN_UBUFS = 4
CARRY_ROWS = 8
VMEM_LIMIT = 56 * 1024 * 1024

_NT = (((1,), (1,)), ((), ()))
_TN = (((0,), (0,)), ((), ()))


def _rms(x, g):
    return x * lax.rsqrt(jnp.mean(x * x, axis=-1, keepdims=True) + EPS) * g


def _dot(a, b):
    return jnp.dot(a, b, preferred_element_type=F32)


def _split_bf16(x):
    hi = x.astype(BF16)
    return hi, (x - hi.astype(F32)).astype(BF16)


def _silu(x):
    return x * (1.0 / (1.0 + jnp.exp2(x * -LOG2E)))


def _const_spec(shape):
    zeros = (0,) * len(shape)
    return pl.BlockSpec(shape, lambda *_: zeros)


def _params(sem):
    return pltpu.CompilerParams(dimension_semantics=sem, vmem_limit_bytes=VMEM_LIMIT)


def _t5_bucket_table():
    L = WINDOW
    dist = (jnp.arange(L)[:, None] + L) - jnp.arange(2 * L)[None, :]
    n = jnp.maximum(dist, 0)
    max_exact = REL_BUCKETS // 2
    nf = jnp.maximum(n, 1).astype(F32)
    large = max_exact + (jnp.log(nf / max_exact) / math.log(REL_MAX_DIST / max_exact)
                         * (REL_BUCKETS - max_exact)).astype(jnp.int32)
    large = jnp.minimum(large, REL_BUCKETS - 1)
    bucket = jnp.where(n < max_exact, n, large)
    return jnp.where((dist >= 0) & (dist < WINDOW), bucket, -1).astype(jnp.int32)


def _inproj_swa_kernel(sinks_ref, relb_ref, bucket_ref, x_ref, g_ref, wg_ref, wa_ref, ws_ref,
                       wal_ref, bal_ref,
                       gq_ref, gk_ref, gv_ref, gr_ref, la_ref, o_ref,
                       bias_ref, kv_prev_ref, s_ref, pr_ref):
    L = WINDOW
    tm = x_ref.shape[0]
    n_blk = tm // L
    pair_w = 2 * SWA_DH
    n_pairs = SWA_QW // pair_w
    pairs_per_kv = n_pairs // SWA_KV_HEADS
    neg_inf = float("-inf")

    @pl.when((pl.program_id(0) == 0) & (pl.program_id(1) == 0))
    def _():
        bucket = bucket_ref[...]
        for h in range(SWA_HEADS):
            def body(b, acc):
                return jnp.where(bucket == b, relb_ref[b, h] * LOG2E, acc)
            bias_ref[h] = lax.fori_loop(0, REL_BUCKETS, body, jnp.full((L, 2 * L), neg_inf, F32))

    @pl.when(pl.program_id(1) == 0)
    def _():
        kv_prev_ref[...] = jnp.zeros_like(kv_prev_ref)

    x = x_ref[...]
    xg = (x * g_ref[...]).astype(BF16)
    inv_rms = jnp.broadcast_to(lax.rsqrt(jnp.mean(x * x, axis=-1, keepdims=True) + EPS),
                               (tm, LANES))

    def scaled_dot(w):
        res = _dot(xg, w)
        return jnp.concatenate([res[:, c:c + LANES] * inv_rms
                                for c in range(0, res.shape[1], LANES)], axis=1)

    def proj_cols(start, stop):
        return scaled_dot(wg_ref[:, start:stop])

    def proj(lo, hi):
        return proj_cols(_OFF[lo], _OFF[hi])

    swa = scaled_dot(ws_ref[...])
    sq = (swa[:, :SWA_QW] * (SWA_DH ** -0.5 * LOG2E)).astype(BF16)
    k_cat = jnp.concatenate([kv_prev_ref[0], swa[:, SWA_QW:SWA_QW + SWA_KVW]], axis=0)
    v_cat = jnp.concatenate([kv_prev_ref[1], swa[:, SWA_QW + SWA_KVW:]], axis=0)
    kv_prev_ref[0] = k_cat[tm:, :]
    kv_prev_ref[1] = v_cat[tm:, :]

    def dup_heads(cat):
        low = lax.broadcasted_iota(jnp.int32, cat.shape, 1) < SWA_DH
        rolled = pltpu.roll(cat, SWA_DH, 1)
        return low, (jnp.where(low, cat, rolled), jnp.where(low, rolled, cat))

    _, kdup = dup_heads(k_cat)
    low_kv, vdup = dup_heads(v_cat)
    kd = [kg.astype(BF16) for kg in kdup]
    vd = [[jnp.where(low_kv, vg, 1.0).astype(BF16), jnp.where(low_kv, 1.0, vg).astype(BF16)]
          for vg in vdup]

    low = lax.broadcasted_iota(jnp.int32, (L, pair_w), 1) < SWA_DH
    no_prev = jnp.where((pl.program_id(1) == 0)
                        & (lax.broadcasted_iota(jnp.int32, (L, 2 * L), 1) < L), neg_inf, 0.0)
    items = [(j, p) for j in range(n_blk) for p in range(n_pairs)]
    sinks = [sinks_ref[h] * LOG2E for h in range(SWA_HEADS)]
    row_max = {}

    def scores(t, e):
        j, p = items[t]
        s = s_ref[t, e * L:(e + 1) * L, :] + bias_ref[2 * p + e]
        return s + no_prev if j == 0 else s

    def swa_score_dots(ts):
        for t in ts:
            j, p = items[t]
            qp = sq[j * L:(j + 1) * L, p * pair_w:(p + 1) * pair_w]
            zero = jnp.zeros_like(qp)
            q2 = jnp.concatenate([jnp.where(low, qp, zero), jnp.where(low, zero, qp)], axis=0)
            keys = kd[p // pairs_per_kv][j * L:(j + 2) * L, :]
            s_ref[t] = lax.dot_general(q2, keys, _NT, preferred_element_type=F32)

    def swa_row_max(ts):
        for t in ts:
            for e in range(2):
                row_max[t, e] = jnp.maximum(jnp.max(scores(t, e), axis=-1, keepdims=True),
                                            sinks[2 * items[t][1] + e])

    def swa_exp(ts):
        for t in ts:
            for e in range(2):
                pr_ref[2 * t + e] = jnp.exp2(scores(t, e) - row_max[t, e]).astype(BF16)

    def swa_out(ts):
        for t in ts:
            j, p = items[t]
            vals = vd[p // pairs_per_kv]
            outs = [_dot(pr_ref[2 * t + e], vals[e][j * L:(j + 2) * L, :]) for e in range(2)]
            sink_terms = [jnp.exp2(sinks[2 * p + e] - row_max[t, e]) for e in range(2)]
            numer = jnp.where(low, outs[0], outs[1])
            denom = (pltpu.roll(jnp.where(low, outs[1], outs[0]), SWA_DH, 1)
                     + jnp.where(low, sink_terms[0], sink_terms[1]))
            o_ref[j * L:(j + 1) * L, p * pair_w:(p + 1) * pair_w] = (
                numer * (1.0 / denom)).astype(o_ref.dtype)

    first, second = range(0, len(items) // 2), range(len(items) // 2, len(items))
    a_low = scaled_dot(wa_ref[...])
    swa_score_dots(range(len(items)))
    a_hi, a_lo = _split_bf16(a_low)
    w_hi, w_lo = _split_bf16(wal_ref[...])
    z = _dot(a_hi, w_hi) + _dot(a_lo, w_hi) + _dot(a_hi, w_lo) + bal_ref[...]
    la_ref[...] = -(jnp.maximum(-z, 0.0) + jnp.log1p(jnp.exp(-jnp.abs(z)))) * (1.0 / GLA_TAU)
    half_v = GLA_VW // 2
    gq_ref[...] = proj(0, 1).astype(BF16)
    swa_row_max(first)
    gk_ref[...] = proj(1, 2).astype(BF16)
    swa_row_max(second)
    gv_ref[:, :half_v] = proj_cols(_OFF[2], _OFF[2] + half_v).astype(BF16)
    swa_exp(first)
    gv_ref[:, half_v:] = proj_cols(_OFF[2] + half_v, _OFF[3]).astype(BF16)
    swa_exp(second)
    gr_ref[:, :half_v] = proj_cols(_OFF[3], _OFF[3] + half_v).astype(BF16)
    swa_out(first)
    swa_out(second)
    gr_ref[:, half_v:] = proj_cols(_OFF[3] + half_v, _OFF[4]).astype(BF16)


def _in_proj_swa(h, g, w_parts, wal_pad, bal, rel_bias, sinks):
    B, T, D = h.shape
    tm = TOK_TILE
    L = WINDOW
    per = tm // L

    def tok(width):
        return pl.BlockSpec((None, tm, width), lambda b, i: (b, i, 0))

    def out(width, dt):
        return jax.ShapeDtypeStruct((B, T, width), dt)

    smem = pl.BlockSpec(memory_space=pltpu.SMEM)
    return pl.pallas_call(
        _inproj_swa_kernel,
        grid=(B, T // tm),
        in_specs=[smem, smem, _const_spec((L, 2 * L)),
                  tok(D), _const_spec((1, D)), _const_spec((D, _OFF[-1])),
                  _const_spec((D, RANK_PAD)), _const_spec((D, D_SWA)),
                  _const_spec((RANK_PAD, GLA_KW)), _const_spec((1, GLA_KW))],
        out_specs=[tok(GLA_KW), tok(GLA_KW), tok(GLA_VW), tok(GLA_VW), tok(GLA_KW), tok(SWA_QW)],
        out_shape=[out(GLA_KW, BF16), out(GLA_KW, BF16), out(GLA_VW, BF16), out(GLA_VW, BF16),
                   out(GLA_KW, F32), out(SWA_QW, BF16)],
        scratch_shapes=[pltpu.VMEM((SWA_HEADS, L, 2 * L), F32),
                        pltpu.VMEM((2, L, SWA_KVW), F32),
                        pltpu.VMEM((per * SWA_HEADS // 2, 2 * L, 2 * L), F32),
                        pltpu.VMEM((per * SWA_HEADS, L, 2 * L), BF16)],
        compiler_params=_params(("arbitrary", "arbitrary")),
        name="in_proj_swa",
    )(sinks, rel_bias, _t5_bucket_table(), h, g, *w_parts, wal_pad, bal)


def _gla_kernel(q_ref, k_ref, v_ref, r_ref, la_ref, gain_ref, o_ref,
                st_ref, q4_ref, kv_ref, dec_ref, oi_ref):
    C = GLA_CHUNK
    H = GLA_HEADS
    n_chunks = q_ref.shape[0] // C

    @pl.when(pl.program_id(1) == 0)
    def _():
        st_ref[...] = jnp.zeros_like(st_ref)

    tril = (lax.broadcasted_iota(jnp.int32, (C, C), 0)
            >= lax.broadcasted_iota(jnp.int32, (C, C), 1)).astype(BF16)
    tril2 = jnp.concatenate([tril, tril], axis=1)
    causal4 = ((lax.broadcasted_iota(jnp.int32, (H * C, C), 0) & (C - 1))
               >= lax.broadcasted_iota(jnp.int32, (H * C, C), 1))
    head_of_lane = lax.broadcasted_iota(jnp.int32, (C, GLA_KW), 1) // GLA_DK
    head_of_st_lane = lax.broadcasted_iota(jnp.int32, (GLA_DV, GLA_KW), 1) // GLA_DK
    gain = gain_ref[...]

    rows_of = [slice(c * C, (c + 1) * C) for c in range(n_chunks)]

    def state_free(chunks):
        bcum = {}
        for c in chunks:
            la_hi, la_lo = _split_bf16(la_ref[rows_of[c], :])
            bcum[c] = _dot(tril2, jnp.concatenate([la_hi, la_lo], axis=0))

        k_inv, k_end = {}, {}
        for c in chunks:
            b_last = bcum[c][C - 1:C, :]
            dec_ref[c] = jnp.exp(b_last)
            q = q_ref[rows_of[c], :].astype(F32) * (GLA_DK ** -0.5)
            k = k_ref[rows_of[c], :].astype(F32)
            q_dec = (q * jnp.exp(bcum[c])).astype(BF16)
            k_inv[c] = (k * jnp.exp(-bcum[c])).astype(BF16)
            k_end[c] = (k * jnp.exp(b_last - bcum[c])).astype(BF16)
            q4_ref[c] = jnp.concatenate(
                [jnp.where(head_of_lane == h, q_dec, jnp.zeros_like(q_dec)) for h in range(H)],
                axis=0)

        att = {c: lax.dot_general(q4_ref[c], k_inv[c], _NT, preferred_element_type=F32)
               for c in chunks}
        kvt_all = {c: lax.dot_general(v_ref[rows_of[c], :], k_end[c], _TN,
                                      preferred_element_type=F32) for c in chunks}

        for c in chunks:
            att_c = jnp.where(causal4, att[c], 0.0).astype(BF16)
            for h in range(H):
                cols = slice(h * GLA_DV, (h + 1) * GLA_DV)
                oi_ref[rows_of[c], cols] = _dot(att_c[h * C:(h + 1) * C, :],
                                                v_ref[rows_of[c], cols])
            kvt = kvt_all[c][(H - 1) * GLA_DV:, :]
            for h in range(H - 2, -1, -1):
                kvt = jnp.where(head_of_st_lane == h, kvt_all[c][h * GLA_DV:(h + 1) * GLA_DV, :],
                                kvt)
            kv_ref[c] = kvt

    def recurrence(chunks, st):
        inter = {}
        for c in chunks:
            inter[c] = lax.dot_general(q4_ref[c], st.astype(BF16), _NT,
                                       preferred_element_type=F32)
            st = st * dec_ref[c] + kv_ref[c]
        return inter, st

    def finish(chunks, inter):
        for c in chunks:
            for h in range(H):
                cols = slice(h * GLA_DV, (h + 1) * GLA_DV)
                o = oi_ref[rows_of[c], cols] + inter[c][h * C:(h + 1) * C, :]
                o = o * lax.rsqrt(jnp.mean(o * o, axis=-1, keepdims=True) + EPS) * gain
                o_ref[rows_of[c], cols] = (
                    o * _silu(r_ref[rows_of[c], cols].astype(F32))).astype(o_ref.dtype)

    chunks = range(n_chunks)
    state_free(chunks)
    inter, st = recurrence(chunks, st_ref[...])
    st_ref[...] = st
    finish(chunks, inter)


def _gla(gq, gk, gv, gr, la, gain):
    B, T, _ = gq.shape
    tm = GLA_TILE

    def tok(width):
        return pl.BlockSpec((None, tm, width), lambda b, i: (b, i, 0))

    return pl.pallas_call(
        _gla_kernel,
        grid=(B, T // tm),
        in_specs=[tok(GLA_KW), tok(GLA_KW), tok(GLA_VW), tok(GLA_VW), tok(GLA_KW),
                  _const_spec((1, GLA_DV))],
        out_specs=tok(GLA_VW),
        out_shape=jax.ShapeDtypeStruct((B, T, GLA_VW), BF16),
        scratch_shapes=[pltpu.VMEM((GLA_DV, GLA_KW), F32),
                        pltpu.VMEM((tm // GLA_CHUNK, GLA_HEADS * GLA_CHUNK, GLA_KW), BF16),
                        pltpu.VMEM((tm // GLA_CHUNK, GLA_DV, GLA_KW), F32),
                        pltpu.VMEM((tm // GLA_CHUNK, 1, GLA_KW), F32),
                        pltpu.VMEM((tm, GLA_VW), F32)],
        compiler_params=_params(("parallel", "arbitrary")),
        name="gla",
    )(gq, gk, gv, gr, la, gain)


def _memkv_kernel(mem_ref, g_ref, wk_ref, wv_ref, k_ref, v_ref):
    mn = _rms(mem_ref[...], g_ref[...]).astype(BF16)
    k_ref[...] = _dot(mn, wk_ref[...]).astype(BF16)
    v_ref[...] = _dot(mn, wv_ref[...]).astype(BF16)


def _mem_kv(mem, g, wk, wv):
    B, M, D = mem.shape
    rows = B * M
    tm = min(TOK_TILE, rows)
    blk = pl.BlockSpec((tm, D), lambda i: (i, 0))
    k, v = pl.pallas_call(
        _memkv_kernel,
        grid=(rows // tm,),
        in_specs=[blk, _const_spec((1, D)), _const_spec((D, D)), _const_spec((D, D))],
        out_specs=[blk, blk],
        out_shape=[jax.ShapeDtypeStruct((rows, D), BF16)] * 2,
        compiler_params=_params(("parallel",)),
        name="mem_kv",
    )(mem.reshape(rows, D), g, wk, wv)
    return k.reshape(B, M, D), v.reshape(B, M, D)


def _outcross_kernel(h_ref, og_ref, os_ref, kc_ref, vc_ref, wout_ref, g_ref,
                     wq_ref, wo_ref, o_ref):
    mix = jnp.concatenate([og_ref[...], os_ref[...]], axis=1)
    h1 = h_ref[...] + _dot(mix, wout_ref[...])
    inv_rms = lax.rsqrt(jnp.mean(h1 * h1, axis=-1, keepdims=True) + EPS)
    hg = (h1 * g_ref[...]).astype(BF16)
    q = (_dot(hg, wq_ref[...]) * (inv_rms * (CROSS_DH ** -0.5 * LOG2E))).astype(BF16)
    cols = [slice(hd * CROSS_DH, (hd + 1) * CROSS_DH) for hd in range(CROSS_HEADS)]
    scores = [lax.dot_general(q[:, c], kc_ref[:, c], _NT, preferred_element_type=F32)
              for c in cols]
    parts = []
    for s, c in zip(scores, cols):
        m = jnp.max(s, axis=-1, keepdims=True)
        p = jnp.exp2(s - m)
        denom = jnp.sum(p, axis=-1, keepdims=True)
        o = (_dot(p.astype(BF16), vc_ref[:, c]) * (1.0 / denom)).astype(BF16)
        parts.append(_dot(o, wo_ref[c, :]))
    attn = parts[0]
    for part in parts[1:]:
        attn = attn + part
    o_ref[...] = h1 + attn


def _out_cross(h, og, osw, kc, vc, wout, g, wq, wo):
    B, T, D = h.shape
    tm = TOK_TILE

    def tok(width):
        return pl.BlockSpec((None, tm, width), lambda b, i: (b, i, 0))

    memblk = pl.BlockSpec((None, MEM_LEN, D), lambda b, i: (b, 0, 0))
    return pl.pallas_call(
        _outcross_kernel,
        grid=(B, T // tm),
        in_specs=[tok(D), tok(GLA_VW), tok(SWA_QW), memblk, memblk,
                  _const_spec((GLA_VW + SWA_QW, D)), _const_spec((1, D)),
                  _const_spec((D, D)), _const_spec((D, D))],
        out_specs=tok(D),
        out_shape=jax.ShapeDtypeStruct((B, T, D), F32),
        compiler_params=_params(("parallel", "parallel")),
        name="out_cross",
    )(h, og, osw, kc, vc, wout, g, wq, wo)


def _ffn_kernel(h_ref, g_ref, wu_ref, wd_ref, cw_ref, cb_ref, gf_ref,
                o_ref, hn_ref, acc_ref, ubuf_ref, carry_ref, *, final_norm):
    tm = h_ref.shape[0]
    n_slab = 2 * FFN_CHUNK // LANES
    half = n_slab // 2

    @pl.when(pl.program_id(1) == 0)
    def _():
        carry_ref[...] = jnp.zeros_like(carry_ref)

    hn_ref[...] = _rms(h_ref[...], g_ref[...]).astype(BF16)

    def cols_of(c, s):
        start = (s // half) * D_FF + c * FFN_CHUNK + (s % half) * LANES
        return slice(start, start + LANES)

    def up(c):
        hn = hn_ref[...]
        for part in range(2):
            start = part * D_FF + c * FFN_CHUNK
            u = _dot(hn, wu_ref[:, start:start + FFN_CHUNK])
            for j in range(half):
                s = part * half + j
                us = u[:, j * LANES:(j + 1) * LANES]
                ubuf_ref[c % FFN_UBUFS, s, 0:CARRY_ROWS, :] = carry_ref[c, s]
                ubuf_ref[c % FFN_UBUFS, s, CARRY_ROWS:, :] = us
                carry_ref[c, s] = us[tm - CARRY_ROWS:, :]

    def conv(c, s):
        lanes = cols_of(c, s)
        taps = [ubuf_ref[c % FFN_UBUFS, s, CARRY_ROWS - d:CARRY_ROWS - d + tm, :] for d in range(CONV_WIDTH)]
        return (cw_ref[2:3, lanes] * taps[0]
                + (cw_ref[1:2, lanes] * taps[1] + (cw_ref[0:1, lanes] * taps[2] + cb_ref[:, lanes])))

    lookahead = FFN_UBUFS - 1
    for c in range(lookahead):
        up(c)
    last = N_FFN_CHUNKS - 1
    pending = []
    for c in range(N_FFN_CHUNKS):
        if c + lookahead <= last:
            up(c + lookahead)
        pending.append([(_silu(conv(c, s)) * conv(c, half + s)).astype(BF16)
                        for s in range(half)])
        if len(pending) == FFN_DOWN_GROUP or c == last:
            first_chunk = c + 1 - len(pending)
            act = jnp.concatenate([a for chunk_acts in pending for a in chunk_acts], axis=1)
            down = _dot(act, wd_ref[first_chunk * FFN_CHUNK:(c + 1) * FFN_CHUNK, :])
            pending = []
            base = h_ref[...] if first_chunk == 0 else acc_ref[...]
            if c < last:
                acc_ref[...] = base + down
            else:
                out = base + down
                if final_norm:
                    out = _rms(out, gf_ref[...])
                o_ref[...] = out


def _ffn(h, g, wu, wd, cw, cb, gf, final_norm):
    B, T, D = h.shape
    tm = FFN_TILE
    nc, fc = N_FFN_CHUNKS, FFN_CHUNK
    n_slab = 2 * fc // LANES
    tok = pl.BlockSpec((None, tm, D), lambda b, i: (b, i, 0))

    def once(shape):
        zeros = (0,) * len(shape)
        return pl.BlockSpec(shape, lambda *_: zeros, pipeline_mode=pl.Buffered(1))

    return pl.pallas_call(
        functools.partial(_ffn_kernel, final_norm=final_norm),
        grid=(B, T // tm),
        in_specs=[tok, _const_spec((1, D)), once((D, 2 * D_FF)), once((D_FF, D)),
                  _const_spec((CONV_WIDTH, 2 * D_FF)), _const_spec((1, 2 * D_FF)),
                  _const_spec((1, D))],
        out_specs=tok,
        out_shape=jax.ShapeDtypeStruct((B, T, D), F32),
        scratch_shapes=[pltpu.VMEM((tm, D), BF16), pltpu.VMEM((tm, D), F32),
                        pltpu.VMEM((FFN_UBUFS, n_slab, CARRY_ROWS + tm, LANES), F32),
                        pltpu.VMEM((nc, n_slab, CARRY_ROWS, LANES), F32)],
        compiler_params=_params(("parallel", "arbitrary")),
        name="ffn",
    )(h, g, wu, wd, cw, cb, gf)


def _split_in_proj(w_in):
    w_rank = jnp.pad(w_in[:, _IN_OFF[4]:_IN_OFF[5]], ((0, 0), (0, RANK_PAD - GLA_RANK)))
    return (w_in[:, :_IN_OFF[4]].astype(BF16), w_rank.astype(BF16),
            w_in[:, _IN_OFF[5]:].astype(BF16))


def kernel(x, mem, norm_mix, w_in, w_alpha, b_alpha, gla_gain, rel_bias, sinks, w_out, norm_cross,
           norm_mem, w_q_c, w_k_c, w_v_c, w_o_c, norm_ffn, w_up, conv_w, conv_b, w_down, norm_final):
    depth = w_in.shape[0]
    row = lambda v: v.reshape(1, -1).astype(F32)
    h = x
    for l in range(depth):
        wal_pad = jnp.pad(w_alpha[l], ((0, RANK_PAD - GLA_RANK), (0, 0)))
        gq, gk, gv, gr, la, o_swa = _in_proj_swa(h, row(norm_mix[l]), _split_in_proj(w_in[l]),
                                                 wal_pad,
                                                 row(b_alpha[l]), rel_bias.astype(F32),
                                                 sinks[l].astype(F32))
        o_gla = _gla(gq, gk, gv, gr, la, row(gla_gain[l]))
        kc, vc = _mem_kv(mem, row(norm_mem[l]), w_k_c[l].astype(BF16), w_v_c[l].astype(BF16))
        h = _out_cross(h, o_gla, o_swa, kc, vc, w_out[l].astype(BF16), row(norm_cross[l]),
                       w_q_c[l].astype(BF16), w_o_c[l].astype(BF16))
        h = _ffn(h, row(norm_ffn[l]), w_up[l].astype(BF16), w_down[l].astype(BF16),
                 conv_w[l].astype(F32), row(conv_b[l]), row(norm_final),
                 final_norm=(l == depth - 1))
    return h
```

```python
import functools
import math

import numpy as np
import jax
import jax.numpy as jnp
from jax import lax
from jax.experimental import pallas as pl
from jax.experimental.pallas import tpu as pltpu

F32 = jnp.float32
BF16 = jnp.bfloat16

D_MODEL = 1024
MEM_LEN = 256
EPS = 1e-6
GLA_HEADS = 4
GLA_DK = 64
GLA_DV = 128
GLA_RANK = 16
GLA_TAU = 16.0
GLA_CHUNK = 64
SWA_HEADS = 8
SWA_KV_HEADS = 2
SWA_DH = 64
WINDOW = 128
REL_BUCKETS = 32
REL_MAX_DIST = 128
CROSS_HEADS = 4
CROSS_DH = D_MODEL // CROSS_HEADS
D_FF = 2816
CONV_WIDTH = 3

GLA_KW = GLA_HEADS * GLA_DK
GLA_VW = GLA_HEADS * GLA_DV
SWA_QW = SWA_HEADS * SWA_DH
SWA_KVW = SWA_KV_HEADS * SWA_DH
IN_SPLITS = (GLA_KW, GLA_KW, GLA_VW, GLA_VW, GLA_RANK, SWA_QW, SWA_KVW, SWA_KVW)

LOG2E = math.log2(math.e)
LANES = 128
RANK_PAD = LANES
_IN_OFF = tuple(int(v) for v in np.cumsum((0,) + IN_SPLITS))
_OFF = _IN_OFF[:5]
D_SWA = SWA_QW + 2 * SWA_KVW

TOK_TILE = 1024
FFN_TILE = 512
GLA_TILE = 2048
FFN_CHUNK = 256
N_FFN_CHUNKS = D_FF // FFN_CHUNK
FFN_DOWN_GROUP = 2
FFN_ROW_BLOCKS = 2
FFN_UBUFS = 4
CARRY_ROWS = 8
VMEM_LIMIT = 56 * 1024 * 1024

_NT = (((1,), (1,)), ((), ()))
_TN = (((0,), (0,)), ((), ()))


def _rms(x, g):
    return x * lax.rsqrt(jnp.mean(x * x, axis=-1, keepdims=True) + EPS) * g


def _dot(a, b):
    return jnp.dot(a, b, preferred_element_type=F32)


def _split_bf16(x):
    hi = x.astype(BF16)
    return hi, (x - hi.astype(F32)).astype(BF16)


def _silu(x):
    return x * (1.0 / (1.0 + jnp.exp2(x * -LOG2E)))


def _const_spec(shape):
    zeros = (0,) * len(shape)
    return pl.BlockSpec(shape, lambda *_: zeros)


def _params(sem):
    return pltpu.CompilerParams(dimension_semantics=sem, vmem_limit_bytes=VMEM_LIMIT)


def _t5_bucket_table():
    L = WINDOW
    dist = (jnp.arange(L)[:, None] + L) - jnp.arange(2 * L)[None, :]
    n = jnp.maximum(dist, 0)
    max_exact = REL_BUCKETS // 2
    nf = jnp.maximum(n, 1).astype(F32)
    large = max_exact + (jnp.log(nf / max_exact) / math.log(REL_MAX_DIST / max_exact)
                         * (REL_BUCKETS - max_exact)).astype(jnp.int32)
    large = jnp.minimum(large, REL_BUCKETS - 1)
    bucket = jnp.where(n < max_exact, n, large)
    return jnp.where((dist >= 0) & (dist < WINDOW), bucket, -1).astype(jnp.int32)


def _inproj_swa_kernel(sinks_ref, relb_ref, bucket_ref, x_ref, g_ref, wg_ref, wa_ref, ws_ref,
                       wal_ref, bal_ref,
                       gq_ref, gk_ref, gv_ref, gr_ref, la_ref, o_ref,
                       bias_ref, kv_prev_ref, s_ref, pr_ref):
    L = WINDOW
    tm = x_ref.shape[0]
    n_blk = tm // L
    pair_w = 2 * SWA_DH
    n_pairs = SWA_QW // pair_w
    pairs_per_kv = n_pairs // SWA_KV_HEADS
    neg_inf = float("-inf")

    @pl.when((pl.program_id(0) == 0) & (pl.program_id(1) == 0))
    def _():
        bucket = bucket_ref[...]
        for h in range(SWA_HEADS):
            def body(b, acc):
                return jnp.where(bucket == b, relb_ref[b, h] * LOG2E, acc)
            bias_ref[h] = lax.fori_loop(0, REL_BUCKETS, body, jnp.full((L, 2 * L), neg_inf, F32))

    @pl.when(pl.program_id(1) == 0)
    def _():
        kv_prev_ref[...] = jnp.zeros_like(kv_prev_ref)

    x = x_ref[...]
    xg = (x * g_ref[...]).astype(BF16)
    inv_rms = jnp.broadcast_to(lax.rsqrt(jnp.mean(x * x, axis=-1, keepdims=True) + EPS),
                               (tm, LANES))

    def scaled_dot(w):
        res = _dot(xg, w)
        return jnp.concatenate([res[:, c:c + LANES] * inv_rms
                                for c in range(0, res.shape[1], LANES)], axis=1)

    def proj_cols(start, stop):
        return scaled_dot(wg_ref[:, start:stop])

    def proj(lo, hi):
        return proj_cols(_OFF[lo], _OFF[hi])

    swa = scaled_dot(ws_ref[...])
    sq = (swa[:, :SWA_QW] * (SWA_DH ** -0.5 * LOG2E)).astype(BF16)
    k_cat = jnp.concatenate([kv_prev_ref[0], swa[:, SWA_QW:SWA_QW + SWA_KVW]], axis=0)
    v_cat = jnp.concatenate([kv_prev_ref[1], swa[:, SWA_QW + SWA_KVW:]], axis=0)
    kv_prev_ref[0] = k_cat[tm:, :]
    kv_prev_ref[1] = v_cat[tm:, :]

    def dup_heads(cat):
        low = lax.broadcasted_iota(jnp.int32, cat.shape, 1) < SWA_DH
        rolled = pltpu.roll(cat, SWA_DH, 1)
        return low, (jnp.where(low, cat, rolled), jnp.where(low, rolled, cat))

    _, kdup = dup_heads(k_cat)
    low_kv, vdup = dup_heads(v_cat)
    kd = [kg.astype(BF16) for kg in kdup]
    vd = [[jnp.where(low_kv, vg, 1.0).astype(BF16), jnp.where(low_kv, 1.0, vg).astype(BF16)]
          for vg in vdup]

    low = lax.broadcasted_iota(jnp.int32, (L, pair_w), 1) < SWA_DH
    no_prev = jnp.where((pl.program_id(1) == 0)
                        & (lax.broadcasted_iota(jnp.int32, (L, 2 * L), 1) < L), neg_inf, 0.0)
    items = [(j, p) for j in range(n_blk) for p in range(n_pairs)]
    sinks = [sinks_ref[h] * LOG2E for h in range(SWA_HEADS)]
    row_max = {}

    def biased_scores(t, e):
        j, p = items[t]
        s = s_ref[t, e * L:(e + 1) * L, :] + bias_ref[2 * p + e]
        return s + no_prev if j == 0 else s

    def swa_score_dots(ts):
        for t in ts:
            j, p = items[t]
            qp = sq[j * L:(j + 1) * L, p * pair_w:(p + 1) * pair_w]
            zero = jnp.zeros_like(qp)
            q2 = jnp.concatenate([jnp.where(low, qp, zero), jnp.where(low, zero, qp)], axis=0)
            keys = kd[p // pairs_per_kv][j * L:(j + 2) * L, :]
            s_ref[t] = lax.dot_general(q2, keys, _NT, preferred_element_type=F32)

    def swa_row_max(ts):
        for t in ts:
            for e in range(2):
                s = biased_scores(t, e)
                s_ref[t, e * L:(e + 1) * L, :] = s
                row_max[t, e] = jnp.maximum(jnp.max(s, axis=-1, keepdims=True),
                                            sinks[2 * items[t][1] + e])

    def swa_exp(ts):
        for t in ts:
            for e in range(2):
                s = s_ref[t, e * L:(e + 1) * L, :]
                pr_ref[2 * t + e] = jnp.exp2(s - row_max[t, e]).astype(BF16)

    def swa_out(ts):
        for t in ts:
            j, p = items[t]
            vals = vd[p // pairs_per_kv]
            outs = [_dot(pr_ref[2 * t + e], vals[e][j * L:(j + 2) * L, :]) for e in range(2)]
            sink_terms = [jnp.exp2(sinks[2 * p + e] - row_max[t, e]) for e in range(2)]
            numer = jnp.where(low, outs[0], outs[1])
            denom = (pltpu.roll(jnp.where(low, outs[1], outs[0]), SWA_DH, 1)
                     + jnp.where(low, sink_terms[0], sink_terms[1]))
            o_ref[j * L:(j + 1) * L, p * pair_w:(p + 1) * pair_w] = (
                numer * (1.0 / denom)).astype(o_ref.dtype)

    first, second = range(0, len(items) // 2), range(len(items) // 2, len(items))
    a_low = scaled_dot(wa_ref[...])
    swa_score_dots(range(len(items)))
    a_hi, a_lo = _split_bf16(a_low)
    w_hi, w_lo = _split_bf16(wal_ref[...])
    z = _dot(a_hi, w_hi) + _dot(a_lo, w_hi) + _dot(a_hi, w_lo) + bal_ref[...]
    la_ref[...] = -(jnp.maximum(-z, 0.0) + jnp.log1p(jnp.exp(-jnp.abs(z)))) * (1.0 / GLA_TAU)
    half_v = GLA_VW // 2
    gq_ref[...] = proj(0, 1).astype(BF16)
    swa_row_max(first)
    gk_ref[...] = proj(1, 2).astype(BF16)
    swa_row_max(second)
    gv_ref[:, :half_v] = proj_cols(_OFF[2], _OFF[2] + half_v).astype(BF16)
    swa_exp(first)
    gv_ref[:, half_v:] = proj_cols(_OFF[2] + half_v, _OFF[3]).astype(BF16)
    swa_exp(second)
    gr_ref[:, :half_v] = proj_cols(_OFF[3], _OFF[3] + half_v).astype(BF16)
    swa_out(first)
    swa_out(second)
    gr_ref[:, half_v:] = proj_cols(_OFF[3] + half_v, _OFF[4]).astype(BF16)


def _in_proj_swa(h, g, w_parts, wal_pad, bal, rel_bias, sinks):
    B, T, D = h.shape
    tm = TOK_TILE
    L = WINDOW
    per = tm // L

    def tok(width):
        return pl.BlockSpec((None, tm, width), lambda b, i: (b, i, 0))

    def out(width, dt):
        return jax.ShapeDtypeStruct((B, T, width), dt)

    smem = pl.BlockSpec(memory_space=pltpu.SMEM)
    return pl.pallas_call(
        _inproj_swa_kernel,
        grid=(B, T // tm),
        in_specs=[smem, smem, _const_spec((L, 2 * L)),
                  tok(D), _const_spec((1, D)), _const_spec((D, _OFF[-1])),
                  _const_spec((D, RANK_PAD)), _const_spec((D, D_SWA)),
                  _const_spec((RANK_PAD, GLA_KW)), _const_spec((1, GLA_KW))],
        out_specs=[tok(GLA_KW), tok(GLA_KW), tok(GLA_VW), tok(GLA_VW), tok(GLA_KW), tok(SWA_QW)],
        out_shape=[out(GLA_KW, BF16), out(GLA_KW, BF16), out(GLA_VW, BF16), out(GLA_VW, BF16),
                   out(GLA_KW, F32), out(SWA_QW, BF16)],
        scratch_shapes=[pltpu.VMEM((SWA_HEADS, L, 2 * L), F32),
                        pltpu.VMEM((2, L, SWA_KVW), F32),
                        pltpu.VMEM((per * SWA_HEADS // 2, 2 * L, 2 * L), F32),
                        pltpu.VMEM((per * SWA_HEADS, L, 2 * L), BF16)],
        compiler_params=_params(("arbitrary", "arbitrary")),
        name="in_proj_swa",
    )(sinks, rel_bias, _t5_bucket_table(), h, g, *w_parts, wal_pad, bal)


def _gla_kernel(q_ref, k_ref, v_ref, r_ref, la_ref, gain_ref, o_ref,
                st_ref, q4_ref, kv_ref, dec_ref, oi_ref):
    C = GLA_CHUNK
    H = GLA_HEADS
    n_chunks = q_ref.shape[0] // C

    @pl.when(pl.program_id(1) == 0)
    def _():
        st_ref[...] = jnp.zeros_like(st_ref)

    tril = (lax.broadcasted_iota(jnp.int32, (C, C), 0)
            >= lax.broadcasted_iota(jnp.int32, (C, C), 1)).astype(BF16)
    tril2 = jnp.concatenate([tril, tril], axis=1)
    causal4 = ((lax.broadcasted_iota(jnp.int32, (H * C, C), 0) & (C - 1))
               >= lax.broadcasted_iota(jnp.int32, (H * C, C), 1))
    head_of_lane = lax.broadcasted_iota(jnp.int32, (C, GLA_KW), 1) // GLA_DK
    head_of_st_lane = lax.broadcasted_iota(jnp.int32, (GLA_DV, GLA_KW), 1) // GLA_DK
    gain = gain_ref[...]

    rows_of = [slice(c * C, (c + 1) * C) for c in range(n_chunks)]

    def state_free(chunks):
        bcum = {}
        for c in chunks:
            la_hi, la_lo = _split_bf16(la_ref[rows_of[c], :])
            bcum[c] = _dot(tril2, jnp.concatenate([la_hi, la_lo], axis=0))

        k_inv, k_end = {}, {}
        for c in chunks:
            b_last = bcum[c][C - 1:C, :]
            dec_ref[c] = jnp.exp(b_last)
            q = q_ref[rows_of[c], :].astype(F32) * (GLA_DK ** -0.5)
            k = k_ref[rows_of[c], :].astype(F32)
            q_dec = (q * jnp.exp(bcum[c])).astype(BF16)
            k_inv[c] = (k * jnp.exp(-bcum[c])).astype(BF16)
            k_end[c] = (k * jnp.exp(b_last - bcum[c])).astype(BF16)
            q4_ref[c] = jnp.concatenate(
                [jnp.where(head_of_lane == h, q_dec, jnp.zeros_like(q_dec)) for h in range(H)],
                axis=0)

        att = {c: lax.dot_general(q4_ref[c], k_inv[c], _NT, preferred_element_type=F32)
               for c in chunks}
        kvt_all = {c: lax.dot_general(v_ref[rows_of[c], :], k_end[c], _TN,
                                      preferred_element_type=F32) for c in chunks}

        for c in chunks:
            att_c = jnp.where(causal4, att[c], 0.0).astype(BF16)
            for h in range(H):
                cols = slice(h * GLA_DV, (h + 1) * GLA_DV)
                oi_ref[rows_of[c], cols] = _dot(att_c[h * C:(h + 1) * C, :],
                                                v_ref[rows_of[c], cols])
            kvt = kvt_all[c][(H - 1) * GLA_DV:, :]
            for h in range(H - 2, -1, -1):
                kvt = jnp.where(head_of_st_lane == h, kvt_all[c][h * GLA_DV:(h + 1) * GLA_DV, :],
                                kvt)
            kv_ref[c] = kvt

    def recurrence(chunks, st):
        inter = {}
        for c in chunks:
            inter[c] = lax.dot_general(q4_ref[c], st.astype(BF16), _NT,
                                       preferred_element_type=F32)
            st = st * dec_ref[c] + kv_ref[c]
        return inter, st

    def finish(chunks, inter):
        for c in chunks:
            for h in range(H):
                cols = slice(h * GLA_DV, (h + 1) * GLA_DV)
                o = oi_ref[rows_of[c], cols] + inter[c][h * C:(h + 1) * C, :]
                o = o * lax.rsqrt(jnp.mean(o * o, axis=-1, keepdims=True) + EPS) * gain
                o_ref[rows_of[c], cols] = (
                    o * _silu(r_ref[rows_of[c], cols].astype(F32))).astype(o_ref.dtype)

    chunks = range(n_chunks)
    state_free(chunks)
    inter, st = recurrence(chunks, st_ref[...])
    st_ref[...] = st
    finish(chunks, inter)


def _gla(gq, gk, gv, gr, la, gain):
    B, T, _ = gq.shape
    tm = GLA_TILE

    def tok(width):
        return pl.BlockSpec((None, tm, width), lambda b, i: (b, i, 0))

    return pl.pallas_call(
        _gla_kernel,
        grid=(B, T // tm),
        in_specs=[tok(GLA_KW), tok(GLA_KW), tok(GLA_VW), tok(GLA_VW), tok(GLA_KW),
                  _const_spec((1, GLA_DV))],
        out_specs=tok(GLA_VW),
        out_shape=jax.ShapeDtypeStruct((B, T, GLA_VW), BF16),
        scratch_shapes=[pltpu.VMEM((GLA_DV, GLA_KW), F32),
                        pltpu.VMEM((tm // GLA_CHUNK, GLA_HEADS * GLA_CHUNK, GLA_KW), BF16),
                        pltpu.VMEM((tm // GLA_CHUNK, GLA_DV, GLA_KW), F32),
                        pltpu.VMEM((tm // GLA_CHUNK, 1, GLA_KW), F32),
                        pltpu.VMEM((tm, GLA_VW), F32)],
        compiler_params=_params(("parallel", "arbitrary")),
        name="gla",
    )(gq, gk, gv, gr, la, gain)


def _memkv_kernel(mem_ref, g_ref, wk_ref, wv_ref, k_ref, v_ref):
    mn = _rms(mem_ref[...], g_ref[...]).astype(BF16)
    k_ref[...] = _dot(mn, wk_ref[...]).astype(BF16)
    v_ref[...] = _dot(mn, wv_ref[...]).astype(BF16)


def _mem_kv(mem, g, wk, wv):
    B, M, D = mem.shape
    rows = B * M
    tm = min(TOK_TILE, rows)
    blk = pl.BlockSpec((tm, D), lambda i: (i, 0))
    k, v = pl.pallas_call(
        _memkv_kernel,
        grid=(rows // tm,),
        in_specs=[blk, _const_spec((1, D)), _const_spec((D, D)), _const_spec((D, D))],
        out_specs=[blk, blk],
        out_shape=[jax.ShapeDtypeStruct((rows, D), BF16)] * 2,
        compiler_params=_params(("parallel",)),
        name="mem_kv",
    )(mem.reshape(rows, D), g, wk, wv)
    return k.reshape(B, M, D), v.reshape(B, M, D)


def _outcross_kernel(h_ref, og_ref, os_ref, kc_ref, vc_ref, wout_ref, g_ref,
                     wq_ref, wo_ref, o_ref):
    mix = jnp.concatenate([og_ref[...], os_ref[...]], axis=1)
    h1 = h_ref[...] + _dot(mix, wout_ref[...])
    inv_rms = lax.rsqrt(jnp.mean(h1 * h1, axis=-1, keepdims=True) + EPS)
    hg = (h1 * g_ref[...]).astype(BF16)
    q = (_dot(hg, wq_ref[...]) * (inv_rms * (CROSS_DH ** -0.5 * LOG2E))).astype(BF16)
    cols = [slice(hd * CROSS_DH, (hd + 1) * CROSS_DH) for hd in range(CROSS_HEADS)]
    scores = [lax.dot_general(q[:, c], kc_ref[:, c], _NT, preferred_element_type=F32)
              for c in cols]
    parts = []
    for s, c in zip(scores, cols):
        m = jnp.max(s, axis=-1, keepdims=True)
        p = jnp.exp2(s - m)
        denom = jnp.sum(p, axis=-1, keepdims=True)
        o = (_dot(p.astype(BF16), vc_ref[:, c]) * (1.0 / denom)).astype(BF16)
        parts.append(_dot(o, wo_ref[c, :]))
    attn = parts[0]
    for part in parts[1:]:
        attn = attn + part
    o_ref[...] = h1 + attn


def _out_cross(h, og, osw, kc, vc, wout, g, wq, wo):
    B, T, D = h.shape
    tm = TOK_TILE

    def tok(width):
        return pl.BlockSpec((None, tm, width), lambda b, i: (b, i, 0))

    memblk = pl.BlockSpec((None, MEM_LEN, D), lambda b, i: (b, 0, 0))
    return pl.pallas_call(
        _outcross_kernel,
        grid=(B, T // tm),
        in_specs=[tok(D), tok(GLA_VW), tok(SWA_QW), memblk, memblk,
                  _const_spec((GLA_VW + SWA_QW, D)), _const_spec((1, D)),
                  _const_spec((D, D)), _const_spec((D, D))],
        out_specs=tok(D),
        out_shape=jax.ShapeDtypeStruct((B, T, D), F32),
        compiler_params=_params(("parallel", "parallel")),
        name="out_cross",
    )(h, og, osw, kc, vc, wout, g, wq, wo)


def _ffn_kernel(h_ref, g_ref, wu_ref, wd_ref, cw_ref, cb_ref, gf_ref,
                o_ref, hn_ref, acc_ref, ubuf_ref, carry_ref, *, final_norm):
    tm = h_ref.shape[0]
    n_slab = 2 * FFN_CHUNK // LANES
    half = n_slab // 2

    @pl.when(pl.program_id(1) == 0)
    def _():
        carry_ref[...] = jnp.zeros_like(carry_ref)

    hn_ref[...] = _rms(h_ref[...], g_ref[...]).astype(BF16)

    def cols_of(c, s):
        start = (s // half) * D_FF + c * FFN_CHUNK + (s % half) * LANES
        return slice(start, start + LANES)

    def up(c):
        hn = hn_ref[...]
        for part in range(2):
            start = part * D_FF + c * FFN_CHUNK
            u = _dot(hn, wu_ref[:, start:start + FFN_CHUNK])
            for j in range(half):
                s = part * half + j
                us = u[:, j * LANES:(j + 1) * LANES]
                ubuf_ref[c % FFN_UBUFS, s, 0:CARRY_ROWS, :] = carry_ref[c, s]
                ubuf_ref[c % FFN_UBUFS, s, CARRY_ROWS:, :] = us
                carry_ref[c, s] = us[tm - CARRY_ROWS:, :]

    def conv(c, s, rows):
        lanes = cols_of(c, s)
        taps = [ubuf_ref[c % FFN_UBUFS, s,
                         CARRY_ROWS - d + rows.start:CARRY_ROWS - d + rows.stop, :]
                for d in range(CONV_WIDTH)]
        return (cw_ref[2:3, lanes] * taps[0]
                + (cw_ref[1:2, lanes] * taps[1] + (cw_ref[0:1, lanes] * taps[2] + cb_ref[:, lanes])))

    lookahead = FFN_UBUFS - 1
    for c in range(lookahead):
        up(c)
    last = N_FFN_CHUNKS - 1

    def down(first_chunk, last_chunk, rows, act):
        res = _dot(act, wd_ref[first_chunk * FFN_CHUNK:(last_chunk + 1) * FFN_CHUNK, :])
        base = h_ref[rows, :] if first_chunk == 0 else acc_ref[rows, :]
        if last_chunk < last:
            acc_ref[rows, :] = base + res
        else:
            out = base + res
            if final_norm:
                out = _rms(out, gf_ref[...])
            o_ref[rows, :] = out

    row_blocks = [slice(r, r + tm // FFN_ROW_BLOCKS) for r in range(0, tm, tm // FFN_ROW_BLOCKS)]
    pending = []
    for c in range(N_FFN_CHUNKS):
        if c + lookahead <= last:
            up(c + lookahead)
        pending.append([[(_silu(conv(c, s, rows)) * conv(c, half + s, rows)).astype(BF16)
                         for s in range(half)] for rows in row_blocks])
        if len(pending) == FFN_DOWN_GROUP or c == last:
            for b, rows in enumerate(row_blocks):
                act = jnp.concatenate([a for chunk_acts in pending for a in chunk_acts[b]], axis=1)
                down(c + 1 - len(pending), c, rows, act)
            pending = []


def _ffn(h, g, wu, wd, cw, cb, gf, final_norm):
    B, T, D = h.shape
    tm = FFN_TILE
    nc, fc = N_FFN_CHUNKS, FFN_CHUNK
    n_slab = 2 * fc // LANES
    tok = pl.BlockSpec((None, tm, D), lambda b, i: (b, i, 0))

    def once(shape):
        zeros = (0,) * len(shape)
        return pl.BlockSpec(shape, lambda *_: zeros, pipeline_mode=pl.Buffered(1))

    return pl.pallas_call(
        functools.partial(_ffn_kernel, final_norm=final_norm),
        grid=(B, T // tm),
        in_specs=[tok, _const_spec((1, D)), once((D, 2 * D_FF)), once((D_FF, D)),
                  _const_spec((CONV_WIDTH, 2 * D_FF)), _const_spec((1, 2 * D_FF)),
                  _const_spec((1, D))],
        out_specs=tok,
        out_shape=jax.ShapeDtypeStruct((B, T, D), F32),
        scratch_shapes=[pltpu.VMEM((tm, D), BF16), pltpu.VMEM((tm, D), F32),
                        pltpu.VMEM((FFN_UBUFS, n_slab, CARRY_ROWS + tm, LANES), F32),
                        pltpu.VMEM((nc, n_slab, CARRY_ROWS, LANES), F32)],
        compiler_params=_params(("parallel", "arbitrary")),
        name="ffn",
    )(h, g, wu, wd, cw, cb, gf)


def _split_in_proj(w_in):
    w_rank = jnp.pad(w_in[:, _IN_OFF[4]:_IN_OFF[5]], ((0, 0), (0, RANK_PAD - GLA_RANK)))
    return (w_in[:, :_IN_OFF[4]].astype(BF16), w_rank.astype(BF16),
            w_in[:, _IN_OFF[5]:].astype(BF16))


def kernel(x, mem, norm_mix, w_in, w_alpha, b_alpha, gla_gain, rel_bias, sinks, w_out, norm_cross,
           norm_mem, w_q_c, w_k_c, w_v_c, w_o_c, norm_ffn, w_up, conv_w, conv_b, w_down, norm_final):
    depth = w_in.shape[0]
    row = lambda v: v.reshape(1, -1).astype(F32)
    h = x
    for l in range(depth):
        wal_pad = jnp.pad(w_alpha[l], ((0, RANK_PAD - GLA_RANK), (0, 0)))
        gq, gk, gv, gr, la, o_swa = _in_proj_swa(h, row(norm_mix[l]), _split_in_proj(w_in[l]),
                                                 wal_pad,
                                                 row(b_alpha[l]), rel_bias.astype(F32),
                                                 sinks[l].astype(F32))
        o_gla = _gla(gq, gk, gv, gr, la, row(gla_gain[l]))
        kc, vc = _mem_kv(mem, row(norm_mem[l]), w_k_c[l].astype(BF16), w_v_c[l].astype(BF16))
        h = _out_cross(h, o_gla, o_swa, kc, vc, w_out[l].astype(BF16), row(norm_cross[l]),
                       w_q_c[l].astype(BF16), w_o_c[l].astype(BF16))
        h = _ffn(h, row(norm_ffn[l]), w_up[l].astype(BF16), w_down[l].astype(BF16),
                 conv_w[l].astype(F32), row(conv_b[l]), row(norm_final),
                 final_norm=(l == depth - 1))
    return h
```

```python
import functools
import math

import numpy as np
import jax
import jax.numpy as jnp
from jax import lax
from jax.experimental import pallas as pl
from jax.experimental.pallas import tpu as pltpu

F32 = jnp.float32
BF16 = jnp.bfloat16

D_MODEL = 1024
MEM_LEN = 256
EPS = 1e-6
GLA_HEADS = 4
GLA_DK = 64
GLA_DV = 128
GLA_RANK = 16
GLA_TAU = 16.0
GLA_CHUNK = 64
SWA_HEADS = 8
SWA_KV_HEADS = 2
SWA_DH = 64
WINDOW = 128
REL_BUCKETS = 32
REL_MAX_DIST = 128
CROSS_HEADS = 4
CROSS_DH = D_MODEL // CROSS_HEADS
D_FF = 2816
CONV_WIDTH = 3

GLA_KW = GLA_HEADS * GLA_DK
GLA_VW = GLA_HEADS * GLA_DV
SWA_QW = SWA_HEADS * SWA_DH
SWA_KVW = SWA_KV_HEADS * SWA_DH
IN_SPLITS = (GLA_KW, GLA_KW, GLA_VW, GLA_VW, GLA_RANK, SWA_QW, SWA_KVW, SWA_KVW)

LOG2E = math.log2(math.e)
LANES = 128
RANK_PAD = LANES
_IN_OFF = tuple(int(v) for v in np.cumsum((0,) + IN_SPLITS))
_OFF = _IN_OFF[:5]
D_SWA = SWA_QW + 2 * SWA_KVW

TOK_TILE = 1024
FFN_TILE = 512
GLA_TILE = 2048
SWA_ITEM_GROUPS = 4
GLA_GROUPS = 8
FFN_CHUNK = 256
N_FFN_CHUNKS = D_FF // FFN_CHUNK
FFN_DOWN_GROUP = 2
FFN_ROW_BLOCKS = 2
FFN_UBUFS = 4
CARRY_ROWS = 8
VMEM_LIMIT = 56 * 1024 * 1024

_NT = (((1,), (1,)), ((), ()))
_TN = (((0,), (0,)), ((), ()))


def _rms(x, g):
    return x * lax.rsqrt(jnp.mean(x * x, axis=-1, keepdims=True) + EPS) * g


def _dot(a, b):
    return jnp.dot(a, b, preferred_element_type=F32)


def _split_bf16(x):
    hi = x.astype(BF16)
    return hi, (x - hi.astype(F32)).astype(BF16)


def _silu(x):
    return x * (1.0 / (1.0 + jnp.exp2(x * -LOG2E)))


def _const_spec(shape):
    zeros = (0,) * len(shape)
    return pl.BlockSpec(shape, lambda *_: zeros)


def _params(sem):
    return pltpu.CompilerParams(dimension_semantics=sem, vmem_limit_bytes=VMEM_LIMIT)


def _t5_bucket_table():
    L = WINDOW
    dist = (jnp.arange(L)[:, None] + L) - jnp.arange(2 * L)[None, :]
    n = jnp.maximum(dist, 0)
    max_exact = REL_BUCKETS // 2
    nf = jnp.maximum(n, 1).astype(F32)
    large = max_exact + (jnp.log(nf / max_exact) / math.log(REL_MAX_DIST / max_exact)
                         * (REL_BUCKETS - max_exact)).astype(jnp.int32)
    large = jnp.minimum(large, REL_BUCKETS - 1)
    bucket = jnp.where(n < max_exact, n, large)
    return jnp.where((dist >= 0) & (dist < WINDOW), bucket, -1).astype(jnp.int32)


def _inproj_swa_kernel(sinks_ref, relb_ref, bucket_ref, x_ref, g_ref, wg_ref, wa_ref, ws_ref,
                       wal_ref, bal_ref,
                       gq_ref, gk_ref, gv_ref, gr_ref, la_ref, o_ref,
                       bias_ref, kv_prev_ref, s_ref, pr_ref):
    L = WINDOW
    tm = x_ref.shape[0]
    n_blk = tm // L
    pair_w = 2 * SWA_DH
    n_pairs = SWA_QW // pair_w
    pairs_per_kv = n_pairs // SWA_KV_HEADS
    neg_inf = float("-inf")

    @pl.when((pl.program_id(0) == 0) & (pl.program_id(1) == 0))
    def _():
        bucket = bucket_ref[...]
        for h in range(SWA_HEADS):
            def body(b, acc):
                return jnp.where(bucket == b, relb_ref[b, h] * LOG2E, acc)
            bias_ref[h] = lax.fori_loop(0, REL_BUCKETS, body, jnp.full((L, 2 * L), neg_inf, F32))

    @pl.when(pl.program_id(1) == 0)
    def _():
        kv_prev_ref[...] = jnp.zeros_like(kv_prev_ref)

    x = x_ref[...]
    xg = (x * g_ref[...]).astype(BF16)
    inv_rms = jnp.broadcast_to(lax.rsqrt(jnp.mean(x * x, axis=-1, keepdims=True) + EPS),
                               (tm, LANES))

    def scaled_dot(w):
        res = _dot(xg, w)
        return jnp.concatenate([res[:, c:c + LANES] * inv_rms
                                for c in range(0, res.shape[1], LANES)], axis=1)

    def proj_cols(start, stop):
        return scaled_dot(wg_ref[:, start:stop])

    def proj(lo, hi):
        return proj_cols(_OFF[lo], _OFF[hi])

    swa = scaled_dot(ws_ref[...])
    sq = (swa[:, :SWA_QW] * (SWA_DH ** -0.5 * LOG2E)).astype(BF16)
    k_cat = jnp.concatenate([kv_prev_ref[0], swa[:, SWA_QW:SWA_QW + SWA_KVW]], axis=0)
    v_cat = jnp.concatenate([kv_prev_ref[1], swa[:, SWA_QW + SWA_KVW:]], axis=0)
    kv_prev_ref[0] = k_cat[tm:, :]
    kv_prev_ref[1] = v_cat[tm:, :]

    def dup_heads(cat):
        low = lax.broadcasted_iota(jnp.int32, cat.shape, 1) < SWA_DH
        rolled = pltpu.roll(cat, SWA_DH, 1)
        return low, (jnp.where(low, cat, rolled), jnp.where(low, rolled, cat))

    _, kdup = dup_heads(k_cat)
    low_kv, vdup = dup_heads(v_cat)
    kd = [kg.astype(BF16) for kg in kdup]
    vd = [[jnp.where(low_kv, vg, 1.0).astype(BF16), jnp.where(low_kv, 1.0, vg).astype(BF16)]
          for vg in vdup]

    low = lax.broadcasted_iota(jnp.int32, (L, pair_w), 1) < SWA_DH
    no_prev = jnp.where((pl.program_id(1) == 0)
                        & (lax.broadcasted_iota(jnp.int32, (L, 2 * L), 1) < L), neg_inf, 0.0)
    items = [(j, p) for j in range(n_blk) for p in range(n_pairs)]
    sinks = [sinks_ref[h] * LOG2E for h in range(SWA_HEADS)]
    row_max = {}

    def biased_scores(t, e):
        j, p = items[t]
        s = s_ref[t, e * L:(e + 1) * L, :] + bias_ref[2 * p + e]
        return s + no_prev if j == 0 else s

    def swa_score_dots(ts):
        for t in ts:
            j, p = items[t]
            qp = sq[j * L:(j + 1) * L, p * pair_w:(p + 1) * pair_w]
            zero = jnp.zeros_like(qp)
            q2 = jnp.concatenate([jnp.where(low, qp, zero), jnp.where(low, zero, qp)], axis=0)
            keys = kd[p // pairs_per_kv][j * L:(j + 2) * L, :]
            s_ref[t] = lax.dot_general(q2, keys, _NT, preferred_element_type=F32)

    def swa_row_max(ts):
        for t in ts:
            for e in range(2):
                s = biased_scores(t, e)
                s_ref[t, e * L:(e + 1) * L, :] = s
                row_max[t, e] = jnp.maximum(jnp.max(s, axis=-1, keepdims=True),
                                            sinks[2 * items[t][1] + e])

    def swa_exp(ts):
        for t in ts:
            for e in range(2):
                s = s_ref[t, e * L:(e + 1) * L, :]
                pr_ref[2 * t + e] = jnp.exp2(s - row_max[t, e]).astype(BF16)

    def swa_out(ts):
        for t in ts:
            j, p = items[t]
            vals = vd[p // pairs_per_kv]
            outs = [_dot(pr_ref[2 * t + e], vals[e][j * L:(j + 2) * L, :]) for e in range(2)]
            sink_terms = [jnp.exp2(sinks[2 * p + e] - row_max[t, e]) for e in range(2)]
            numer = jnp.where(low, outs[0], outs[1])
            denom = (pltpu.roll(jnp.where(low, outs[1], outs[0]), SWA_DH, 1)
                     + jnp.where(low, sink_terms[0], sink_terms[1]))
            o_ref[j * L:(j + 1) * L, p * pair_w:(p + 1) * pair_w] = (
                numer * (1.0 / denom)).astype(o_ref.dtype)

    a_low = scaled_dot(wa_ref[...])
    a_hi, a_lo = _split_bf16(a_low)
    w_hi, w_lo = _split_bf16(wal_ref[...])
    z = _dot(a_hi, w_hi) + _dot(a_lo, w_hi) + _dot(a_hi, w_lo) + bal_ref[...]
    la_ref[...] = -(jnp.maximum(-z, 0.0) + jnp.log1p(jnp.exp(-jnp.abs(z)))) * (1.0 / GLA_TAU)

    half_v = GLA_VW // 2

    def gla_piece(i):
        if i == 0:
            gq_ref[...] = proj(0, 1).astype(BF16)
        elif i == 1:
            gk_ref[...] = proj(1, 2).astype(BF16)
        else:
            ref = gv_ref if i < 4 else gr_ref
            base = _OFF[2] if i < 4 else _OFF[3]
            lo = (i % 2) * half_v
            ref[:, lo:lo + half_v] = proj_cols(base + lo, base + lo + half_v).astype(BF16)

    n_groups = SWA_ITEM_GROUPS
    per_group = len(items) // n_groups
    groups = [range(g * per_group, (g + 1) * per_group) for g in range(n_groups)]
    phases = (swa_score_dots, swa_row_max, swa_exp, swa_out)
    n_pieces = 6
    for stage in range(n_groups + len(phases) - 1):
        if stage < n_pieces:
            gla_piece(stage)
        for depth, phase in enumerate(phases):
            if 0 <= stage - depth < n_groups:
                phase(groups[stage - depth])
    for piece in range(n_groups + len(phases) - 1, n_pieces):
        gla_piece(piece)


def _in_proj_swa(h, g, w_parts, wal_pad, bal, rel_bias, sinks):
    B, T, D = h.shape
    tm = TOK_TILE
    L = WINDOW
    per = tm // L

    def tok(width):
        return pl.BlockSpec((None, tm, width), lambda b, i: (b, i, 0))

    def out(width, dt):
        return jax.ShapeDtypeStruct((B, T, width), dt)

    smem = pl.BlockSpec(memory_space=pltpu.SMEM)
    return pl.pallas_call(
        _inproj_swa_kernel,
        grid=(B, T // tm),
        in_specs=[smem, smem, _const_spec((L, 2 * L)),
                  tok(D), _const_spec((1, D)), _const_spec((D, _OFF[-1])),
                  _const_spec((D, RANK_PAD)), _const_spec((D, D_SWA)),
                  _const_spec((RANK_PAD, GLA_KW)), _const_spec((1, GLA_KW))],
        out_specs=[tok(GLA_KW), tok(GLA_KW), tok(GLA_VW), tok(GLA_VW), tok(GLA_KW), tok(SWA_QW)],
        out_shape=[out(GLA_KW, BF16), out(GLA_KW, BF16), out(GLA_VW, BF16), out(GLA_VW, BF16),
                   out(GLA_KW, F32), out(SWA_QW, BF16)],
        scratch_shapes=[pltpu.VMEM((SWA_HEADS, L, 2 * L), F32),
                        pltpu.VMEM((2, L, SWA_KVW), F32),
                        pltpu.VMEM((per * SWA_HEADS // 2, 2 * L, 2 * L), F32),
                        pltpu.VMEM((per * SWA_HEADS, L, 2 * L), BF16)],
        compiler_params=_params(("arbitrary", "arbitrary")),
        name="in_proj_swa",
    )(sinks, rel_bias, _t5_bucket_table(), h, g, *w_parts, wal_pad, bal)


def _gla_kernel(q_ref, k_ref, v_ref, r_ref, la_ref, gain_ref, o_ref,
                st_ref, q4_ref, kv_ref, dec_ref, oi_ref):
    C = GLA_CHUNK
    H = GLA_HEADS
    n_chunks = q_ref.shape[0] // C

    @pl.when(pl.program_id(1) == 0)
    def _():
        st_ref[...] = jnp.zeros_like(st_ref)

    tril = (lax.broadcasted_iota(jnp.int32, (C, C), 0)
            >= lax.broadcasted_iota(jnp.int32, (C, C), 1)).astype(BF16)
    tril2 = jnp.concatenate([tril, tril], axis=1)
    causal4 = ((lax.broadcasted_iota(jnp.int32, (H * C, C), 0) & (C - 1))
               >= lax.broadcasted_iota(jnp.int32, (H * C, C), 1))
    head_of_lane = lax.broadcasted_iota(jnp.int32, (C, GLA_KW), 1) // GLA_DK
    head_of_st_lane = lax.broadcasted_iota(jnp.int32, (GLA_DV, GLA_KW), 1) // GLA_DK
    gain = gain_ref[...]

    rows_of = [slice(c * C, (c + 1) * C) for c in range(n_chunks)]

    k_inv, k_end = {}, {}

    def decays(chunks):
        bcum = {}
        for c in chunks:
            la_hi, la_lo = _split_bf16(la_ref[rows_of[c], :])
            bcum[c] = _dot(tril2, jnp.concatenate([la_hi, la_lo], axis=0))

        for c in chunks:
            b_last = bcum[c][C - 1:C, :]
            dec_ref[c] = jnp.exp(b_last)
            q = q_ref[rows_of[c], :].astype(F32) * (GLA_DK ** -0.5)
            k = k_ref[rows_of[c], :].astype(F32)
            q_dec = (q * jnp.exp(bcum[c])).astype(BF16)
            k_inv[c] = (k * jnp.exp(-bcum[c])).astype(BF16)
            k_end[c] = (k * jnp.exp(b_last - bcum[c])).astype(BF16)
            q4_ref[c] = jnp.concatenate(
                [jnp.where(head_of_lane == h, q_dec, jnp.zeros_like(q_dec)) for h in range(H)],
                axis=0)

    def chunk_dots(chunks):
        att = {c: lax.dot_general(q4_ref[c], k_inv[c], _NT, preferred_element_type=F32)
               for c in chunks}
        kvt_all = {c: lax.dot_general(v_ref[rows_of[c], :], k_end[c], _TN,
                                      preferred_element_type=F32) for c in chunks}

        for c in chunks:
            att_c = jnp.where(causal4, att[c], 0.0).astype(BF16)
            for h in range(H):
                cols = slice(h * GLA_DV, (h + 1) * GLA_DV)
                oi_ref[rows_of[c], cols] = _dot(att_c[h * C:(h + 1) * C, :],
                                                v_ref[rows_of[c], cols])
            kvt = kvt_all[c][(H - 1) * GLA_DV:, :]
            for h in range(H - 2, -1, -1):
                kvt = jnp.where(head_of_st_lane == h, kvt_all[c][h * GLA_DV:(h + 1) * GLA_DV, :],
                                kvt)
            kv_ref[c] = kvt

    def recurrence(chunks, st):
        inter = {}
        for c in chunks:
            inter[c] = lax.dot_general(q4_ref[c], st.astype(BF16), _NT,
                                       preferred_element_type=F32)
            st = st * dec_ref[c] + kv_ref[c]
        return inter, st

    def finish(chunks, inter):
        for c in chunks:
            for h in range(H):
                cols = slice(h * GLA_DV, (h + 1) * GLA_DV)
                o = oi_ref[rows_of[c], cols] + inter[c][h * C:(h + 1) * C, :]
                o = o * lax.rsqrt(jnp.mean(o * o, axis=-1, keepdims=True) + EPS) * gain
                o_ref[rows_of[c], cols] = (
                    o * _silu(r_ref[rows_of[c], cols].astype(F32))).astype(o_ref.dtype)

    per_group = max(n_chunks // GLA_GROUPS, 1)
    groups = [range(g, min(g + per_group, n_chunks)) for g in range(0, n_chunks, per_group)]
    st = st_ref[...]
    inter_prev = None
    decays(groups[0])
    for gi, group in enumerate(groups):
        chunk_dots(group)
        if inter_prev is not None:
            finish(groups[gi - 1], inter_prev)
        if gi + 1 < len(groups):
            decays(groups[gi + 1])
        inter_prev, st = recurrence(group, st)
    st_ref[...] = st
    finish(groups[-1], inter_prev)


def _gla(gq, gk, gv, gr, la, gain):
    B, T, _ = gq.shape
    tm = GLA_TILE

    def tok(width):
        return pl.BlockSpec((None, tm, width), lambda b, i: (b, i, 0))

    return pl.pallas_call(
        _gla_kernel,
        grid=(B, T // tm),
        in_specs=[tok(GLA_KW), tok(GLA_KW), tok(GLA_VW), tok(GLA_VW), tok(GLA_KW),
                  _const_spec((1, GLA_DV))],
        out_specs=tok(GLA_VW),
        out_shape=jax.ShapeDtypeStruct((B, T, GLA_VW), BF16),
        scratch_shapes=[pltpu.VMEM((GLA_DV, GLA_KW), F32),
                        pltpu.VMEM((tm // GLA_CHUNK, GLA_HEADS * GLA_CHUNK, GLA_KW), BF16),
                        pltpu.VMEM((tm // GLA_CHUNK, GLA_DV, GLA_KW), F32),
                        pltpu.VMEM((tm // GLA_CHUNK, 1, GLA_KW), F32),
                        pltpu.VMEM((tm, GLA_VW), F32)],
        compiler_params=_params(("parallel", "arbitrary")),
        name="gla",
    )(gq, gk, gv, gr, la, gain)


def _memkv_kernel(mem_ref, g_ref, wk_ref, wv_ref, k_ref, v_ref):
    mn = _rms(mem_ref[...], g_ref[...]).astype(BF16)
    k_ref[...] = _dot(mn, wk_ref[...]).astype(BF16)
    v_ref[...] = _dot(mn, wv_ref[...]).astype(BF16)


def _mem_kv(mem, g, wk, wv):
    B, M, D = mem.shape
    rows = B * M
    tm = min(TOK_TILE, rows)
    blk = pl.BlockSpec((tm, D), lambda i: (i, 0))
    k, v = pl.pallas_call(
        _memkv_kernel,
        grid=(rows // tm,),
        in_specs=[blk, _const_spec((1, D)), _const_spec((D, D)), _const_spec((D, D))],
        out_specs=[blk, blk],
        out_shape=[jax.ShapeDtypeStruct((rows, D), BF16)] * 2,
        compiler_params=_params(("parallel",)),
        name="mem_kv",
    )(mem.reshape(rows, D), g, wk, wv)
    return k.reshape(B, M, D), v.reshape(B, M, D)


def _outcross_kernel(h_ref, og_ref, os_ref, kc_ref, vc_ref, wout_ref, g_ref,
                     wq_ref, wo_ref, o_ref):
    mix = jnp.concatenate([og_ref[...], os_ref[...]], axis=1)
    h1 = h_ref[...] + _dot(mix, wout_ref[...])
    inv_rms = lax.rsqrt(jnp.mean(h1 * h1, axis=-1, keepdims=True) + EPS)
    hg = (h1 * g_ref[...]).astype(BF16)
    q = (_dot(hg, wq_ref[...]) * (inv_rms * (CROSS_DH ** -0.5 * LOG2E))).astype(BF16)
    cols = [slice(hd * CROSS_DH, (hd + 1) * CROSS_DH) for hd in range(CROSS_HEADS)]
    scores = [lax.dot_general(q[:, c], kc_ref[:, c], _NT, preferred_element_type=F32)
              for c in cols]
    parts = []
    for s, c in zip(scores, cols):
        m = jnp.max(s, axis=-1, keepdims=True)
        p = jnp.exp2(s - m)
        denom = jnp.sum(p, axis=-1, keepdims=True)
        o = (_dot(p.astype(BF16), vc_ref[:, c]) * (1.0 / denom)).astype(BF16)
        parts.append(_dot(o, wo_ref[c, :]))
    attn = parts[0]
    for part in parts[1:]:
        attn = attn + part
    o_ref[...] = h1 + attn


def _out_cross(h, og, osw, kc, vc, wout, g, wq, wo):
    B, T, D = h.shape
    tm = TOK_TILE

    def tok(width):
        return pl.BlockSpec((None, tm, width), lambda b, i: (b, i, 0))

    memblk = pl.BlockSpec((None, MEM_LEN, D), lambda b, i: (b, 0, 0))
    return pl.pallas_call(
        _outcross_kernel,
        grid=(B, T // tm),
        in_specs=[tok(D), tok(GLA_VW), tok(SWA_QW), memblk, memblk,
                  _const_spec((GLA_VW + SWA_QW, D)), _const_spec((1, D)),
                  _const_spec((D, D)), _const_spec((D, D))],
        out_specs=tok(D),
        out_shape=jax.ShapeDtypeStruct((B, T, D), F32),
        compiler_params=_params(("parallel", "parallel")),
        name="out_cross",
    )(h, og, osw, kc, vc, wout, g, wq, wo)


def _ffn_kernel(h_ref, g_ref, wu_ref, wd_ref, cw_ref, cb_ref, gf_ref,
                o_ref, hn_ref, acc_ref, ubuf_ref, carry_ref, *, final_norm):
    tm = h_ref.shape[0]
    n_slab = 2 * FFN_CHUNK // LANES
    half = n_slab // 2

    @pl.when(pl.program_id(1) == 0)
    def _():
        carry_ref[...] = jnp.zeros_like(carry_ref)

    hn_ref[...] = _rms(h_ref[...], g_ref[...]).astype(BF16)

    def cols_of(c, s):
        start = (s // half) * D_FF + c * FFN_CHUNK + (s % half) * LANES
        return slice(start, start + LANES)

    def up(c):
        hn = hn_ref[...]
        for part in range(2):
            start = part * D_FF + c * FFN_CHUNK
            u = _dot(hn, wu_ref[:, start:start + FFN_CHUNK])
            for j in range(half):
                s = part * half + j
                us = u[:, j * LANES:(j + 1) * LANES]
                ubuf_ref[c % FFN_UBUFS, s, 0:CARRY_ROWS, :] = carry_ref[c, s]
                ubuf_ref[c % FFN_UBUFS, s, CARRY_ROWS:, :] = us
                carry_ref[c, s] = us[tm - CARRY_ROWS:, :]

    def conv(c, s, rows):
        lanes = cols_of(c, s)
        taps = [ubuf_ref[c % FFN_UBUFS, s,
                         CARRY_ROWS - d + rows.start:CARRY_ROWS - d + rows.stop, :]
                for d in range(CONV_WIDTH)]
        return (cw_ref[2:3, lanes] * taps[0]
                + (cw_ref[1:2, lanes] * taps[1] + (cw_ref[0:1, lanes] * taps[2] + cb_ref[:, lanes])))

    lookahead = FFN_UBUFS - 1
    for c in range(lookahead):
        up(c)
    last = N_FFN_CHUNKS - 1

    def down(first_chunk, last_chunk, rows, act):
        res = _dot(act, wd_ref[first_chunk * FFN_CHUNK:(last_chunk + 1) * FFN_CHUNK, :])
        base = h_ref[rows, :] if first_chunk == 0 else acc_ref[rows, :]
        if last_chunk < last:
            acc_ref[rows, :] = base + res
        else:
            out = base + res
            if final_norm:
                out = _rms(out, gf_ref[...])
            o_ref[rows, :] = out

    row_blocks = [slice(r, r + tm // FFN_ROW_BLOCKS) for r in range(0, tm, tm // FFN_ROW_BLOCKS)]
    pending = []
    for c in range(N_FFN_CHUNKS):
        if c + lookahead <= last:
            up(c + lookahead)
        pending.append([[(_silu(conv(c, s, rows)) * conv(c, half + s, rows)).astype(BF16)
                         for s in range(half)] for rows in row_blocks])
        if len(pending) == FFN_DOWN_GROUP or c == last:
            for b, rows in enumerate(row_blocks):
                act = jnp.concatenate([a for chunk_acts in pending for a in chunk_acts[b]], axis=1)
                down(c + 1 - len(pending), c, rows, act)
            pending = []


def _ffn(h, g, wu, wd, cw, cb, gf, final_norm):
    B, T, D = h.shape
    tm = FFN_TILE
    nc, fc = N_FFN_CHUNKS, FFN_CHUNK
    n_slab = 2 * fc // LANES
    tok = pl.BlockSpec((None, tm, D), lambda b, i: (b, i, 0))

    def once(shape):
        zeros = (0,) * len(shape)
        return pl.BlockSpec(shape, lambda *_: zeros, pipeline_mode=pl.Buffered(1))

    return pl.pallas_call(
        functools.partial(_ffn_kernel, final_norm=final_norm),
        grid=(B, T // tm),
        in_specs=[tok, _const_spec((1, D)), once((D, 2 * D_FF)), once((D_FF, D)),
                  _const_spec((CONV_WIDTH, 2 * D_FF)), _const_spec((1, 2 * D_FF)),
                  _const_spec((1, D))],
        out_specs=tok,
        out_shape=jax.ShapeDtypeStruct((B, T, D), F32),
        scratch_shapes=[pltpu.VMEM((tm, D), BF16), pltpu.VMEM((tm, D), F32),
                        pltpu.VMEM((FFN_UBUFS, n_slab, CARRY_ROWS + tm, LANES), F32),
                        pltpu.VMEM((nc, n_slab, CARRY_ROWS, LANES), F32)],
        compiler_params=_params(("parallel", "arbitrary")),
        name="ffn",
    )(h, g, wu, wd, cw, cb, gf)


def _split_in_proj(w_in):
    w_rank = jnp.pad(w_in[:, _IN_OFF[4]:_IN_OFF[5]], ((0, 0), (0, RANK_PAD - GLA_RANK)))
    return (w_in[:, :_IN_OFF[4]].astype(BF16), w_rank.astype(BF16),
            w_in[:, _IN_OFF[5]:].astype(BF16))


def kernel(x, mem, norm_mix, w_in, w_alpha, b_alpha, gla_gain, rel_bias, sinks, w_out, norm_cross,
           norm_mem, w_q_c, w_k_c, w_v_c, w_o_c, norm_ffn, w_up, conv_w, conv_b, w_down, norm_final):
    depth = w_in.shape[0]
    row = lambda v: v.reshape(1, -1).astype(F32)
    h = x
    for l in range(depth):
        wal_pad = jnp.pad(w_alpha[l], ((0, RANK_PAD - GLA_RANK), (0, 0)))
        gq, gk, gv, gr, la, o_swa = _in_proj_swa(h, row(norm_mix[l]), _split_in_proj(w_in[l]),
                                                 wal_pad,
                                                 row(b_alpha[l]), rel_bias.astype(F32),
                                                 sinks[l].astype(F32))
        o_gla = _gla(gq, gk, gv, gr, la, row(gla_gain[l]))
        kc, vc = _mem_kv(mem, row(norm_mem[l]), w_k_c[l].astype(BF16), w_v_c[l].astype(BF16))
        h = _out_cross(h, o_gla, o_swa, kc, vc, w_out[l].astype(BF16), row(norm_cross[l]),
                       w_q_c[l].astype(BF16), w_o_c[l].astype(BF16))
        h = _ffn(h, row(norm_ffn[l]), w_up[l].astype(BF16), w_down[l].astype(BF16),
                 conv_w[l].astype(F32), row(conv_b[l]), row(norm_final),
                 final_norm=(l == depth - 1))
    return h
```

```python
import functools
import math

import numpy as np
import jax
import jax.numpy as jnp
from jax import lax
from jax.experimental import pallas as pl
from jax.experimental.pallas import tpu as pltpu

F32 = jnp.float32
BF16 = jnp.bfloat16

D_MODEL = 1024
MEM_LEN = 256
EPS = 1e-6
GLA_HEADS = 4
GLA_DK = 64
GLA_DV = 128
GLA_RANK = 16
GLA_TAU = 16.0
GLA_CHUNK = 64
SWA_HEADS = 8
SWA_KV_HEADS = 2
SWA_DH = 64
WINDOW = 128
REL_BUCKETS = 32
REL_MAX_DIST = 128
CROSS_HEADS = 4
CROSS_DH = D_MODEL // CROSS_HEADS
D_FF = 2816
CONV_WIDTH = 3

GLA_KW = GLA_HEADS * GLA_DK
GLA_VW = GLA_HEADS * GLA_DV
SWA_QW = SWA_HEADS * SWA_DH
SWA_KVW = SWA_KV_HEADS * SWA_DH
IN_SPLITS = (GLA_KW, GLA_KW, GLA_VW, GLA_VW, GLA_RANK, SWA_QW, SWA_KVW, SWA_KVW)

LOG2E = math.log2(math.e)
LANES = 128
RANK_PAD = LANES
_IN_OFF = tuple(int(v) for v in np.cumsum((0,) + IN_SPLITS))
_OFF = _IN_OFF[:5]
D_SWA = SWA_QW + 2 * SWA_KVW

TOK_TILE = 1024
FFN_TILE = 512
GLA_TILE = 2048
CROSS_ROW_BLOCKS = 2
SWA_ITEM_GROUPS = 4
GLA_GROUP_CHUNKS = 4
FFN_CHUNK = 256
N_FFN_CHUNKS = D_FF // FFN_CHUNK
FFN_DOWN_GROUP = 2
FFN_ROW_BLOCKS = 2
FFN_UBUFS = 4
CARRY_ROWS = 8
VMEM_LIMIT = 56 * 1024 * 1024

_NT = (((1,), (1,)), ((), ()))
_TN = (((0,), (0,)), ((), ()))


def _rms(x, g):
    return x * lax.rsqrt(jnp.mean(x * x, axis=-1, keepdims=True) + EPS) * g


def _dot(a, b):
    return jnp.dot(a, b, preferred_element_type=F32)


def _split_bf16(x):
    hi = x.astype(BF16)
    return hi, (x - hi.astype(F32)).astype(BF16)


def _silu(x):
    return x * (1.0 / (1.0 + jnp.exp2(x * -LOG2E)))


def _const_spec(shape):
    zeros = (0,) * len(shape)
    return pl.BlockSpec(shape, lambda *_: zeros)


def _params(sem):
    return pltpu.CompilerParams(dimension_semantics=sem, vmem_limit_bytes=VMEM_LIMIT)


def _t5_bucket_table():
    L = WINDOW
    dist = (jnp.arange(L)[:, None] + L) - jnp.arange(2 * L)[None, :]
    n = jnp.maximum(dist, 0)
    max_exact = REL_BUCKETS // 2
    nf = jnp.maximum(n, 1).astype(F32)
    large = max_exact + (jnp.log(nf / max_exact) / math.log(REL_MAX_DIST / max_exact)
                         * (REL_BUCKETS - max_exact)).astype(jnp.int32)
    large = jnp.minimum(large, REL_BUCKETS - 1)
    bucket = jnp.where(n < max_exact, n, large)
    return jnp.where((dist >= 0) & (dist < WINDOW), bucket, -1).astype(jnp.int32)


def _inproj_swa_kernel(sinks_ref, relb_ref, bucket_ref, x_ref, g_ref, wg_ref, wa_ref, ws_ref,
                       wal_ref, bal_ref,
                       gq_ref, gk_ref, gv_ref, gr_ref, la_ref, o_ref,
                       bias_ref, kv_prev_ref, s_ref, pr_ref):
    L = WINDOW
    tm = x_ref.shape[0]
    n_blk = tm // L
    pair_w = 2 * SWA_DH
    n_pairs = SWA_QW // pair_w
    pairs_per_kv = n_pairs // SWA_KV_HEADS
    neg_inf = float("-inf")

    @pl.when((pl.program_id(0) == 0) & (pl.program_id(1) == 0))
    def _():
        bucket = bucket_ref[...]
        for h in range(SWA_HEADS):
            def body(b, acc):
                return jnp.where(bucket == b, relb_ref[b, h] * LOG2E, acc)
            bias_ref[h] = lax.fori_loop(0, REL_BUCKETS, body, jnp.full((L, 2 * L), neg_inf, F32))

    @pl.when(pl.program_id(1) == 0)
    def _():
        kv_prev_ref[...] = jnp.zeros_like(kv_prev_ref)

    x = x_ref[...]
    xg = (x * g_ref[...]).astype(BF16)
    inv_rms = jnp.broadcast_to(lax.rsqrt(jnp.mean(x * x, axis=-1, keepdims=True) + EPS),
                               (tm, LANES))

    def scaled_dot(w):
        res = _dot(xg, w)
        return jnp.concatenate([res[:, c:c + LANES] * inv_rms
                                for c in range(0, res.shape[1], LANES)], axis=1)

    def proj_cols(start, stop):
        return scaled_dot(wg_ref[:, start:stop])

    def proj(lo, hi):
        return proj_cols(_OFF[lo], _OFF[hi])

    swa = scaled_dot(ws_ref[...])
    sq = (swa[:, :SWA_QW] * (SWA_DH ** -0.5 * LOG2E)).astype(BF16)
    k_cat = jnp.concatenate([kv_prev_ref[0], swa[:, SWA_QW:SWA_QW + SWA_KVW]], axis=0)
    v_cat = jnp.concatenate([kv_prev_ref[1], swa[:, SWA_QW + SWA_KVW:]], axis=0)
    kv_prev_ref[0] = k_cat[tm:, :]
    kv_prev_ref[1] = v_cat[tm:, :]

    def dup_heads(cat):
        low = lax.broadcasted_iota(jnp.int32, cat.shape, 1) < SWA_DH
        rolled = pltpu.roll(cat, SWA_DH, 1)
        return low, (jnp.where(low, cat, rolled), jnp.where(low, rolled, cat))

    _, kdup = dup_heads(k_cat)
    low_kv, vdup = dup_heads(v_cat)
    kd = [kg.astype(BF16) for kg in kdup]
    vd = [[jnp.where(low_kv, vg, 1.0).astype(BF16), jnp.where(low_kv, 1.0, vg).astype(BF16)]
          for vg in vdup]

    low = lax.broadcasted_iota(jnp.int32, (L, pair_w), 1) < SWA_DH
    no_prev = jnp.where((pl.program_id(1) == 0)
                        & (lax.broadcasted_iota(jnp.int32, (L, 2 * L), 1) < L), neg_inf, 0.0)
    items = [(j, p) for j in range(n_blk) for p in range(n_pairs)]
    sinks = [sinks_ref[h] * LOG2E for h in range(SWA_HEADS)]
    row_max = {}

    def biased_scores(t, e):
        j, p = items[t]
        s = s_ref[t, e * L:(e + 1) * L, :] + bias_ref[2 * p + e]
        return s + no_prev if j == 0 else s

    def swa_score_dots(ts):
        for t in ts:
            j, p = items[t]
            qp = sq[j * L:(j + 1) * L, p * pair_w:(p + 1) * pair_w]
            zero = jnp.zeros_like(qp)
            q2 = jnp.concatenate([jnp.where(low, qp, zero), jnp.where(low, zero, qp)], axis=0)
            keys = kd[p // pairs_per_kv][j * L:(j + 2) * L, :]
            s_ref[t] = lax.dot_general(q2, keys, _NT, preferred_element_type=F32)

    def swa_row_max(ts):
        for t in ts:
            for e in range(2):
                s = biased_scores(t, e)
                s_ref[t, e * L:(e + 1) * L, :] = s
                row_max[t, e] = jnp.maximum(jnp.max(s, axis=-1, keepdims=True),
                                            sinks[2 * items[t][1] + e])

    def swa_exp(ts):
        for t in ts:
            for e in range(2):
                s = s_ref[t, e * L:(e + 1) * L, :]
                pr_ref[2 * t + e] = jnp.exp2(s - row_max[t, e]).astype(BF16)

    def swa_out(ts):
        for t in ts:
            j, p = items[t]
            vals = vd[p // pairs_per_kv]
            outs = [_dot(pr_ref[2 * t + e], vals[e][j * L:(j + 2) * L, :]) for e in range(2)]
            sink_terms = [jnp.exp2(sinks[2 * p + e] - row_max[t, e]) for e in range(2)]
            numer = jnp.where(low, outs[0], outs[1])
            denom = (pltpu.roll(jnp.where(low, outs[1], outs[0]), SWA_DH, 1)
                     + jnp.where(low, sink_terms[0], sink_terms[1]))
            o_ref[j * L:(j + 1) * L, p * pair_w:(p + 1) * pair_w] = (
                numer * (1.0 / denom)).astype(o_ref.dtype)

    a_low = scaled_dot(wa_ref[...])
    a_hi, a_lo = _split_bf16(a_low)
    w_hi, w_lo = _split_bf16(wal_ref[...])
    z = _dot(a_hi, w_hi) + _dot(a_lo, w_hi) + _dot(a_hi, w_lo) + bal_ref[...]
    la_ref[...] = -(jnp.maximum(-z, 0.0) + jnp.log1p(jnp.exp(-jnp.abs(z)))) * (1.0 / GLA_TAU)

    half_v = GLA_VW // 2

    def gla_piece(i):
        if i == 0:
            gq_ref[...] = proj(0, 1).astype(BF16)
        elif i == 1:
            gk_ref[...] = proj(1, 2).astype(BF16)
        else:
            ref = gv_ref if i < 4 else gr_ref
            base = _OFF[2] if i < 4 else _OFF[3]
            lo = (i % 2) * half_v
            ref[:, lo:lo + half_v] = proj_cols(base + lo, base + lo + half_v).astype(BF16)

    n_groups = SWA_ITEM_GROUPS
    per_group = len(items) // n_groups
    groups = [range(g * per_group, (g + 1) * per_group) for g in range(n_groups)]
    phases = (swa_score_dots, swa_row_max, swa_exp, swa_out)
    n_pieces = 6
    for stage in range(n_groups + len(phases) - 1):
        if stage < n_pieces:
            gla_piece(stage)
        for depth, phase in enumerate(phases):
            if 0 <= stage - depth < n_groups:
                phase(groups[stage - depth])
    for piece in range(n_groups + len(phases) - 1, n_pieces):
        gla_piece(piece)


def _in_proj_swa(h, g, w_parts, wal_pad, bal, rel_bias, sinks):
    B, T, D = h.shape
    tm = TOK_TILE
    L = WINDOW
    per = tm // L

    def tok(width):
        return pl.BlockSpec((None, tm, width), lambda b, i: (b, i, 0))

    def out(width, dt):
        return jax.ShapeDtypeStruct((B, T, width), dt)

    smem = pl.BlockSpec(memory_space=pltpu.SMEM)
    return pl.pallas_call(
        _inproj_swa_kernel,
        grid=(B, T // tm),
        in_specs=[smem, smem, _const_spec((L, 2 * L)),
                  tok(D), _const_spec((1, D)), _const_spec((D, _OFF[-1])),
                  _const_spec((D, RANK_PAD)), _const_spec((D, D_SWA)),
                  _const_spec((RANK_PAD, GLA_KW)), _const_spec((1, GLA_KW))],
        out_specs=[tok(GLA_KW), tok(GLA_KW), tok(GLA_VW), tok(GLA_VW), tok(GLA_KW), tok(SWA_QW)],
        out_shape=[out(GLA_KW, BF16), out(GLA_KW, BF16), out(GLA_VW, BF16), out(GLA_VW, BF16),
                   out(GLA_KW, F32), out(SWA_QW, BF16)],
        scratch_shapes=[pltpu.VMEM((SWA_HEADS, L, 2 * L), F32),
                        pltpu.VMEM((2, L, SWA_KVW), F32),
                        pltpu.VMEM((per * SWA_HEADS // 2, 2 * L, 2 * L), F32),
                        pltpu.VMEM((per * SWA_HEADS, L, 2 * L), BF16)],
        compiler_params=_params(("arbitrary", "arbitrary")),
        name="in_proj_swa",
    )(sinks, rel_bias, _t5_bucket_table(), h, g, *w_parts, wal_pad, bal)


def _gla_kernel(q_ref, k_ref, v_ref, r_ref, la_ref, gain_ref, o_ref,
                st_ref, q4_ref, kv_ref, dec_ref, oi_ref):
    C = GLA_CHUNK
    H = GLA_HEADS
    n_chunks = q_ref.shape[0] // C

    @pl.when(pl.program_id(1) == 0)
    def _():
        st_ref[...] = jnp.zeros_like(st_ref)

    tril = (lax.broadcasted_iota(jnp.int32, (C, C), 0)
            >= lax.broadcasted_iota(jnp.int32, (C, C), 1)).astype(BF16)
    tril2 = jnp.concatenate([tril, tril], axis=1)
    causal4 = ((lax.broadcasted_iota(jnp.int32, (H * C, C), 0) & (C - 1))
               >= lax.broadcasted_iota(jnp.int32, (H * C, C), 1))
    head_of_lane = lax.broadcasted_iota(jnp.int32, (C, GLA_KW), 1) // GLA_DK
    head_of_st_lane = lax.broadcasted_iota(jnp.int32, (GLA_DV, GLA_KW), 1) // GLA_DK
    gain = gain_ref[...]

    rows_of = [slice(c * C, (c + 1) * C) for c in range(n_chunks)]

    k_inv, k_end = {}, {}

    def decays(chunks):
        bcum = {}
        for c in chunks:
            la_hi, la_lo = _split_bf16(la_ref[rows_of[c], :])
            bcum[c] = _dot(tril2, jnp.concatenate([la_hi, la_lo], axis=0))

        for c in chunks:
            b_last = bcum[c][C - 1:C, :]
            dec_ref[c] = jnp.exp(b_last)
            q = q_ref[rows_of[c], :].astype(F32) * (GLA_DK ** -0.5)
            k = k_ref[rows_of[c], :].astype(F32)
            q_dec = (q * jnp.exp(bcum[c])).astype(BF16)
            k_inv[c] = (k * jnp.exp(-bcum[c])).astype(BF16)
            k_end[c] = (k * jnp.exp(b_last - bcum[c])).astype(BF16)
            q4_ref[c] = jnp.concatenate(
                [jnp.where(head_of_lane == h, q_dec, jnp.zeros_like(q_dec)) for h in range(H)],
                axis=0)

    def chunk_dots(chunks):
        att = {c: lax.dot_general(q4_ref[c], k_inv[c], _NT, preferred_element_type=F32)
               for c in chunks}
        kvt_all = {c: lax.dot_general(v_ref[rows_of[c], :], k_end[c], _TN,
                                      preferred_element_type=F32) for c in chunks}

        for c in chunks:
            att_c = jnp.where(causal4, att[c], 0.0).astype(BF16)
            for h in range(H):
                cols = slice(h * GLA_DV, (h + 1) * GLA_DV)
                oi_ref[rows_of[c], cols] = _dot(att_c[h * C:(h + 1) * C, :],
                                                v_ref[rows_of[c], cols])
            kvt = kvt_all[c][(H - 1) * GLA_DV:, :]
            for h in range(H - 2, -1, -1):
                kvt = jnp.where(head_of_st_lane == h, kvt_all[c][h * GLA_DV:(h + 1) * GLA_DV, :],
                                kvt)
            kv_ref[c] = kvt

    def recurrence(chunks, st):
        inter = {}
        for c in chunks:
            inter[c] = lax.dot_general(q4_ref[c], st.astype(BF16), _NT,
                                       preferred_element_type=F32)
            st = st * dec_ref[c] + kv_ref[c]
        return inter, st

    def finish(chunks, inter):
        for c in chunks:
            for h in range(H):
                cols = slice(h * GLA_DV, (h + 1) * GLA_DV)
                o = oi_ref[rows_of[c], cols] + inter[c][h * C:(h + 1) * C, :]
                o = o * lax.rsqrt(jnp.mean(o * o, axis=-1, keepdims=True) + EPS) * gain
                o_ref[rows_of[c], cols] = (
                    o * _silu(r_ref[rows_of[c], cols].astype(F32))).astype(o_ref.dtype)

    per_group = GLA_GROUP_CHUNKS
    groups = [range(g, min(g + per_group, n_chunks)) for g in range(0, n_chunks, per_group)]
    st = st_ref[...]
    inter_prev = None
    decays(groups[0])
    for gi, group in enumerate(groups):
        chunk_dots(group)
        if inter_prev is not None:
            finish(groups[gi - 1], inter_prev)
        if gi + 1 < len(groups):
            decays(groups[gi + 1])
        inter_prev, st = recurrence(group, st)
    st_ref[...] = st
    finish(groups[-1], inter_prev)


def _gla(gq, gk, gv, gr, la, gain):
    B, T, _ = gq.shape
    tm = GLA_TILE

    def tok(width):
        return pl.BlockSpec((None, tm, width), lambda b, i: (b, i, 0))

    return pl.pallas_call(
        _gla_kernel,
        grid=(B, T // tm),
        in_specs=[tok(GLA_KW), tok(GLA_KW), tok(GLA_VW), tok(GLA_VW), tok(GLA_KW),
                  _const_spec((1, GLA_DV))],
        out_specs=tok(GLA_VW),
        out_shape=jax.ShapeDtypeStruct((B, T, GLA_VW), BF16),
        scratch_shapes=[pltpu.VMEM((GLA_DV, GLA_KW), F32),
                        pltpu.VMEM((tm // GLA_CHUNK, GLA_HEADS * GLA_CHUNK, GLA_KW), BF16),
                        pltpu.VMEM((tm // GLA_CHUNK, GLA_DV, GLA_KW), F32),
                        pltpu.VMEM((tm // GLA_CHUNK, 1, GLA_KW), F32),
                        pltpu.VMEM((tm, GLA_VW), F32)],
        compiler_params=_params(("parallel", "arbitrary")),
        name="gla",
    )(gq, gk, gv, gr, la, gain)


def _memkv_kernel(mem_ref, g_ref, wk_ref, wv_ref, k_ref, v_ref):
    mn = _rms(mem_ref[...], g_ref[...]).astype(BF16)
    k_ref[...] = _dot(mn, wk_ref[...]).astype(BF16)
    v_ref[...] = _dot(mn, wv_ref[...]).astype(BF16)


def _mem_kv(mem, g, wk, wv):
    B, M, D = mem.shape
    rows = B * M
    tm = min(TOK_TILE, rows)
    blk = pl.BlockSpec((tm, D), lambda i: (i, 0))
    k, v = pl.pallas_call(
        _memkv_kernel,
        grid=(rows // tm,),
        in_specs=[blk, _const_spec((1, D)), _const_spec((D, D)), _const_spec((D, D))],
        out_specs=[blk, blk],
        out_shape=[jax.ShapeDtypeStruct((rows, D), BF16)] * 2,
        compiler_params=_params(("parallel",)),
        name="mem_kv",
    )(mem.reshape(rows, D), g, wk, wv)
    return k.reshape(B, M, D), v.reshape(B, M, D)


def _outcross_kernel(h_ref, og_ref, os_ref, kc_ref, vc_ref, wout_ref, g_ref,
                     wq_ref, wo_ref, o_ref):
    mix = jnp.concatenate([og_ref[...], os_ref[...]], axis=1)
    h1 = h_ref[...] + _dot(mix, wout_ref[...])
    inv_rms = lax.rsqrt(jnp.mean(h1 * h1, axis=-1, keepdims=True) + EPS)
    hg = (h1 * g_ref[...]).astype(BF16)
    q = (_dot(hg, wq_ref[...]) * (inv_rms * (CROSS_DH ** -0.5 * LOG2E))).astype(BF16)
    tm = h1.shape[0]
    blk = tm // CROSS_ROW_BLOCKS
    items = [(slice(rb * blk, (rb + 1) * blk), slice(hd * CROSS_DH, (hd + 1) * CROSS_DH))
             for rb in range(CROSS_ROW_BLOCKS) for hd in range(CROSS_HEADS)]
    scores, probs, attn = {}, {}, {}
    for stage in range(len(items) + 2):
        if stage < len(items):
            rows, c = items[stage]
            scores[stage] = lax.dot_general(q[rows, c], kc_ref[:, c], _NT,
                                            preferred_element_type=F32)
        if 0 <= stage - 1 < len(items):
            s = scores.pop(stage - 1)
            p = jnp.exp2(s - jnp.max(s, axis=-1, keepdims=True))
            probs[stage - 1] = (p.astype(BF16), 1.0 / jnp.sum(p, axis=-1, keepdims=True))
        if 0 <= stage - 2 < len(items):
            i = stage - 2
            rows, c = items[i]
            p, inv_denom = probs.pop(i)
            o = (_dot(p, vc_ref[:, c]) * inv_denom).astype(BF16)
            part = _dot(o, wo_ref[c, :])
            rb = i // CROSS_HEADS
            attn[rb] = part if i % CROSS_HEADS == 0 else attn[rb] + part
            if i % CROSS_HEADS == CROSS_HEADS - 1:
                o_ref[rows, :] = h1[rows, :] + attn.pop(rb)


def _out_cross(h, og, osw, kc, vc, wout, g, wq, wo):
    B, T, D = h.shape
    tm = TOK_TILE

    def tok(width):
        return pl.BlockSpec((None, tm, width), lambda b, i: (b, i, 0))

    memblk = pl.BlockSpec((None, MEM_LEN, D), lambda b, i: (b, 0, 0))
    return pl.pallas_call(
        _outcross_kernel,
        grid=(B, T // tm),
        in_specs=[tok(D), tok(GLA_VW), tok(SWA_QW), memblk, memblk,
                  _const_spec((GLA_VW + SWA_QW, D)), _const_spec((1, D)),
                  _const_spec((D, D)), _const_spec((D, D))],
        out_specs=tok(D),
        out_shape=jax.ShapeDtypeStruct((B, T, D), F32),
        compiler_params=_params(("parallel", "parallel")),
        name="out_cross",
    )(h, og, osw, kc, vc, wout, g, wq, wo)


def _ffn_kernel(h_ref, g_ref, wu_ref, wd_ref, cw_ref, cb_ref, gf_ref,
                o_ref, hn_ref, acc_ref, ubuf_ref, carry_ref, *, final_norm):
    tm = h_ref.shape[0]
    n_slab = 2 * FFN_CHUNK // LANES
    half = n_slab // 2

    @pl.when(pl.program_id(1) == 0)
    def _():
        carry_ref[...] = jnp.zeros_like(carry_ref)

    hn_ref[...] = _rms(h_ref[...], g_ref[...]).astype(BF16)

    def cols_of(c, s):
        start = (s // half) * D_FF + c * FFN_CHUNK + (s % half) * LANES
        return slice(start, start + LANES)

    def up(c):
        hn = hn_ref[...]
        for part in range(2):
            start = part * D_FF + c * FFN_CHUNK
            u = _dot(hn, wu_ref[:, start:start + FFN_CHUNK])
            for j in range(half):
                s = part * half + j
                us = u[:, j * LANES:(j + 1) * LANES]
                ubuf_ref[c % FFN_UBUFS, s, 0:CARRY_ROWS, :] = carry_ref[c, s]
                ubuf_ref[c % FFN_UBUFS, s, CARRY_ROWS:, :] = us
                carry_ref[c, s] = us[tm - CARRY_ROWS:, :]

    def conv(c, s, rows):
        lanes = cols_of(c, s)
        taps = [ubuf_ref[c % FFN_UBUFS, s,
                         CARRY_ROWS - d + rows.start:CARRY_ROWS - d + rows.stop, :]
                for d in range(CONV_WIDTH)]
        return (cw_ref[2:3, lanes] * taps[0]
                + (cw_ref[1:2, lanes] * taps[1] + (cw_ref[0:1, lanes] * taps[2] + cb_ref[:, lanes])))

    lookahead = FFN_UBUFS - 1
    for c in range(lookahead):
        up(c)
    last = N_FFN_CHUNKS - 1

    def down(first_chunk, last_chunk, rows, act):
        res = _dot(act, wd_ref[first_chunk * FFN_CHUNK:(last_chunk + 1) * FFN_CHUNK, :])
        base = h_ref[rows, :] if first_chunk == 0 else acc_ref[rows, :]
        if last_chunk < last:
            acc_ref[rows, :] = base + res
        else:
            out = base + res
            if final_norm:
                out = _rms(out, gf_ref[...])
            o_ref[rows, :] = out

    row_blocks = [slice(r, r + tm // FFN_ROW_BLOCKS) for r in range(0, tm, tm // FFN_ROW_BLOCKS)]
    pending = []
    for c in range(N_FFN_CHUNKS):
        if c + lookahead <= last:
            up(c + lookahead)
        pending.append([[(_silu(conv(c, s, rows)) * conv(c, half + s, rows)).astype(BF16)
                         for s in range(half)] for rows in row_blocks])
        if len(pending) == FFN_DOWN_GROUP or c == last:
            for b, rows in enumerate(row_blocks):
                act = jnp.concatenate([a for chunk_acts in pending for a in chunk_acts[b]], axis=1)
                down(c + 1 - len(pending), c, rows, act)
            pending = []


def _ffn(h, g, wu, wd, cw, cb, gf, final_norm):
    B, T, D = h.shape
    tm = FFN_TILE
    nc, fc = N_FFN_CHUNKS, FFN_CHUNK
    n_slab = 2 * fc // LANES
    tok = pl.BlockSpec((None, tm, D), lambda b, i: (b, i, 0))

    def once(shape):
        zeros = (0,) * len(shape)
        return pl.BlockSpec(shape, lambda *_: zeros, pipeline_mode=pl.Buffered(1))

    return pl.pallas_call(
        functools.partial(_ffn_kernel, final_norm=final_norm),
        grid=(B, T // tm),
        in_specs=[tok, _const_spec((1, D)), once((D, 2 * D_FF)), once((D_FF, D)),
                  _const_spec((CONV_WIDTH, 2 * D_FF)), _const_spec((1, 2 * D_FF)),
                  _const_spec((1, D))],
        out_specs=tok,
        out_shape=jax.ShapeDtypeStruct((B, T, D), F32),
        scratch_shapes=[pltpu.VMEM((tm, D), BF16), pltpu.VMEM((tm, D), F32),
                        pltpu.VMEM((FFN_UBUFS, n_slab, CARRY_ROWS + tm, LANES), F32),
                        pltpu.VMEM((nc, n_slab, CARRY_ROWS, LANES), F32)],
        compiler_params=_params(("parallel", "arbitrary")),
        name="ffn",
    )(h, g, wu, wd, cw, cb, gf)


def _split_in_proj(w_in):
    w_rank = jnp.pad(w_in[:, _IN_OFF[4]:_IN_OFF[5]], ((0, 0), (0, RANK_PAD - GLA_RANK)))
    return (w_in[:, :_IN_OFF[4]].astype(BF16), w_rank.astype(BF16),
            w_in[:, _IN_OFF[5]:].astype(BF16))


def kernel(x, mem, norm_mix, w_in, w_alpha, b_alpha, gla_gain, rel_bias, sinks, w_out, norm_cross,
           norm_mem, w_q_c, w_k_c, w_v_c, w_o_c, norm_ffn, w_up, conv_w, conv_b, w_down, norm_final):
    depth = w_in.shape[0]
    row = lambda v: v.reshape(1, -1).astype(F32)
    h = x
    for l in range(depth):
        wal_pad = jnp.pad(w_alpha[l], ((0, RANK_PAD - GLA_RANK), (0, 0)))
        gq, gk, gv, gr, la, o_swa = _in_proj_swa(h, row(norm_mix[l]), _split_in_proj(w_in[l]),
                                                 wal_pad,
                                                 row(b_alpha[l]), rel_bias.astype(F32),
                                                 sinks[l].astype(F32))
        o_gla = _gla(gq, gk, gv, gr, la, row(gla_gain[l]))
        kc, vc = _mem_kv(mem, row(norm_mem[l]), w_k_c[l].astype(BF16), w_v_c[l].astype(BF16))
        h = _out_cross(h, o_gla, o_swa, kc, vc, w_out[l].astype(BF16), row(norm_cross[l]),
                       w_q_c[l].astype(BF16), w_o_c[l].astype(BF16))
        h = _ffn(h, row(norm_ffn[l]), w_up[l].astype(BF16), w_down[l].astype(BF16),
                 conv_w[l].astype(F32), row(conv_b[l]), row(norm_final),
                 final_norm=(l == depth - 1))
    return h
```

```python
import functools
import math

import numpy as np
import jax
import jax.numpy as jnp
from jax import lax
from jax.experimental import pallas as pl
from jax.experimental.pallas import tpu as pltpu

F32 = jnp.float32
BF16 = jnp.bfloat16

D_MODEL = 1024
MEM_LEN = 256
EPS = 1e-6
GLA_HEADS = 4
GLA_DK = 64
GLA_DV = 128
GLA_RANK = 16
GLA_TAU = 16.0
GLA_CHUNK = 64
SWA_HEADS = 8
SWA_KV_HEADS = 2
SWA_DH = 64
WINDOW = 128
REL_BUCKETS = 32
REL_MAX_DIST = 128
CROSS_HEADS = 4
CROSS_DH = D_MODEL // CROSS_HEADS
D_FF = 2816
CONV_WIDTH = 3

GLA_KW = GLA_HEADS * GLA_DK
GLA_VW = GLA_HEADS * GLA_DV
SWA_QW = SWA_HEADS * SWA_DH
SWA_KVW = SWA_KV_HEADS * SWA_DH
IN_SPLITS = (GLA_KW, GLA_KW, GLA_VW, GLA_VW, GLA_RANK, SWA_QW, SWA_KVW, SWA_KVW)

LOG2E = math.log2(math.e)
LANES = 128
BF16_SUBLANES = 16
RANK_PAD = LANES
_IN_OFF = tuple(int(v) for v in np.cumsum((0,) + IN_SPLITS))
_OFF = _IN_OFF[:5]
D_SWA = SWA_QW + 2 * SWA_KVW

TOK_TILE = 1024
FFN_TILE = 512
GLA_TILE = 2048
CROSS_ROW_BLOCKS = 2
SWA_ITEM_GROUPS = 4
GLA_GROUP_CHUNKS = 4
FFN_CHUNK = 256
N_FFN_CHUNKS = D_FF // FFN_CHUNK
FFN_DOWN_GROUP = 2
FFN_ROW_BLOCKS = 2
FFN_UBUFS = 4
CARRY_ROWS = 8
VMEM_LIMIT = 56 * 1024 * 1024

_NT = (((1,), (1,)), ((), ()))
_TN = (((0,), (0,)), ((), ()))


def _rms(x, g):
    return x * lax.rsqrt(jnp.mean(x * x, axis=-1, keepdims=True) + EPS) * g


def _dot(a, b):
    return jnp.dot(a, b, preferred_element_type=F32)


def _split_bf16(x):
    hi = x.astype(BF16)
    return hi, (x - hi.astype(F32)).astype(BF16)


def _silu(x):
    return x * (1.0 / (1.0 + jnp.exp2(x * -LOG2E)))


def _const_spec(shape):
    zeros = (0,) * len(shape)
    return pl.BlockSpec(shape, lambda *_: zeros)


def _params(sem):
    return pltpu.CompilerParams(dimension_semantics=sem, vmem_limit_bytes=VMEM_LIMIT)


def _cast_specs(weights, grid):
    n_steps = grid[0] * grid[1]
    specs, shapes = [], []
    for w in weights:
        rows = w.shape[0] // n_steps
        assert rows * n_steps == w.shape[0] and rows % BF16_SUBLANES == 0, w.shape
        specs.append(pl.BlockSpec((rows, w.shape[1]), lambda b, i: (b * grid[1] + i, 0)))
        shapes.append(jax.ShapeDtypeStruct(w.shape, BF16))
    return specs, shapes


def _cast_blocks(src_refs, dst_refs):
    for src, dst in zip(src_refs, dst_refs):
        dst[...] = src[...].astype(BF16)


def _t5_bucket_table():
    L = WINDOW
    dist = (jnp.arange(L)[:, None] + L) - jnp.arange(2 * L)[None, :]
    n = jnp.maximum(dist, 0)
    max_exact = REL_BUCKETS // 2
    nf = jnp.maximum(n, 1).astype(F32)
    large = max_exact + (jnp.log(nf / max_exact) / math.log(REL_MAX_DIST / max_exact)
                         * (REL_BUCKETS - max_exact)).astype(jnp.int32)
    large = jnp.minimum(large, REL_BUCKETS - 1)
    bucket = jnp.where(n < max_exact, n, large)
    return jnp.where((dist >= 0) & (dist < WINDOW), bucket, -1).astype(jnp.int32)


def _inproj_swa_kernel(sinks_ref, relb_ref, bucket_ref, x_ref, g_ref, wg_ref, wa_ref, ws_ref,
                       wal_ref, bal_ref, *refs, n_cast):
    cast_src, refs = refs[:n_cast], refs[n_cast:]
    gq_ref, gk_ref, gv_ref, gr_ref, la_ref, o_ref = refs[:6]
    cast_dst, (bias_ref, kv_prev_ref, s_ref, pr_ref) = refs[6:6 + n_cast], refs[6 + n_cast:]
    _cast_blocks(cast_src, cast_dst)

    L = WINDOW
    tm = x_ref.shape[0]
    n_blk = tm // L
    pair_w = 2 * SWA_DH
    n_pairs = SWA_QW // pair_w
    pairs_per_kv = n_pairs // SWA_KV_HEADS
    neg_inf = float("-inf")

    @pl.when((pl.program_id(0) == 0) & (pl.program_id(1) == 0))
    def _():
        bucket = bucket_ref[...]
        for h in range(SWA_HEADS):
            def body(b, acc):
                return jnp.where(bucket == b, relb_ref[b, h] * LOG2E, acc)
            bias_ref[h] = lax.fori_loop(0, REL_BUCKETS, body, jnp.full((L, 2 * L), neg_inf, F32))

    @pl.when(pl.program_id(1) == 0)
    def _():
        kv_prev_ref[...] = jnp.zeros_like(kv_prev_ref)

    x = x_ref[...]
    xg = (x * g_ref[...]).astype(BF16)
    inv_rms = jnp.broadcast_to(lax.rsqrt(jnp.mean(x * x, axis=-1, keepdims=True) + EPS),
                               (tm, LANES))

    def scaled_dot(w):
        res = _dot(xg, w)
        return jnp.concatenate([res[:, c:c + LANES] * inv_rms
                                for c in range(0, res.shape[1], LANES)], axis=1)

    def proj_cols(start, stop):
        return scaled_dot(wg_ref[:, start:stop])

    def proj(lo, hi):
        return proj_cols(_OFF[lo], _OFF[hi])

    swa = scaled_dot(ws_ref[...])
    sq = (swa[:, :SWA_QW] * (SWA_DH ** -0.5 * LOG2E)).astype(BF16)
    k_cat = jnp.concatenate([kv_prev_ref[0], swa[:, SWA_QW:SWA_QW + SWA_KVW]], axis=0)
    v_cat = jnp.concatenate([kv_prev_ref[1], swa[:, SWA_QW + SWA_KVW:]], axis=0)
    kv_prev_ref[0] = k_cat[tm:, :]
    kv_prev_ref[1] = v_cat[tm:, :]

    def dup_heads(cat):
        low = lax.broadcasted_iota(jnp.int32, cat.shape, 1) < SWA_DH
        rolled = pltpu.roll(cat, SWA_DH, 1)
        return low, (jnp.where(low, cat, rolled), jnp.where(low, rolled, cat))

    _, kdup = dup_heads(k_cat)
    low_kv, vdup = dup_heads(v_cat)
    kd = [kg.astype(BF16) for kg in kdup]
    vd = [[jnp.where(low_kv, vg, 1.0).astype(BF16), jnp.where(low_kv, 1.0, vg).astype(BF16)]
          for vg in vdup]

    low = lax.broadcasted_iota(jnp.int32, (L, pair_w), 1) < SWA_DH
    no_prev = jnp.where((pl.program_id(1) == 0)
                        & (lax.broadcasted_iota(jnp.int32, (L, 2 * L), 1) < L), neg_inf, 0.0)
    items = [(j, p) for j in range(n_blk) for p in range(n_pairs)]
    sinks = [sinks_ref[h] * LOG2E for h in range(SWA_HEADS)]
    row_max = {}

    def biased_scores(t, e):
        j, p = items[t]
        s = s_ref[t, e * L:(e + 1) * L, :] + bias_ref[2 * p + e]
        return s + no_prev if j == 0 else s

    def swa_score_dots(ts):
        for t in ts:
            j, p = items[t]
            qp = sq[j * L:(j + 1) * L, p * pair_w:(p + 1) * pair_w]
            zero = jnp.zeros_like(qp)
            q2 = jnp.concatenate([jnp.where(low, qp, zero), jnp.where(low, zero, qp)], axis=0)
            keys = kd[p // pairs_per_kv][j * L:(j + 2) * L, :]
            s_ref[t] = lax.dot_general(q2, keys, _NT, preferred_element_type=F32)

    def swa_row_max(ts):
        for t in ts:
            for e in range(2):
                s = biased_scores(t, e)
                s_ref[t, e * L:(e + 1) * L, :] = s
                row_max[t, e] = jnp.maximum(jnp.max(s, axis=-1, keepdims=True),
                                            sinks[2 * items[t][1] + e])

    def swa_exp(ts):
        for t in ts:
            for e in range(2):
                s = s_ref[t, e * L:(e + 1) * L, :]
                pr_ref[2 * t + e] = jnp.exp2(s - row_max[t, e]).astype(BF16)

    def swa_out(ts):
        for t in ts:
            j, p = items[t]
            vals = vd[p // pairs_per_kv]
            outs = [_dot(pr_ref[2 * t + e], vals[e][j * L:(j + 2) * L, :]) for e in range(2)]
            sink_terms = [jnp.exp2(sinks[2 * p + e] - row_max[t, e]) for e in range(2)]
            numer = jnp.where(low, outs[0], outs[1])
            denom = (pltpu.roll(jnp.where(low, outs[1], outs[0]), SWA_DH, 1)
                     + jnp.where(low, sink_terms[0], sink_terms[1]))
            o_ref[j * L:(j + 1) * L, p * pair_w:(p + 1) * pair_w] = (
                numer * (1.0 / denom)).astype(o_ref.dtype)

    a_low = scaled_dot(wa_ref[...])
    a_hi, a_lo = _split_bf16(a_low)
    w_hi, w_lo = _split_bf16(wal_ref[...])
    z = _dot(a_hi, w_hi) + _dot(a_lo, w_hi) + _dot(a_hi, w_lo) + bal_ref[...]
    la_ref[...] = -(jnp.maximum(-z, 0.0) + jnp.log1p(jnp.exp(-jnp.abs(z)))) * (1.0 / GLA_TAU)

    half_v = GLA_VW // 2

    def gla_piece(i):
        if i == 0:
            gq_ref[...] = proj(0, 1).astype(BF16)
        elif i == 1:
            gk_ref[...] = proj(1, 2).astype(BF16)
        else:
            ref = gv_ref if i < 4 else gr_ref
            base = _OFF[2] if i < 4 else _OFF[3]
            lo = (i % 2) * half_v
            ref[:, lo:lo + half_v] = proj_cols(base + lo, base + lo + half_v).astype(BF16)

    n_groups = SWA_ITEM_GROUPS
    per_group = len(items) // n_groups
    groups = [range(g * per_group, (g + 1) * per_group) for g in range(n_groups)]
    phases = (swa_score_dots, swa_row_max, swa_exp, swa_out)
    n_pieces = 6
    for stage in range(n_groups + len(phases) - 1):
        if stage < n_pieces:
            gla_piece(stage)
        for depth, phase in enumerate(phases):
            if 0 <= stage - depth < n_groups:
                phase(groups[stage - depth])
    for piece in range(n_groups + len(phases) - 1, n_pieces):
        gla_piece(piece)


def _in_proj_swa(h, g, w_parts, wal_pad, bal, rel_bias, sinks, cast_weights):
    B, T, D = h.shape
    tm = TOK_TILE
    L = WINDOW
    per = tm // L

    def tok(width):
        return pl.BlockSpec((None, tm, width), lambda b, i: (b, i, 0))

    def out(width, dt):
        return jax.ShapeDtypeStruct((B, T, width), dt)

    smem = pl.BlockSpec(memory_space=pltpu.SMEM)
    grid = (B, T // tm)
    cast_specs, cast_shapes = _cast_specs(cast_weights, grid)
    outs = pl.pallas_call(
        functools.partial(_inproj_swa_kernel, n_cast=len(cast_weights)),
        grid=grid,
        in_specs=[smem, smem, _const_spec((L, 2 * L)),
                  tok(D), _const_spec((1, D)), _const_spec((D, _OFF[-1])),
                  _const_spec((D, RANK_PAD)), _const_spec((D, D_SWA)),
                  _const_spec((RANK_PAD, GLA_KW)), _const_spec((1, GLA_KW))] + cast_specs,
        out_specs=[tok(GLA_KW), tok(GLA_KW), tok(GLA_VW), tok(GLA_VW), tok(GLA_KW), tok(SWA_QW)]
        + cast_specs,
        out_shape=[out(GLA_KW, BF16), out(GLA_KW, BF16), out(GLA_VW, BF16), out(GLA_VW, BF16),
                   out(GLA_KW, F32), out(SWA_QW, BF16)] + cast_shapes,
        scratch_shapes=[pltpu.VMEM((SWA_HEADS, L, 2 * L), F32),
                        pltpu.VMEM((2, L, SWA_KVW), F32),
                        pltpu.VMEM((per * SWA_HEADS // 2, 2 * L, 2 * L), F32),
                        pltpu.VMEM((per * SWA_HEADS, L, 2 * L), BF16)],
        compiler_params=_params(("arbitrary", "arbitrary")),
        name="in_proj_swa",
    )(sinks, rel_bias, _t5_bucket_table(), h, g, *w_parts, wal_pad, bal, *cast_weights)
    return outs[:6], outs[6:]


def _gla_kernel(q_ref, k_ref, v_ref, r_ref, la_ref, gain_ref, *refs, n_cast):
    cast_src, o_ref, refs = refs[:n_cast], refs[n_cast], refs[n_cast + 1:]
    cast_dst, (st_ref, q4_ref, kv_ref, dec_ref, oi_ref) = refs[:n_cast], refs[n_cast:]
    _cast_blocks(cast_src, cast_dst)

    C = GLA_CHUNK
    H = GLA_HEADS
    n_chunks = q_ref.shape[0] // C

    @pl.when(pl.program_id(1) == 0)
    def _():
        st_ref[...] = jnp.zeros_like(st_ref)

    tril = (lax.broadcasted_iota(jnp.int32, (C, C), 0)
            >= lax.broadcasted_iota(jnp.int32, (C, C), 1)).astype(BF16)
    tril2 = jnp.concatenate([tril, tril], axis=1)
    causal4 = ((lax.broadcasted_iota(jnp.int32, (H * C, C), 0) & (C - 1))
               >= lax.broadcasted_iota(jnp.int32, (H * C, C), 1))
    head_of_lane = lax.broadcasted_iota(jnp.int32, (C, GLA_KW), 1) // GLA_DK
    head_of_st_lane = lax.broadcasted_iota(jnp.int32, (GLA_DV, GLA_KW), 1) // GLA_DK
    gain = gain_ref[...]

    rows_of = [slice(c * C, (c + 1) * C) for c in range(n_chunks)]

    k_inv, k_end = {}, {}

    def decays(chunks):
        bcum = {}
        for c in chunks:
            la_hi, la_lo = _split_bf16(la_ref[rows_of[c], :])
            bcum[c] = _dot(tril2, jnp.concatenate([la_hi, la_lo], axis=0))

        for c in chunks:
            b_last = bcum[c][C - 1:C, :]
            dec_ref[c] = jnp.exp(b_last)
            q = q_ref[rows_of[c], :].astype(F32) * (GLA_DK ** -0.5)
            k = k_ref[rows_of[c], :].astype(F32)
            q_dec = (q * jnp.exp(bcum[c])).astype(BF16)
            k_inv[c] = (k * jnp.exp(-bcum[c])).astype(BF16)
            k_end[c] = (k * jnp.exp(b_last - bcum[c])).astype(BF16)
            q4_ref[c] = jnp.concatenate(
                [jnp.where(head_of_lane == h, q_dec, jnp.zeros_like(q_dec)) for h in range(H)],
                axis=0)

    def chunk_dots(chunks):
        att = {c: lax.dot_general(q4_ref[c], k_inv[c], _NT, preferred_element_type=F32)
               for c in chunks}
        kvt_all = {c: lax.dot_general(v_ref[rows_of[c], :], k_end[c], _TN,
                                      preferred_element_type=F32) for c in chunks}

        for c in chunks:
            att_c = jnp.where(causal4, att[c], 0.0).astype(BF16)
            for h in range(H):
                cols = slice(h * GLA_DV, (h + 1) * GLA_DV)
                oi_ref[rows_of[c], cols] = _dot(att_c[h * C:(h + 1) * C, :],
                                                v_ref[rows_of[c], cols])
            kvt = kvt_all[c][(H - 1) * GLA_DV:, :]
            for h in range(H - 2, -1, -1):
                kvt = jnp.where(head_of_st_lane == h, kvt_all[c][h * GLA_DV:(h + 1) * GLA_DV, :],
                                kvt)
            kv_ref[c] = kvt

    def recurrence(chunks, st):
        inter = {}
        for c in chunks:
            inter[c] = lax.dot_general(q4_ref[c], st.astype(BF16), _NT,
                                       preferred_element_type=F32)
            st = st * dec_ref[c] + kv_ref[c]
        return inter, st

    def finish(chunks, inter):
        for c in chunks:
            for h in range(H):
                cols = slice(h * GLA_DV, (h + 1) * GLA_DV)
                o = oi_ref[rows_of[c], cols] + inter[c][h * C:(h + 1) * C, :]
                o = o * lax.rsqrt(jnp.mean(o * o, axis=-1, keepdims=True) + EPS) * gain
                o_ref[rows_of[c], cols] = (
                    o * _silu(r_ref[rows_of[c], cols].astype(F32))).astype(o_ref.dtype)

    per_group = GLA_GROUP_CHUNKS
    groups = [range(g, min(g + per_group, n_chunks)) for g in range(0, n_chunks, per_group)]
    st = st_ref[...]
    inter_prev = None
    decays(groups[0])
    for gi, group in enumerate(groups):
        chunk_dots(group)
        if inter_prev is not None:
            finish(groups[gi - 1], inter_prev)
        if gi + 1 < len(groups):
            decays(groups[gi + 1])
        inter_prev, st = recurrence(group, st)
    st_ref[...] = st
    finish(groups[-1], inter_prev)


def _gla(gq, gk, gv, gr, la, gain, cast_weights):
    B, T, _ = gq.shape
    tm = GLA_TILE

    def tok(width):
        return pl.BlockSpec((None, tm, width), lambda b, i: (b, i, 0))

    grid = (B, T // tm)
    cast_specs, cast_shapes = _cast_specs(cast_weights, grid)
    outs = pl.pallas_call(
        functools.partial(_gla_kernel, n_cast=len(cast_weights)),
        grid=grid,
        in_specs=[tok(GLA_KW), tok(GLA_KW), tok(GLA_VW), tok(GLA_VW), tok(GLA_KW),
                  _const_spec((1, GLA_DV))] + cast_specs,
        out_specs=[tok(GLA_VW)] + cast_specs,
        out_shape=[jax.ShapeDtypeStruct((B, T, GLA_VW), BF16)] + cast_shapes,
        scratch_shapes=[pltpu.VMEM((GLA_DV, GLA_KW), F32),
                        pltpu.VMEM((tm // GLA_CHUNK, GLA_HEADS * GLA_CHUNK, GLA_KW), BF16),
                        pltpu.VMEM((tm // GLA_CHUNK, GLA_DV, GLA_KW), F32),
                        pltpu.VMEM((tm // GLA_CHUNK, 1, GLA_KW), F32),
                        pltpu.VMEM((tm, GLA_VW), F32)],
        compiler_params=_params(("parallel", "arbitrary")),
        name="gla",
    )(gq, gk, gv, gr, la, gain, *cast_weights)
    return outs[0], outs[1:]


def _memkv_kernel(mem_ref, g_ref, wk_ref, wv_ref, k_ref, v_ref):
    mn = _rms(mem_ref[...], g_ref[...]).astype(BF16)
    k_ref[...] = _dot(mn, wk_ref[...]).astype(BF16)
    v_ref[...] = _dot(mn, wv_ref[...]).astype(BF16)


def _mem_kv(mem, g, wk, wv):
    B, M, D = mem.shape
    rows = B * M
    tm = min(TOK_TILE, rows)
    blk = pl.BlockSpec((tm, D), lambda i: (i, 0))
    k, v = pl.pallas_call(
        _memkv_kernel,
        grid=(rows // tm,),
        in_specs=[blk, _const_spec((1, D)), _const_spec((D, D)), _const_spec((D, D))],
        out_specs=[blk, blk],
        out_shape=[jax.ShapeDtypeStruct((rows, D), BF16)] * 2,
        compiler_params=_params(("parallel",)),
        name="mem_kv",
    )(mem.reshape(rows, D), g, wk, wv)
    return k.reshape(B, M, D), v.reshape(B, M, D)


def _outcross_kernel(h_ref, og_ref, os_ref, kc_ref, vc_ref, wout_ref, g_ref,
                     wq_ref, wo_ref, o_ref):
    mix = jnp.concatenate([og_ref[...], os_ref[...]], axis=1)
    h1 = h_ref[...] + _dot(mix, wout_ref[...])
    inv_rms = lax.rsqrt(jnp.mean(h1 * h1, axis=-1, keepdims=True) + EPS)
    hg = (h1 * g_ref[...]).astype(BF16)
    q = (_dot(hg, wq_ref[...]) * (inv_rms * (CROSS_DH ** -0.5 * LOG2E))).astype(BF16)
    tm = h1.shape[0]
    blk = tm // CROSS_ROW_BLOCKS
    items = [(slice(rb * blk, (rb + 1) * blk), slice(hd * CROSS_DH, (hd + 1) * CROSS_DH))
             for rb in range(CROSS_ROW_BLOCKS) for hd in range(CROSS_HEADS)]
    scores, probs, attn = {}, {}, {}
    for stage in range(len(items) + 2):
        if stage < len(items):
            rows, c = items[stage]
            scores[stage] = lax.dot_general(q[rows, c], kc_ref[:, c], _NT,
                                            preferred_element_type=F32)
        if 0 <= stage - 1 < len(items):
            s = scores.pop(stage - 1)
            p = jnp.exp2(s - jnp.max(s, axis=-1, keepdims=True))
            probs[stage - 1] = (p.astype(BF16), 1.0 / jnp.sum(p, axis=-1, keepdims=True))
        if 0 <= stage - 2 < len(items):
            i = stage - 2
            rows, c = items[i]
            p, inv_denom = probs.pop(i)
            o = (_dot(p, vc_ref[:, c]) * inv_denom).astype(BF16)
            part = _dot(o, wo_ref[c, :])
            rb = i // CROSS_HEADS
            attn[rb] = part if i % CROSS_HEADS == 0 else attn[rb] + part
            if i % CROSS_HEADS == CROSS_HEADS - 1:
                o_ref[rows, :] = h1[rows, :] + attn.pop(rb)


def _out_cross(h, og, osw, kc, vc, wout, g, wq, wo):
    B, T, D = h.shape
    tm = TOK_TILE

    def tok(width):
        return pl.BlockSpec((None, tm, width), lambda b, i: (b, i, 0))

    memblk = pl.BlockSpec((None, MEM_LEN, D), lambda b, i: (b, 0, 0))
    return pl.pallas_call(
        _outcross_kernel,
        grid=(B, T // tm),
        in_specs=[tok(D), tok(GLA_VW), tok(SWA_QW), memblk, memblk,
                  _const_spec((GLA_VW + SWA_QW, D)), _const_spec((1, D)),
                  _const_spec((D, D)), _const_spec((D, D))],
        out_specs=tok(D),
        out_shape=jax.ShapeDtypeStruct((B, T, D), F32),
        compiler_params=_params(("parallel", "parallel")),
        name="out_cross",
    )(h, og, osw, kc, vc, wout, g, wq, wo)


def _ffn_kernel(h_ref, g_ref, wu_ref, wd_ref, cw_ref, cb_ref, gf_ref,
                o_ref, hn_ref, acc_ref, ubuf_ref, carry_ref, *, final_norm):
    tm = h_ref.shape[0]
    n_slab = 2 * FFN_CHUNK // LANES
    half = n_slab // 2

    @pl.when(pl.program_id(1) == 0)
    def _():
        carry_ref[...] = jnp.zeros_like(carry_ref)

    hn_ref[...] = _rms(h_ref[...], g_ref[...]).astype(BF16)

    def cols_of(c, s):
        start = (s // half) * D_FF + c * FFN_CHUNK + (s % half) * LANES
        return slice(start, start + LANES)

    def up(c):
        hn = hn_ref[...]
        for part in range(2):
            start = part * D_FF + c * FFN_CHUNK
            u = _dot(hn, wu_ref[:, start:start + FFN_CHUNK])
            for j in range(half):
                s = part * half + j
                us = u[:, j * LANES:(j + 1) * LANES]
                ubuf_ref[c % FFN_UBUFS, s, 0:CARRY_ROWS, :] = carry_ref[c, s]
                ubuf_ref[c % FFN_UBUFS, s, CARRY_ROWS:, :] = us
                carry_ref[c, s] = us[tm - CARRY_ROWS:, :]

    def conv(c, s, rows):
        lanes = cols_of(c, s)
        taps = [ubuf_ref[c % FFN_UBUFS, s,
                         CARRY_ROWS - d + rows.start:CARRY_ROWS - d + rows.stop, :]
                for d in range(CONV_WIDTH)]
        return (cw_ref[2:3, lanes] * taps[0]
                + (cw_ref[1:2, lanes] * taps[1] + (cw_ref[0:1, lanes] * taps[2] + cb_ref[:, lanes])))

    lookahead = FFN_UBUFS - 1
    for c in range(lookahead):
        up(c)
    last = N_FFN_CHUNKS - 1

    def down(first_chunk, last_chunk, rows, act):
        res = _dot(act, wd_ref[first_chunk * FFN_CHUNK:(last_chunk + 1) * FFN_CHUNK, :])
        base = h_ref[rows, :] if first_chunk == 0 else acc_ref[rows, :]
        if last_chunk < last:
            acc_ref[rows, :] = base + res
        else:
            out = base + res
            if final_norm:
                out = _rms(out, gf_ref[...])
            o_ref[rows, :] = out

    row_blocks = [slice(r, r + tm // FFN_ROW_BLOCKS) for r in range(0, tm, tm // FFN_ROW_BLOCKS)]
    pending = []
    for c in range(N_FFN_CHUNKS):
        if c + lookahead <= last:
            up(c + lookahead)
        pending.append([[(_silu(conv(c, s, rows)) * conv(c, half + s, rows)).astype(BF16)
                         for s in range(half)] for rows in row_blocks])
        if len(pending) == FFN_DOWN_GROUP or c == last:
            for b, rows in enumerate(row_blocks):
                act = jnp.concatenate([a for chunk_acts in pending for a in chunk_acts[b]], axis=1)
                down(c + 1 - len(pending), c, rows, act)
            pending = []


def _ffn(h, g, wu, wd, cw, cb, gf, final_norm):
    B, T, D = h.shape
    tm = FFN_TILE
    nc, fc = N_FFN_CHUNKS, FFN_CHUNK
    n_slab = 2 * fc // LANES
    tok = pl.BlockSpec((None, tm, D), lambda b, i: (b, i, 0))

    def once(shape):
        zeros = (0,) * len(shape)
        return pl.BlockSpec(shape, lambda *_: zeros, pipeline_mode=pl.Buffered(1))

    return pl.pallas_call(
        functools.partial(_ffn_kernel, final_norm=final_norm),
        grid=(B, T // tm),
        in_specs=[tok, _const_spec((1, D)), once((D, 2 * D_FF)), once((D_FF, D)),
                  _const_spec((CONV_WIDTH, 2 * D_FF)), _const_spec((1, 2 * D_FF)),
                  _const_spec((1, D))],
        out_specs=tok,
        out_shape=jax.ShapeDtypeStruct((B, T, D), F32),
        scratch_shapes=[pltpu.VMEM((tm, D), BF16), pltpu.VMEM((tm, D), F32),
                        pltpu.VMEM((FFN_UBUFS, n_slab, CARRY_ROWS + tm, LANES), F32),
                        pltpu.VMEM((nc, n_slab, CARRY_ROWS, LANES), F32)],
        compiler_params=_params(("parallel", "arbitrary")),
        name="ffn",
    )(h, g, wu, wd, cw, cb, gf)


def _split_in_proj(w_in):
    w_rank = jnp.pad(w_in[:, _IN_OFF[4]:_IN_OFF[5]], ((0, 0), (0, RANK_PAD - GLA_RANK)))
    return (w_in[:, :_IN_OFF[4]].astype(BF16), w_rank.astype(BF16),
            w_in[:, _IN_OFF[5]:].astype(BF16))


def kernel(x, mem, norm_mix, w_in, w_alpha, b_alpha, gla_gain, rel_bias, sinks, w_out, norm_cross,
           norm_mem, w_q_c, w_k_c, w_v_c, w_o_c, norm_ffn, w_up, conv_w, conv_b, w_down, norm_final):
    depth = w_in.shape[0]
    row = lambda v: v.reshape(1, -1).astype(F32)
    h = x
    for l in range(depth):
        wal_pad = jnp.pad(w_alpha[l], ((0, RANK_PAD - GLA_RANK), (0, 0)))
        (gq, gk, gv, gr, la, o_swa), (wout, wq, wo, wk, wv) = _in_proj_swa(
            h, row(norm_mix[l]), _split_in_proj(w_in[l]), wal_pad, row(b_alpha[l]),
            rel_bias.astype(F32), sinks[l].astype(F32),
            cast_weights=[w_out[l], w_q_c[l], w_o_c[l], w_k_c[l], w_v_c[l]])
        o_gla, (wup, wdown) = _gla(gq, gk, gv, gr, la, row(gla_gain[l]),
                                   cast_weights=[w_up[l], w_down[l]])
        kc, vc = _mem_kv(mem, row(norm_mem[l]), wk, wv)
        h = _out_cross(h, o_gla, o_swa, kc, vc, wout, row(norm_cross[l]), wq, wo)
        h = _ffn(h, row(norm_ffn[l]), wup, wdown, conv_w[l].astype(F32), row(conv_b[l]),
                 row(norm_final), final_norm=(l == depth - 1))
    return h
```

```python
import functools
import math

import numpy as np
import jax
import jax.numpy as jnp
from jax import lax
from jax.experimental import pallas as pl
from jax.experimental.pallas import tpu as pltpu

F32 = jnp.float32
BF16 = jnp.bfloat16

D_MODEL = 1024
MEM_LEN = 256
EPS = 1e-6
GLA_HEADS = 4
GLA_DK = 64
GLA_DV = 128
GLA_RANK = 16
GLA_TAU = 16.0
GLA_CHUNK = 64
SWA_HEADS = 8
SWA_KV_HEADS = 2
SWA_DH = 64
WINDOW = 128
REL_BUCKETS = 32
REL_MAX_DIST = 128
CROSS_HEADS = 4
CROSS_DH = D_MODEL // CROSS_HEADS
D_FF = 2816
CONV_WIDTH = 3

GLA_KW = GLA_HEADS * GLA_DK
GLA_VW = GLA_HEADS * GLA_DV
SWA_QW = SWA_HEADS * SWA_DH
SWA_KVW = SWA_KV_HEADS * SWA_DH
IN_SPLITS = (GLA_KW, GLA_KW, GLA_VW, GLA_VW, GLA_RANK, SWA_QW, SWA_KVW, SWA_KVW)

LOG2E = math.log2(math.e)
LANES = 128
BF16_SUBLANES = 16
RANK_PAD = LANES
_IN_OFF = tuple(int(v) for v in np.cumsum((0,) + IN_SPLITS))
_OFF = _IN_OFF[:5]
D_SWA = SWA_QW + 2 * SWA_KVW

F32_SUBLANES = 8
V7X_VMEM_BYTES = 64 * 1024 * 1024
VMEM_LIMIT = V7X_VMEM_BYTES * 7 // 8

TOK_TILE = 1024
FFN_TILE = 512
GLA_TILE = 2048
CROSS_ROW_BLOCKS = 2
SWA_ITEM_GROUPS = 4
GLA_GROUP_CHUNKS = 4
FFN_CHUNK = 256
N_FFN_CHUNKS = D_FF // FFN_CHUNK
FFN_DOWN_GROUP = 2
FFN_ROW_BLOCKS = 2
FFN_UBUFS = 4
CARRY_ROWS = F32_SUBLANES

_NT = (((1,), (1,)), ((), ()))
_TN = (((0,), (0,)), ((), ()))


def _rms(x, g):
    return x * lax.rsqrt(jnp.mean(x * x, axis=-1, keepdims=True) + EPS) * g


def _dot(a, b):
    return jnp.dot(a, b, preferred_element_type=F32)


def _split_bf16(x):
    hi = x.astype(BF16)
    return hi, (x - hi.astype(F32)).astype(BF16)


def _silu(x):
    return x * (1.0 / (1.0 + jnp.exp2(x * -LOG2E)))


def _const_spec(shape):
    zeros = (0,) * len(shape)
    return pl.BlockSpec(shape, lambda *_: zeros)


def _params(sem):
    return pltpu.CompilerParams(dimension_semantics=sem, vmem_limit_bytes=VMEM_LIMIT)


def _cast_specs(weights, grid):
    n_steps = grid[0] * grid[1]
    specs, shapes = [], []
    for w in weights:
        rows = w.shape[0] // n_steps
        assert rows * n_steps == w.shape[0] and rows % BF16_SUBLANES == 0, w.shape
        specs.append(pl.BlockSpec((rows, w.shape[1]), lambda b, i: (b * grid[1] + i, 0)))
        shapes.append(jax.ShapeDtypeStruct(w.shape, BF16))
    return specs, shapes


def _cast_blocks(src_refs, dst_refs):
    for src, dst in zip(src_refs, dst_refs):
        dst[...] = src[...].astype(BF16)


def _t5_bucket_table():
    L = WINDOW
    dist = (jnp.arange(L)[:, None] + L) - jnp.arange(2 * L)[None, :]
    n = jnp.maximum(dist, 0)
    max_exact = REL_BUCKETS // 2
    nf = jnp.maximum(n, 1).astype(F32)
    large = max_exact + (jnp.log(nf / max_exact) / math.log(REL_MAX_DIST / max_exact)
                         * (REL_BUCKETS - max_exact)).astype(jnp.int32)
    large = jnp.minimum(large, REL_BUCKETS - 1)
    bucket = jnp.where(n < max_exact, n, large)
    return jnp.where((dist >= 0) & (dist < WINDOW), bucket, -1).astype(jnp.int32)


def _inproj_swa_kernel(sinks_ref, relb_ref, bucket_ref, x_ref, g_ref, wg_ref, wa_ref, ws_ref,
                       wal_ref, bal_ref, *refs, n_cast):
    cast_src, refs = refs[:n_cast], refs[n_cast:]
    gq_ref, gk_ref, gv_ref, gr_ref, la_ref, o_ref = refs[:6]
    cast_dst, (bias_ref, kv_prev_ref, s_ref, pr_ref) = refs[6:6 + n_cast], refs[6 + n_cast:]
    _cast_blocks(cast_src, cast_dst)

    L = WINDOW
    tm = x_ref.shape[0]
    n_blk = tm // L
    pair_w = 2 * SWA_DH
    n_pairs = SWA_QW // pair_w
    pairs_per_kv = n_pairs // SWA_KV_HEADS
    neg_inf = float("-inf")

    @pl.when((pl.program_id(0) == 0) & (pl.program_id(1) == 0))
    def _():
        bucket = bucket_ref[...]
        for h in range(SWA_HEADS):
            def body(b, acc):
                return jnp.where(bucket == b, relb_ref[b, h] * LOG2E, acc)
            bias_ref[h] = lax.fori_loop(0, REL_BUCKETS, body, jnp.full((L, 2 * L), neg_inf, F32))

    @pl.when(pl.program_id(1) == 0)
    def _():
        kv_prev_ref[...] = jnp.zeros_like(kv_prev_ref)

    x = x_ref[...]
    xg = (x * g_ref[...]).astype(BF16)
    inv_rms = jnp.broadcast_to(lax.rsqrt(jnp.mean(x * x, axis=-1, keepdims=True) + EPS),
                               (tm, LANES))

    def scaled_dot(w):
        res = _dot(xg, w)
        return jnp.concatenate([res[:, c:c + LANES] * inv_rms
                                for c in range(0, res.shape[1], LANES)], axis=1)

    def proj_cols(start, stop):
        return scaled_dot(wg_ref[:, start:stop])

    def proj(lo, hi):
        return proj_cols(_OFF[lo], _OFF[hi])

    swa = scaled_dot(ws_ref[...])
    sq = (swa[:, :SWA_QW] * (SWA_DH ** -0.5 * LOG2E)).astype(BF16)
    k_cat = jnp.concatenate([kv_prev_ref[0], swa[:, SWA_QW:SWA_QW + SWA_KVW]], axis=0)
    v_cat = jnp.concatenate([kv_prev_ref[1], swa[:, SWA_QW + SWA_KVW:]], axis=0)
    kv_prev_ref[0] = k_cat[tm:, :]
    kv_prev_ref[1] = v_cat[tm:, :]

    def dup_heads(cat):
        low = lax.broadcasted_iota(jnp.int32, cat.shape, 1) < SWA_DH
        rolled = pltpu.roll(cat, SWA_DH, 1)
        return low, (jnp.where(low, cat, rolled), jnp.where(low, rolled, cat))

    _, kdup = dup_heads(k_cat)
    low_kv, vdup = dup_heads(v_cat)
    kd = [kg.astype(BF16) for kg in kdup]
    vd = [[jnp.where(low_kv, vg, 1.0).astype(BF16), jnp.where(low_kv, 1.0, vg).astype(BF16)]
          for vg in vdup]

    low = lax.broadcasted_iota(jnp.int32, (L, pair_w), 1) < SWA_DH
    no_prev = jnp.where((pl.program_id(1) == 0)
                        & (lax.broadcasted_iota(jnp.int32, (L, 2 * L), 1) < L), neg_inf, 0.0)
    items = [(j, p) for j in range(n_blk) for p in range(n_pairs)]
    sinks = [sinks_ref[h] * LOG2E for h in range(SWA_HEADS)]
    row_max = {}

    def biased_scores(t, e):
        j, p = items[t]
        s = s_ref[t, e * L:(e + 1) * L, :] + bias_ref[2 * p + e]
        return s + no_prev if j == 0 else s

    def swa_score_dots(ts):
        for t in ts:
            j, p = items[t]
            qp = sq[j * L:(j + 1) * L, p * pair_w:(p + 1) * pair_w]
            zero = jnp.zeros_like(qp)
            q2 = jnp.concatenate([jnp.where(low, qp, zero), jnp.where(low, zero, qp)], axis=0)
            keys = kd[p // pairs_per_kv][j * L:(j + 2) * L, :]
            s_ref[t] = lax.dot_general(q2, keys, _NT, preferred_element_type=F32)

    def swa_row_max(ts):
        for t in ts:
            for e in range(2):
                s = biased_scores(t, e)
                s_ref[t, e * L:(e + 1) * L, :] = s
                row_max[t, e] = jnp.maximum(jnp.max(s, axis=-1, keepdims=True),
                                            sinks[2 * items[t][1] + e])

    def swa_exp(ts):
        for t in ts:
            for e in range(2):
                s = s_ref[t, e * L:(e + 1) * L, :]
                pr_ref[2 * t + e] = jnp.exp2(s - row_max[t, e]).astype(BF16)

    def swa_out(ts):
        for t in ts:
            j, p = items[t]
            vals = vd[p // pairs_per_kv]
            outs = [_dot(pr_ref[2 * t + e], vals[e][j * L:(j + 2) * L, :]) for e in range(2)]
            sink_terms = [jnp.exp2(sinks[2 * p + e] - row_max[t, e]) for e in range(2)]
            numer = jnp.where(low, outs[0], outs[1])
            denom = (pltpu.roll(jnp.where(low, outs[1], outs[0]), SWA_DH, 1)
                     + jnp.where(low, sink_terms[0], sink_terms[1]))
            o_ref[j * L:(j + 1) * L, p * pair_w:(p + 1) * pair_w] = (
                numer * (1.0 / denom)).astype(o_ref.dtype)

    a_low = scaled_dot(wa_ref[...])
    a_hi, a_lo = _split_bf16(a_low)
    w_hi, w_lo = _split_bf16(wal_ref[...])
    z = _dot(a_hi, w_hi) + _dot(a_lo, w_hi) + _dot(a_hi, w_lo) + bal_ref[...]
    la_ref[...] = -(jnp.maximum(-z, 0.0) + jnp.log1p(jnp.exp(-jnp.abs(z)))) * (1.0 / GLA_TAU)

    half_v = GLA_VW // 2

    def gla_piece(i):
        if i == 0:
            gq_ref[...] = proj(0, 1).astype(BF16)
        elif i == 1:
            gk_ref[...] = proj(1, 2).astype(BF16)
        else:
            ref = gv_ref if i < 4 else gr_ref
            base = _OFF[2] if i < 4 else _OFF[3]
            lo = (i % 2) * half_v
            ref[:, lo:lo + half_v] = proj_cols(base + lo, base + lo + half_v).astype(BF16)

    n_groups = SWA_ITEM_GROUPS
    per_group = len(items) // n_groups
    groups = [range(g * per_group, (g + 1) * per_group) for g in range(n_groups)]
    phases = (swa_score_dots, swa_row_max, swa_exp, swa_out)
    n_pieces = 6
    for stage in range(n_groups + len(phases) - 1):
        if stage < n_pieces:
            gla_piece(stage)
        for depth, phase in enumerate(phases):
            if 0 <= stage - depth < n_groups:
                phase(groups[stage - depth])
    for piece in range(n_groups + len(phases) - 1, n_pieces):
        gla_piece(piece)


def _in_proj_swa(h, g, w_parts, wal_pad, bal, rel_bias, sinks, cast_weights):
    B, T, D = h.shape
    tm = TOK_TILE
    L = WINDOW
    per = tm // L

    def tok(width):
        return pl.BlockSpec((None, tm, width), lambda b, i: (b, i, 0))

    def out(width, dt):
        return jax.ShapeDtypeStruct((B, T, width), dt)

    smem = pl.BlockSpec(memory_space=pltpu.SMEM)
    grid = (B, T // tm)
    cast_specs, cast_shapes = _cast_specs(cast_weights, grid)
    outs = pl.pallas_call(
        functools.partial(_inproj_swa_kernel, n_cast=len(cast_weights)),
        grid=grid,
        in_specs=[smem, smem, _const_spec((L, 2 * L)),
                  tok(D), _const_spec((1, D)), _const_spec((D, _OFF[-1])),
                  _const_spec((D, RANK_PAD)), _const_spec((D, D_SWA)),
                  _const_spec((RANK_PAD, GLA_KW)), _const_spec((1, GLA_KW))] + cast_specs,
        out_specs=[tok(GLA_KW), tok(GLA_KW), tok(GLA_VW), tok(GLA_VW), tok(GLA_KW), tok(SWA_QW)]
        + cast_specs,
        out_shape=[out(GLA_KW, BF16), out(GLA_KW, BF16), out(GLA_VW, BF16), out(GLA_VW, BF16),
                   out(GLA_KW, F32), out(SWA_QW, BF16)] + cast_shapes,
        scratch_shapes=[pltpu.VMEM((SWA_HEADS, L, 2 * L), F32),
                        pltpu.VMEM((2, L, SWA_KVW), F32),
                        pltpu.VMEM((per * SWA_HEADS // 2, 2 * L, 2 * L), F32),
                        pltpu.VMEM((per * SWA_HEADS, L, 2 * L), BF16)],
        compiler_params=_params(("arbitrary", "arbitrary")),
        name="in_proj_swa",
    )(sinks, rel_bias, _t5_bucket_table(), h, g, *w_parts, wal_pad, bal, *cast_weights)
    return outs[:6], outs[6:]


def _gla_kernel(q_ref, k_ref, v_ref, r_ref, la_ref, gain_ref, *refs, n_cast):
    cast_src, o_ref, refs = refs[:n_cast], refs[n_cast], refs[n_cast + 1:]
    cast_dst, (st_ref, q4_ref, kv_ref, dec_ref, oi_ref) = refs[:n_cast], refs[n_cast:]
    _cast_blocks(cast_src, cast_dst)

    C = GLA_CHUNK
    H = GLA_HEADS
    n_chunks = q_ref.shape[0] // C

    @pl.when(pl.program_id(1) == 0)
    def _():
        st_ref[...] = jnp.zeros_like(st_ref)

    tril = (lax.broadcasted_iota(jnp.int32, (C, C), 0)
            >= lax.broadcasted_iota(jnp.int32, (C, C), 1)).astype(BF16)
    tril2 = jnp.concatenate([tril, tril], axis=1)
    causal4 = ((lax.broadcasted_iota(jnp.int32, (H * C, C), 0) & (C - 1))
               >= lax.broadcasted_iota(jnp.int32, (H * C, C), 1))
    head_of_lane = lax.broadcasted_iota(jnp.int32, (C, GLA_KW), 1) // GLA_DK
    head_of_st_lane = lax.broadcasted_iota(jnp.int32, (GLA_DV, GLA_KW), 1) // GLA_DK
    gain = gain_ref[...]

    rows_of = [slice(c * C, (c + 1) * C) for c in range(n_chunks)]

    k_inv, k_end = {}, {}

    def decays(chunks):
        bcum = {}
        for c in chunks:
            la_hi, la_lo = _split_bf16(la_ref[rows_of[c], :])
            bcum[c] = _dot(tril2, jnp.concatenate([la_hi, la_lo], axis=0))

        for c in chunks:
            b_last = bcum[c][C - 1:C, :]
            dec_ref[c] = jnp.exp(b_last)
            q = q_ref[rows_of[c], :].astype(F32) * (GLA_DK ** -0.5)
            k = k_ref[rows_of[c], :].astype(F32)
            q_dec = (q * jnp.exp(bcum[c])).astype(BF16)
            k_inv[c] = (k * jnp.exp(-bcum[c])).astype(BF16)
            k_end[c] = (k * jnp.exp(b_last - bcum[c])).astype(BF16)
            q4_ref[c] = jnp.concatenate(
                [jnp.where(head_of_lane == h, q_dec, jnp.zeros_like(q_dec)) for h in range(H)],
                axis=0)

    def chunk_dots(chunks):
        att = {c: lax.dot_general(q4_ref[c], k_inv[c], _NT, preferred_element_type=F32)
               for c in chunks}
        kvt_all = {c: lax.dot_general(v_ref[rows_of[c], :], k_end[c], _TN,
                                      preferred_element_type=F32) for c in chunks}

        for c in chunks:
            att_c = jnp.where(causal4, att[c], 0.0).astype(BF16)
            for h in range(H):
                cols = slice(h * GLA_DV, (h + 1) * GLA_DV)
                oi_ref[rows_of[c], cols] = _dot(att_c[h * C:(h + 1) * C, :],
                                                v_ref[rows_of[c], cols])
            kvt = kvt_all[c][(H - 1) * GLA_DV:, :]
            for h in range(H - 2, -1, -1):
                kvt = jnp.where(head_of_st_lane == h, kvt_all[c][h * GLA_DV:(h + 1) * GLA_DV, :],
                                kvt)
            kv_ref[c] = kvt

    def recurrence(chunks, st):
        inter = {}
        for c in chunks:
            inter[c] = lax.dot_general(q4_ref[c], st.astype(BF16), _NT,
                                       preferred_element_type=F32)
            st = st * dec_ref[c] + kv_ref[c]
        return inter, st

    def finish(chunks, inter):
        for c in chunks:
            for h in range(H):
                cols = slice(h * GLA_DV, (h + 1) * GLA_DV)
                o = oi_ref[rows_of[c], cols] + inter[c][h * C:(h + 1) * C, :]
                o = o * lax.rsqrt(jnp.mean(o * o, axis=-1, keepdims=True) + EPS) * gain
                o_ref[rows_of[c], cols] = (
                    o * _silu(r_ref[rows_of[c], cols].astype(F32))).astype(o_ref.dtype)

    per_group = GLA_GROUP_CHUNKS
    groups = [range(g, min(g + per_group, n_chunks)) for g in range(0, n_chunks, per_group)]
    st = st_ref[...]
    inter_prev = None
    decays(groups[0])
    for gi, group in enumerate(groups):
        chunk_dots(group)
        if inter_prev is not None:
            finish(groups[gi - 1], inter_prev)
        if gi + 1 < len(groups):
            decays(groups[gi + 1])
        inter_prev, st = recurrence(group, st)
    st_ref[...] = st
    finish(groups[-1], inter_prev)


def _gla(gq, gk, gv, gr, la, gain, cast_weights):
    B, T, _ = gq.shape
    tm = GLA_TILE

    def tok(width):
        return pl.BlockSpec((None, tm, width), lambda b, i: (b, i, 0))

    grid = (B, T // tm)
    cast_specs, cast_shapes = _cast_specs(cast_weights, grid)
    outs = pl.pallas_call(
        functools.partial(_gla_kernel, n_cast=len(cast_weights)),
        grid=grid,
        in_specs=[tok(GLA_KW), tok(GLA_KW), tok(GLA_VW), tok(GLA_VW), tok(GLA_KW),
                  _const_spec((1, GLA_DV))] + cast_specs,
        out_specs=[tok(GLA_VW)] + cast_specs,
        out_shape=[jax.ShapeDtypeStruct((B, T, GLA_VW), BF16)] + cast_shapes,
        scratch_shapes=[pltpu.VMEM((GLA_DV, GLA_KW), F32),
                        pltpu.VMEM((tm // GLA_CHUNK, GLA_HEADS * GLA_CHUNK, GLA_KW), BF16),
                        pltpu.VMEM((tm // GLA_CHUNK, GLA_DV, GLA_KW), F32),
                        pltpu.VMEM((tm // GLA_CHUNK, 1, GLA_KW), F32),
                        pltpu.VMEM((tm, GLA_VW), F32)],
        compiler_params=_params(("parallel", "arbitrary")),
        name="gla",
    )(gq, gk, gv, gr, la, gain, *cast_weights)
    return outs[0], outs[1:]


def _memkv_kernel(mem_ref, g_ref, wk_ref, wv_ref, k_ref, v_ref):
    mn = _rms(mem_ref[...], g_ref[...]).astype(BF16)
    k_ref[...] = _dot(mn, wk_ref[...]).astype(BF16)
    v_ref[...] = _dot(mn, wv_ref[...]).astype(BF16)


def _mem_kv(mem, g, wk, wv):
    B, M, D = mem.shape
    rows = B * M
    tm = min(TOK_TILE, rows)
    blk = pl.BlockSpec((tm, D), lambda i: (i, 0))
    k, v = pl.pallas_call(
        _memkv_kernel,
        grid=(rows // tm,),
        in_specs=[blk, _const_spec((1, D)), _const_spec((D, D)), _const_spec((D, D))],
        out_specs=[blk, blk],
        out_shape=[jax.ShapeDtypeStruct((rows, D), BF16)] * 2,
        compiler_params=_params(("parallel",)),
        name="mem_kv",
    )(mem.reshape(rows, D), g, wk, wv)
    return k.reshape(B, M, D), v.reshape(B, M, D)


def _outcross_kernel(h_ref, og_ref, os_ref, kc_ref, vc_ref, wout_ref, g_ref,
                     wq_ref, wo_ref, o_ref):
    mix = jnp.concatenate([og_ref[...], os_ref[...]], axis=1)
    h1 = h_ref[...] + _dot(mix, wout_ref[...])
    inv_rms = lax.rsqrt(jnp.mean(h1 * h1, axis=-1, keepdims=True) + EPS)
    hg = (h1 * g_ref[...]).astype(BF16)
    q = (_dot(hg, wq_ref[...]) * (inv_rms * (CROSS_DH ** -0.5 * LOG2E))).astype(BF16)
    tm = h1.shape[0]
    blk = tm // CROSS_ROW_BLOCKS
    items = [(slice(rb * blk, (rb + 1) * blk), slice(hd * CROSS_DH, (hd + 1) * CROSS_DH))
             for rb in range(CROSS_ROW_BLOCKS) for hd in range(CROSS_HEADS)]
    scores, probs, attn = {}, {}, {}
    for stage in range(len(items) + 2):
        if stage < len(items):
            rows, c = items[stage]
            scores[stage] = lax.dot_general(q[rows, c], kc_ref[:, c], _NT,
                                            preferred_element_type=F32)
        if 0 <= stage - 1 < len(items):
            s = scores.pop(stage - 1)
            p = jnp.exp2(s - jnp.max(s, axis=-1, keepdims=True))
            probs[stage - 1] = (p.astype(BF16), 1.0 / jnp.sum(p, axis=-1, keepdims=True))
        if 0 <= stage - 2 < len(items):
            i = stage - 2
            rows, c = items[i]
            p, inv_denom = probs.pop(i)
            o = (_dot(p, vc_ref[:, c]) * inv_denom).astype(BF16)
            part = _dot(o, wo_ref[c, :])
            rb = i // CROSS_HEADS
            attn[rb] = part if i % CROSS_HEADS == 0 else attn[rb] + part
            if i % CROSS_HEADS == CROSS_HEADS - 1:
                o_ref[rows, :] = h1[rows, :] + attn.pop(rb)


def _out_cross(h, og, osw, kc, vc, wout, g, wq, wo):
    B, T, D = h.shape
    tm = TOK_TILE

    def tok(width):
        return pl.BlockSpec((None, tm, width), lambda b, i: (b, i, 0))

    memblk = pl.BlockSpec((None, MEM_LEN, D), lambda b, i: (b, 0, 0))
    return pl.pallas_call(
        _outcross_kernel,
        grid=(B, T // tm),
        in_specs=[tok(D), tok(GLA_VW), tok(SWA_QW), memblk, memblk,
                  _const_spec((GLA_VW + SWA_QW, D)), _const_spec((1, D)),
                  _const_spec((D, D)), _const_spec((D, D))],
        out_specs=tok(D),
        out_shape=jax.ShapeDtypeStruct((B, T, D), F32),
        compiler_params=_params(("parallel", "parallel")),
        name="out_cross",
    )(h, og, osw, kc, vc, wout, g, wq, wo)


def _ffn_kernel(h_ref, g_ref, wu_ref, wd_ref, cw_ref, cb_ref, gf_ref,
                o_ref, hn_ref, acc_ref, ubuf_ref, carry_ref, *, final_norm):
    tm = h_ref.shape[0]
    n_slab = 2 * FFN_CHUNK // LANES
    half = n_slab // 2

    @pl.when(pl.program_id(1) == 0)
    def _():
        carry_ref[...] = jnp.zeros_like(carry_ref)

    hn_ref[...] = _rms(h_ref[...], g_ref[...]).astype(BF16)

    def cols_of(c, s):
        start = (s // half) * D_FF + c * FFN_CHUNK + (s % half) * LANES
        return slice(start, start + LANES)

    def up(c):
        hn = hn_ref[...]
        for part in range(2):
            start = part * D_FF + c * FFN_CHUNK
            u = _dot(hn, wu_ref[:, start:start + FFN_CHUNK])
            for j in range(half):
                s = part * half + j
                us = u[:, j * LANES:(j + 1) * LANES]
                ubuf_ref[c % FFN_UBUFS, s, 0:CARRY_ROWS, :] = carry_ref[c, s]
                ubuf_ref[c % FFN_UBUFS, s, CARRY_ROWS:, :] = us
                carry_ref[c, s] = us[tm - CARRY_ROWS:, :]

    def conv(c, s, rows):
        lanes = cols_of(c, s)
        taps = [ubuf_ref[c % FFN_UBUFS, s,
                         CARRY_ROWS - d + rows.start:CARRY_ROWS - d + rows.stop, :]
                for d in range(CONV_WIDTH)]
        return (cw_ref[2:3, lanes] * taps[0]
                + (cw_ref[1:2, lanes] * taps[1] + (cw_ref[0:1, lanes] * taps[2] + cb_ref[:, lanes])))

    lookahead = FFN_UBUFS - 1
    for c in range(lookahead):
        up(c)
    last = N_FFN_CHUNKS - 1

    def down(first_chunk, last_chunk, rows, act):
        res = _dot(act, wd_ref[first_chunk * FFN_CHUNK:(last_chunk + 1) * FFN_CHUNK, :])
        base = h_ref[rows, :] if first_chunk == 0 else acc_ref[rows, :]
        if last_chunk < last:
            acc_ref[rows, :] = base + res
        else:
            out = base + res
            if final_norm:
                out = _rms(out, gf_ref[...])
            o_ref[rows, :] = out

    row_blocks = [slice(r, r + tm // FFN_ROW_BLOCKS) for r in range(0, tm, tm // FFN_ROW_BLOCKS)]
    pending = []
    for c in range(N_FFN_CHUNKS):
        if c + lookahead <= last:
            up(c + lookahead)
        pending.append([[(_silu(conv(c, s, rows)) * conv(c, half + s, rows)).astype(BF16)
                         for s in range(half)] for rows in row_blocks])
        if len(pending) == FFN_DOWN_GROUP or c == last:
            for b, rows in enumerate(row_blocks):
                act = jnp.concatenate([a for chunk_acts in pending for a in chunk_acts[b]], axis=1)
                down(c + 1 - len(pending), c, rows, act)
            pending = []


def _ffn(h, g, wu, wd, cw, cb, gf, final_norm):
    B, T, D = h.shape
    tm = FFN_TILE
    nc, fc = N_FFN_CHUNKS, FFN_CHUNK
    n_slab = 2 * fc // LANES
    tok = pl.BlockSpec((None, tm, D), lambda b, i: (b, i, 0))

    def once(shape):
        zeros = (0,) * len(shape)
        return pl.BlockSpec(shape, lambda *_: zeros, pipeline_mode=pl.Buffered(1))

    return pl.pallas_call(
        functools.partial(_ffn_kernel, final_norm=final_norm),
        grid=(B, T // tm),
        in_specs=[tok, _const_spec((1, D)), once((D, 2 * D_FF)), once((D_FF, D)),
                  _const_spec((CONV_WIDTH, 2 * D_FF)), _const_spec((1, 2 * D_FF)),
                  _const_spec((1, D))],
        out_specs=tok,
        out_shape=jax.ShapeDtypeStruct((B, T, D), F32),
        scratch_shapes=[pltpu.VMEM((tm, D), BF16), pltpu.VMEM((tm, D), F32),
                        pltpu.VMEM((FFN_UBUFS, n_slab, CARRY_ROWS + tm, LANES), F32),
                        pltpu.VMEM((nc, n_slab, CARRY_ROWS, LANES), F32)],
        compiler_params=_params(("parallel", "arbitrary")),
        name="ffn",
    )(h, g, wu, wd, cw, cb, gf)


def _split_in_proj(w_in):
    w_rank = jnp.pad(w_in[:, _IN_OFF[4]:_IN_OFF[5]], ((0, 0), (0, RANK_PAD - GLA_RANK)))
    return (w_in[:, :_IN_OFF[4]].astype(BF16), w_rank.astype(BF16),
            w_in[:, _IN_OFF[5]:].astype(BF16))


def kernel(x, mem, norm_mix, w_in, w_alpha, b_alpha, gla_gain, rel_bias, sinks, w_out, norm_cross,
           norm_mem, w_q_c, w_k_c, w_v_c, w_o_c, norm_ffn, w_up, conv_w, conv_b, w_down, norm_final):
    depth = w_in.shape[0]
    row = lambda v: v.reshape(1, -1).astype(F32)
    h = x
    for l in range(depth):
        wal_pad = jnp.pad(w_alpha[l], ((0, RANK_PAD - GLA_RANK), (0, 0)))
        (gq, gk, gv, gr, la, o_swa), (wout, wq, wo, wk, wv) = _in_proj_swa(
            h, row(norm_mix[l]), _split_in_proj(w_in[l]), wal_pad, row(b_alpha[l]),
            rel_bias.astype(F32), sinks[l].astype(F32),
            cast_weights=[w_out[l], w_q_c[l], w_o_c[l], w_k_c[l], w_v_c[l]])
        o_gla, (wup, wdown) = _gla(gq, gk, gv, gr, la, row(gla_gain[l]),
                                   cast_weights=[w_up[l], w_down[l]])
        kc, vc = _mem_kv(mem, row(norm_mem[l]), wk, wv)
        h = _out_cross(h, o_gla, o_swa, kc, vc, wout, row(norm_cross[l]), wq, wo)
        h = _ffn(h, row(norm_ffn[l]), wup, wdown, conv_w[l].astype(F32), row(conv_b[l]),
                 row(norm_final), final_norm=(l == depth - 1))
    return h
```

```python
import functools
import math

import numpy as np
import jax
import jax.numpy as jnp
from jax import lax
from jax.experimental import pallas as pl
from jax.experimental.pallas import tpu as pltpu

F32 = jnp.float32
BF16 = jnp.bfloat16

D_MODEL = 1024
MEM_LEN = 256
EPS = 1e-6
GLA_HEADS = 4
GLA_DK = 64
GLA_DV = 128
GLA_RANK = 16
GLA_TAU = 16.0
GLA_CHUNK = 64
SWA_HEADS = 8
SWA_KV_HEADS = 2
SWA_DH = 64
WINDOW = 128
REL_BUCKETS = 32
REL_MAX_DIST = 128
CROSS_HEADS = 4
CROSS_DH = D_MODEL // CROSS_HEADS
D_FF = 2816
CONV_WIDTH = 3

GLA_KW = GLA_HEADS * GLA_DK
GLA_VW = GLA_HEADS * GLA_DV
SWA_QW = SWA_HEADS * SWA_DH
SWA_KVW = SWA_KV_HEADS * SWA_DH
IN_SPLITS = (GLA_KW, GLA_KW, GLA_VW, GLA_VW, GLA_RANK, SWA_QW, SWA_KVW, SWA_KVW)

LOG2E = math.log2(math.e)
LANES = 128
BF16_SUBLANES = 16
RANK_PAD = LANES
_IN_OFF = tuple(int(v) for v in np.cumsum((0,) + IN_SPLITS))
_OFF = _IN_OFF[:5]
D_SWA = SWA_QW + 2 * SWA_KVW

F32_SUBLANES = 8
V7X_VMEM_BYTES = 64 * 1024 * 1024
VMEM_LIMIT = V7X_VMEM_BYTES * 7 // 8

TOK_TILE = 1024
FFN_TILE = 512
GLA_TILE = 2048
CROSS_ROW_BLOCKS = 2
SWA_ITEM_GROUPS = 4
GLA_GROUP_CHUNKS = 4
FFN_CHUNK = 256
N_FFN_CHUNKS = D_FF // FFN_CHUNK
FFN_DOWN_GROUP = 2
FFN_ROW_BLOCKS = 2
FFN_UBUFS = 4
CARRY_ROWS = F32_SUBLANES

_NT = (((1,), (1,)), ((), ()))
_TN = (((0,), (0,)), ((), ()))


def _rms(x, g):
    return x * lax.rsqrt(jnp.mean(x * x, axis=-1, keepdims=True) + EPS) * g


def _dot(a, b):
    return jnp.dot(a, b, preferred_element_type=F32)


def _split_bf16(x):
    hi = x.astype(BF16)
    return hi, (x - hi.astype(F32)).astype(BF16)


def _silu(x):
    return x * (1.0 / (1.0 + jnp.exp2(x * -LOG2E)))


def _const_spec(shape):
    zeros = (0,) * len(shape)
    return pl.BlockSpec(shape, lambda *_: zeros)


def _params(sem):
    return pltpu.CompilerParams(dimension_semantics=sem, vmem_limit_bytes=VMEM_LIMIT)


def _cast_specs(weights, grid):
    n_steps = grid[0] * grid[1]
    specs, shapes = [], []
    for w in weights:
        rows = w.shape[0] // n_steps
        assert rows * n_steps == w.shape[0] and rows % BF16_SUBLANES == 0, w.shape
        specs.append(pl.BlockSpec((rows, w.shape[1]), lambda b, i: (b * grid[1] + i, 0)))
        shapes.append(jax.ShapeDtypeStruct(w.shape, BF16))
    return specs, shapes


def _cast_blocks(src_refs, dst_refs):
    for src, dst in zip(src_refs, dst_refs):
        dst[...] = src[...].astype(BF16)


def _t5_bucket_table():
    L = WINDOW
    dist = (jnp.arange(L)[:, None] + L) - jnp.arange(2 * L)[None, :]
    n = jnp.maximum(dist, 0)
    max_exact = REL_BUCKETS // 2
    nf = jnp.maximum(n, 1).astype(F32)
    large = max_exact + (jnp.log(nf / max_exact) / math.log(REL_MAX_DIST / max_exact)
                         * (REL_BUCKETS - max_exact)).astype(jnp.int32)
    large = jnp.minimum(large, REL_BUCKETS - 1)
    bucket = jnp.where(n < max_exact, n, large)
    return jnp.where((dist >= 0) & (dist < WINDOW), bucket, -1).astype(jnp.int32)


def _inproj_swa_kernel(sinks_ref, relb_ref, bucket_ref, x_ref, g_ref, wg_ref, wa_ref, ws_ref,
                       wal_ref, bal_ref, *refs, n_cast):
    cast_src, refs = refs[:n_cast], refs[n_cast:]
    gq_ref, gk_ref, gv_ref, gr_ref, la_ref, o_ref = refs[:6]
    cast_dst, (bias_ref, kv_prev_ref, s_ref, pr_ref) = refs[6:6 + n_cast], refs[6 + n_cast:]
    _cast_blocks(cast_src, cast_dst)

    L = WINDOW
    tm = x_ref.shape[0]
    n_blk = tm // L
    pair_w = 2 * SWA_DH
    n_pairs = SWA_QW // pair_w
    pairs_per_kv = n_pairs // SWA_KV_HEADS
    neg_inf = float("-inf")

    @pl.when((pl.program_id(0) == 0) & (pl.program_id(1) == 0))
    def _():
        bucket = bucket_ref[...]
        for h in range(SWA_HEADS):
            def body(b, acc):
                return jnp.where(bucket == b, relb_ref[b, h] * LOG2E, acc)
            bias_ref[h] = lax.fori_loop(0, REL_BUCKETS, body, jnp.full((L, 2 * L), neg_inf, F32))

    @pl.when(pl.program_id(1) == 0)
    def _():
        kv_prev_ref[...] = jnp.zeros_like(kv_prev_ref)

    x = x_ref[...]
    xg = (x * g_ref[...]).astype(BF16)
    inv_rms = jnp.broadcast_to(lax.rsqrt(jnp.mean(x * x, axis=-1, keepdims=True) + EPS),
                               (tm, LANES))

    def scaled_dot(w):
        res = _dot(xg, w)
        return jnp.concatenate([res[:, c:c + LANES] * inv_rms
                                for c in range(0, res.shape[1], LANES)], axis=1)

    def proj_cols(start, stop):
        return scaled_dot(wg_ref[:, start:stop])

    def proj(lo, hi):
        return proj_cols(_OFF[lo], _OFF[hi])

    swa = scaled_dot(ws_ref[...])
    sq = (swa[:, :SWA_QW] * (SWA_DH ** -0.5 * LOG2E)).astype(BF16)
    k_cat = jnp.concatenate([kv_prev_ref[0], swa[:, SWA_QW:SWA_QW + SWA_KVW]], axis=0)
    v_cat = jnp.concatenate([kv_prev_ref[1], swa[:, SWA_QW + SWA_KVW:]], axis=0)
    kv_prev_ref[0] = k_cat[tm:, :]
    kv_prev_ref[1] = v_cat[tm:, :]

    def dup_heads(cat):
        low = lax.broadcasted_iota(jnp.int32, cat.shape, 1) < SWA_DH
        rolled = pltpu.roll(cat, SWA_DH, 1)
        return low, (jnp.where(low, cat, rolled), jnp.where(low, rolled, cat))

    _, kdup = dup_heads(k_cat)
    low_kv, vdup = dup_heads(v_cat)
    kd = [kg.astype(BF16) for kg in kdup]
    vd = [[jnp.where(low_kv, vg, 1.0).astype(BF16), jnp.where(low_kv, 1.0, vg).astype(BF16)]
          for vg in vdup]

    low = lax.broadcasted_iota(jnp.int32, (L, pair_w), 1) < SWA_DH
    no_prev = jnp.where((pl.program_id(1) == 0)
                        & (lax.broadcasted_iota(jnp.int32, (L, 2 * L), 1) < L), neg_inf, 0.0)
    items = [(j, p) for j in range(n_blk) for p in range(n_pairs)]
    sinks = [sinks_ref[h] * LOG2E for h in range(SWA_HEADS)]
    row_max = {}

    def biased_scores(t, e):
        j, p = items[t]
        s = s_ref[t, e * L:(e + 1) * L, :] + bias_ref[2 * p + e]
        return s + no_prev if j == 0 else s

    def swa_score_dots(ts):
        for t in ts:
            j, p = items[t]
            qp = sq[j * L:(j + 1) * L, p * pair_w:(p + 1) * pair_w]
            zero = jnp.zeros_like(qp)
            q2 = jnp.concatenate([jnp.where(low, qp, zero), jnp.where(low, zero, qp)], axis=0)
            keys = kd[p // pairs_per_kv][j * L:(j + 2) * L, :]
            s_ref[t] = lax.dot_general(q2, keys, _NT, preferred_element_type=F32)

    def swa_row_max(ts):
        for t in ts:
            for e in range(2):
                s = biased_scores(t, e)
                s_ref[t, e * L:(e + 1) * L, :] = s
                row_max[t, e] = jnp.maximum(jnp.max(s, axis=-1, keepdims=True),
                                            sinks[2 * items[t][1] + e])

    def swa_exp(ts):
        for t in ts:
            for e in range(2):
                s = s_ref[t, e * L:(e + 1) * L, :]
                pr_ref[2 * t + e] = jnp.exp2(s - row_max[t, e]).astype(BF16)

    def swa_out(ts):
        for t in ts:
            j, p = items[t]
            vals = vd[p // pairs_per_kv]
            outs = [_dot(pr_ref[2 * t + e], vals[e][j * L:(j + 2) * L, :]) for e in range(2)]
            sink_terms = [jnp.exp2(sinks[2 * p + e] - row_max[t, e]) for e in range(2)]
            numer = jnp.where(low, outs[0], outs[1])
            denom = (pltpu.roll(jnp.where(low, outs[1], outs[0]), SWA_DH, 1)
                     + jnp.where(low, sink_terms[0], sink_terms[1]))
            o_ref[j * L:(j + 1) * L, p * pair_w:(p + 1) * pair_w] = (
                numer * (1.0 / denom)).astype(o_ref.dtype)

    a_low = scaled_dot(wa_ref[...])
    a_hi, a_lo = _split_bf16(a_low)
    w_hi, w_lo = _split_bf16(wal_ref[...])
    z = _dot(a_hi, w_hi) + _dot(a_lo, w_hi) + _dot(a_hi, w_lo) + bal_ref[...]
    la_ref[...] = -(jnp.maximum(-z, 0.0) + jnp.log1p(jnp.exp(-jnp.abs(z)))) * (1.0 / GLA_TAU)

    half_v = GLA_VW // 2

    def gla_piece(i):
        if i == 0:
            gq_ref[...] = proj(0, 1).astype(BF16)
        elif i == 1:
            gk_ref[...] = proj(1, 2).astype(BF16)
        else:
            ref = gv_ref if i < 4 else gr_ref
            base = _OFF[2] if i < 4 else _OFF[3]
            lo = (i % 2) * half_v
            ref[:, lo:lo + half_v] = proj_cols(base + lo, base + lo + half_v).astype(BF16)

    n_groups = SWA_ITEM_GROUPS
    per_group = len(items) // n_groups
    groups = [range(g * per_group, (g + 1) * per_group) for g in range(n_groups)]
    phases = (swa_score_dots, swa_row_max, swa_exp, swa_out)
    n_pieces = 6
    for stage in range(n_groups + len(phases) - 1):
        if stage < n_pieces:
            gla_piece(stage)
        for depth, phase in enumerate(phases):
            if 0 <= stage - depth < n_groups:
                phase(groups[stage - depth])
    for piece in range(n_groups + len(phases) - 1, n_pieces):
        gla_piece(piece)


def _in_proj_swa(h, g, w_parts, wal_pad, bal, rel_bias, sinks, cast_weights):
    B, T, D = h.shape
    tm = TOK_TILE
    L = WINDOW
    per = tm // L

    def tok(width):
        return pl.BlockSpec((None, tm, width), lambda b, i: (b, i, 0))

    def out(width, dt):
        return jax.ShapeDtypeStruct((B, T, width), dt)

    smem = pl.BlockSpec(memory_space=pltpu.SMEM)
    grid = (B, T // tm)
    cast_specs, cast_shapes = _cast_specs(cast_weights, grid)
    outs = pl.pallas_call(
        functools.partial(_inproj_swa_kernel, n_cast=len(cast_weights)),
        grid=grid,
        in_specs=[smem, smem, _const_spec((L, 2 * L)),
                  tok(D), _const_spec((1, D)), _const_spec((D, _OFF[-1])),
                  _const_spec((D, RANK_PAD)), _const_spec((D, D_SWA)),
                  _const_spec((RANK_PAD, GLA_KW)), _const_spec((1, GLA_KW))] + cast_specs,
        out_specs=[tok(GLA_KW), tok(GLA_KW), tok(GLA_VW), tok(GLA_VW), tok(GLA_KW), tok(SWA_QW)]
        + cast_specs,
        out_shape=[out(GLA_KW, BF16), out(GLA_KW, BF16), out(GLA_VW, BF16), out(GLA_VW, BF16),
                   out(GLA_KW, F32), out(SWA_QW, BF16)] + cast_shapes,
        scratch_shapes=[pltpu.VMEM((SWA_HEADS, L, 2 * L), F32),
                        pltpu.VMEM((2, L, SWA_KVW), F32),
                        pltpu.VMEM((per * SWA_HEADS // 2, 2 * L, 2 * L), F32),
                        pltpu.VMEM((per * SWA_HEADS, L, 2 * L), BF16)],
        compiler_params=_params(("arbitrary", "arbitrary")),
        name="in_proj_swa",
    )(sinks, rel_bias, _t5_bucket_table(), h, g, *w_parts, wal_pad, bal, *cast_weights)
    return outs[:6], outs[6:]


def _gla_kernel(q_ref, k_ref, v_ref, r_ref, la_ref, gain_ref, *refs, n_cast):
    cast_src, o_ref, refs = refs[:n_cast], refs[n_cast], refs[n_cast + 1:]
    cast_dst, (st_ref, q4_ref, kv_ref, dec_ref, oi_ref) = refs[:n_cast], refs[n_cast:]
    _cast_blocks(cast_src, cast_dst)

    C = GLA_CHUNK
    H = GLA_HEADS
    n_chunks = q_ref.shape[0] // C

    @pl.when(pl.program_id(1) == 0)
    def _():
        st_ref[...] = jnp.zeros_like(st_ref)

    tril = (lax.broadcasted_iota(jnp.int32, (C, C), 0)
            >= lax.broadcasted_iota(jnp.int32, (C, C), 1)).astype(BF16)
    tril2 = jnp.concatenate([tril, tril], axis=1)
    causal4 = ((lax.broadcasted_iota(jnp.int32, (H * C, C), 0) & (C - 1))
               >= lax.broadcasted_iota(jnp.int32, (H * C, C), 1))
    head_of_lane = lax.broadcasted_iota(jnp.int32, (C, GLA_KW), 1) // GLA_DK

    def head_rows(x):
        return jnp.concatenate([jnp.where(head_of_lane == h, x, jnp.zeros_like(x))
                                for h in range(H)], axis=0)
    gain = gain_ref[...]

    rows_of = [slice(c * C, (c + 1) * C) for c in range(n_chunks)]

    k_inv, k_end = {}, {}

    def decays(chunks):
        bcum = {}
        for c in chunks:
            la_hi, la_lo = _split_bf16(la_ref[rows_of[c], :])
            bcum[c] = _dot(tril2, jnp.concatenate([la_hi, la_lo], axis=0))

        for c in chunks:
            b_last = bcum[c][C - 1:C, :]
            dec_ref[c] = jnp.exp(b_last)
            q = q_ref[rows_of[c], :].astype(F32) * (GLA_DK ** -0.5)
            k = k_ref[rows_of[c], :].astype(F32)
            q_dec = (q * jnp.exp(bcum[c])).astype(BF16)
            k_inv[c] = (k * jnp.exp(-bcum[c])).astype(BF16)
            k_end[c] = head_rows((k * jnp.exp(b_last - bcum[c])).astype(BF16))
            q4_ref[c] = head_rows(q_dec)

    def chunk_dots(chunks):
        att = {c: lax.dot_general(q4_ref[c], k_inv[c], _NT, preferred_element_type=F32)
               for c in chunks}
        for c in chunks:
            v_rows = jnp.concatenate([v_ref[rows_of[c], h * GLA_DV:(h + 1) * GLA_DV]
                                      for h in range(H)], axis=0)
            kv_ref[c] = lax.dot_general(v_rows, k_end[c], _TN, preferred_element_type=F32)

        for c in chunks:
            att_c = jnp.where(causal4, att[c], 0.0).astype(BF16)
            for h in range(H):
                cols = slice(h * GLA_DV, (h + 1) * GLA_DV)
                oi_ref[rows_of[c], cols] = _dot(att_c[h * C:(h + 1) * C, :],
                                                v_ref[rows_of[c], cols])

    def recurrence(chunks, st):
        inter = {}
        for c in chunks:
            inter[c] = lax.dot_general(q4_ref[c], st.astype(BF16), _NT,
                                       preferred_element_type=F32)
            st = st * dec_ref[c] + kv_ref[c]
        return inter, st

    def finish(chunks, inter):
        for c in chunks:
            for h in range(H):
                cols = slice(h * GLA_DV, (h + 1) * GLA_DV)
                o = oi_ref[rows_of[c], cols] + inter[c][h * C:(h + 1) * C, :]
                o = o * lax.rsqrt(jnp.mean(o * o, axis=-1, keepdims=True) + EPS) * gain
                o_ref[rows_of[c], cols] = (
                    o * _silu(r_ref[rows_of[c], cols].astype(F32))).astype(o_ref.dtype)

    per_group = GLA_GROUP_CHUNKS
    groups = [range(g, min(g + per_group, n_chunks)) for g in range(0, n_chunks, per_group)]
    st = st_ref[...]
    inter_prev = None
    decays(groups[0])
    for gi, group in enumerate(groups):
        chunk_dots(group)
        if inter_prev is not None:
            finish(groups[gi - 1], inter_prev)
        if gi + 1 < len(groups):
            decays(groups[gi + 1])
        inter_prev, st = recurrence(group, st)
    st_ref[...] = st
    finish(groups[-1], inter_prev)


def _gla(gq, gk, gv, gr, la, gain, cast_weights):
    B, T, _ = gq.shape
    tm = GLA_TILE

    def tok(width):
        return pl.BlockSpec((None, tm, width), lambda b, i: (b, i, 0))

    grid = (B, T // tm)
    cast_specs, cast_shapes = _cast_specs(cast_weights, grid)
    outs = pl.pallas_call(
        functools.partial(_gla_kernel, n_cast=len(cast_weights)),
        grid=grid,
        in_specs=[tok(GLA_KW), tok(GLA_KW), tok(GLA_VW), tok(GLA_VW), tok(GLA_KW),
                  _const_spec((1, GLA_DV))] + cast_specs,
        out_specs=[tok(GLA_VW)] + cast_specs,
        out_shape=[jax.ShapeDtypeStruct((B, T, GLA_VW), BF16)] + cast_shapes,
        scratch_shapes=[pltpu.VMEM((GLA_DV, GLA_KW), F32),
                        pltpu.VMEM((tm // GLA_CHUNK, GLA_HEADS * GLA_CHUNK, GLA_KW), BF16),
                        pltpu.VMEM((tm // GLA_CHUNK, GLA_DV, GLA_KW), F32),
                        pltpu.VMEM((tm // GLA_CHUNK, 1, GLA_KW), F32),
                        pltpu.VMEM((tm, GLA_VW), F32)],
        compiler_params=_params(("parallel", "arbitrary")),
        name="gla",
    )(gq, gk, gv, gr, la, gain, *cast_weights)
    return outs[0], outs[1:]


def _memkv_kernel(mem_ref, g_ref, wk_ref, wv_ref, k_ref, v_ref):
    mn = _rms(mem_ref[...], g_ref[...]).astype(BF16)
    k_ref[...] = _dot(mn, wk_ref[...]).astype(BF16)
    v_ref[...] = _dot(mn, wv_ref[...]).astype(BF16)


def _mem_kv(mem, g, wk, wv):
    B, M, D = mem.shape
    rows = B * M
    tm = min(TOK_TILE, rows)
    blk = pl.BlockSpec((tm, D), lambda i: (i, 0))
    k, v = pl.pallas_call(
        _memkv_kernel,
        grid=(rows // tm,),
        in_specs=[blk, _const_spec((1, D)), _const_spec((D, D)), _const_spec((D, D))],
        out_specs=[blk, blk],
        out_shape=[jax.ShapeDtypeStruct((rows, D), BF16)] * 2,
        compiler_params=_params(("parallel",)),
        name="mem_kv",
    )(mem.reshape(rows, D), g, wk, wv)
    return k.reshape(B, M, D), v.reshape(B, M, D)


def _outcross_kernel(h_ref, og_ref, os_ref, kc_ref, vc_ref, wout_ref, g_ref,
                     wq_ref, wo_ref, o_ref):
    mix = jnp.concatenate([og_ref[...], os_ref[...]], axis=1)
    h1 = h_ref[...] + _dot(mix, wout_ref[...])
    inv_rms = lax.rsqrt(jnp.mean(h1 * h1, axis=-1, keepdims=True) + EPS)
    hg = (h1 * g_ref[...]).astype(BF16)
    q = (_dot(hg, wq_ref[...]) * (inv_rms * (CROSS_DH ** -0.5 * LOG2E))).astype(BF16)
    tm = h1.shape[0]
    blk = tm // CROSS_ROW_BLOCKS
    items = [(slice(rb * blk, (rb + 1) * blk), slice(hd * CROSS_DH, (hd + 1) * CROSS_DH))
             for rb in range(CROSS_ROW_BLOCKS) for hd in range(CROSS_HEADS)]
    scores, probs, attn = {}, {}, {}
    for stage in range(len(items) + 2):
        if stage < len(items):
            rows, c = items[stage]
            scores[stage] = lax.dot_general(q[rows, c], kc_ref[:, c], _NT,
                                            preferred_element_type=F32)
        if 0 <= stage - 1 < len(items):
            s = scores.pop(stage - 1)
            p = jnp.exp2(s - jnp.max(s, axis=-1, keepdims=True))
            probs[stage - 1] = (p.astype(BF16), 1.0 / jnp.sum(p, axis=-1, keepdims=True))
        if 0 <= stage - 2 < len(items):
            i = stage - 2
            rows, c = items[i]
            p, inv_denom = probs.pop(i)
            o = (_dot(p, vc_ref[:, c]) * inv_denom).astype(BF16)
            part = _dot(o, wo_ref[c, :])
            rb = i // CROSS_HEADS
            attn[rb] = part if i % CROSS_HEADS == 0 else attn[rb] + part
            if i % CROSS_HEADS == CROSS_HEADS - 1:
                o_ref[rows, :] = h1[rows, :] + attn.pop(rb)


def _out_cross(h, og, osw, kc, vc, wout, g, wq, wo):
    B, T, D = h.shape
    tm = TOK_TILE

    def tok(width):
        return pl.BlockSpec((None, tm, width), lambda b, i: (b, i, 0))

    memblk = pl.BlockSpec((None, MEM_LEN, D), lambda b, i: (b, 0, 0))
    return pl.pallas_call(
        _outcross_kernel,
        grid=(B, T // tm),
        in_specs=[tok(D), tok(GLA_VW), tok(SWA_QW), memblk, memblk,
                  _const_spec((GLA_VW + SWA_QW, D)), _const_spec((1, D)),
                  _const_spec((D, D)), _const_spec((D, D))],
        out_specs=tok(D),
        out_shape=jax.ShapeDtypeStruct((B, T, D), F32),
        compiler_params=_params(("parallel", "parallel")),
        name="out_cross",
    )(h, og, osw, kc, vc, wout, g, wq, wo)


def _ffn_kernel(h_ref, g_ref, wu_ref, wd_ref, cw_ref, cb_ref, gf_ref,
                o_ref, hn_ref, acc_ref, ubuf_ref, carry_ref, *, final_norm):
    tm = h_ref.shape[0]
    n_slab = 2 * FFN_CHUNK // LANES
    half = n_slab // 2

    @pl.when(pl.program_id(1) == 0)
    def _():
        carry_ref[...] = jnp.zeros_like(carry_ref)

    hn_ref[...] = _rms(h_ref[...], g_ref[...]).astype(BF16)

    def cols_of(c, s):
        start = (s // half) * D_FF + c * FFN_CHUNK + (s % half) * LANES
        return slice(start, start + LANES)

    def up(c):
        hn = hn_ref[...]
        for part in range(2):
            start = part * D_FF + c * FFN_CHUNK
            u = _dot(hn, wu_ref[:, start:start + FFN_CHUNK])
            for j in range(half):
                s = part * half + j
                us = u[:, j * LANES:(j + 1) * LANES]
                ubuf_ref[c % FFN_UBUFS, s, 0:CARRY_ROWS, :] = carry_ref[c, s]
                ubuf_ref[c % FFN_UBUFS, s, CARRY_ROWS:, :] = us
                carry_ref[c, s] = us[tm - CARRY_ROWS:, :]

    def conv(c, s, rows):
        lanes = cols_of(c, s)
        taps = [ubuf_ref[c % FFN_UBUFS, s,
                         CARRY_ROWS - d + rows.start:CARRY_ROWS - d + rows.stop, :]
                for d in range(CONV_WIDTH)]
        return (cw_ref[2:3, lanes] * taps[0]
                + (cw_ref[1:2, lanes] * taps[1] + (cw_ref[0:1, lanes] * taps[2] + cb_ref[:, lanes])))

    lookahead = FFN_UBUFS - 1
    for c in range(lookahead):
        up(c)
    last = N_FFN_CHUNKS - 1

    def down(first_chunk, last_chunk, rows, act):
        res = _dot(act, wd_ref[first_chunk * FFN_CHUNK:(last_chunk + 1) * FFN_CHUNK, :])
        base = h_ref[rows, :] if first_chunk == 0 else acc_ref[rows, :]
        if last_chunk < last:
            acc_ref[rows, :] = base + res
        else:
            out = base + res
            if final_norm:
                out = _rms(out, gf_ref[...])
            o_ref[rows, :] = out

    row_blocks = [slice(r, r + tm // FFN_ROW_BLOCKS) for r in range(0, tm, tm // FFN_ROW_BLOCKS)]
    pending = []
    for c in range(N_FFN_CHUNKS):
        if c + lookahead <= last:
            up(c + lookahead)
        pending.append([[(_silu(conv(c, s, rows)) * conv(c, half + s, rows)).astype(BF16)
                         for s in range(half)] for rows in row_blocks])
        if len(pending) == FFN_DOWN_GROUP or c == last:
            for b, rows in enumerate(row_blocks):
                act = jnp.concatenate([a for chunk_acts in pending for a in chunk_acts[b]], axis=1)
                down(c + 1 - len(pending), c, rows, act)
            pending = []


def _ffn(h, g, wu, wd, cw, cb, gf, final_norm):
    B, T, D = h.shape
    tm = FFN_TILE
    nc, fc = N_FFN_CHUNKS, FFN_CHUNK
    n_slab = 2 * fc // LANES
    tok = pl.BlockSpec((None, tm, D), lambda b, i: (b, i, 0))

    def once(shape):
        zeros = (0,) * len(shape)
        return pl.BlockSpec(shape, lambda *_: zeros, pipeline_mode=pl.Buffered(1))

    return pl.pallas_call(
        functools.partial(_ffn_kernel, final_norm=final_norm),
        grid=(B, T // tm),
        in_specs=[tok, _const_spec((1, D)), once((D, 2 * D_FF)), once((D_FF, D)),
                  _const_spec((CONV_WIDTH, 2 * D_FF)), _const_spec((1, 2 * D_FF)),
                  _const_spec((1, D))],
        out_specs=tok,
        out_shape=jax.ShapeDtypeStruct((B, T, D), F32),
        scratch_shapes=[pltpu.VMEM((tm, D), BF16), pltpu.VMEM((tm, D), F32),
                        pltpu.VMEM((FFN_UBUFS, n_slab, CARRY_ROWS + tm, LANES), F32),
                        pltpu.VMEM((nc, n_slab, CARRY_ROWS, LANES), F32)],
        compiler_params=_params(("parallel", "arbitrary")),
        name="ffn",
    )(h, g, wu, wd, cw, cb, gf)


def _split_in_proj(w_in):
    w_rank = jnp.pad(w_in[:, _IN_OFF[4]:_IN_OFF[5]], ((0, 0), (0, RANK_PAD - GLA_RANK)))
    return (w_in[:, :_IN_OFF[4]].astype(BF16), w_rank.astype(BF16),
            w_in[:, _IN_OFF[5]:].astype(BF16))


def kernel(x, mem, norm_mix, w_in, w_alpha, b_alpha, gla_gain, rel_bias, sinks, w_out, norm_cross,
           norm_mem, w_q_c, w_k_c, w_v_c, w_o_c, norm_ffn, w_up, conv_w, conv_b, w_down, norm_final):
    depth = w_in.shape[0]
    row = lambda v: v.reshape(1, -1).astype(F32)
    h = x
    for l in range(depth):
        wal_pad = jnp.pad(w_alpha[l], ((0, RANK_PAD - GLA_RANK), (0, 0)))
        (gq, gk, gv, gr, la, o_swa), (wout, wq, wo, wk, wv) = _in_proj_swa(
            h, row(norm_mix[l]), _split_in_proj(w_in[l]), wal_pad, row(b_alpha[l]),
            rel_bias.astype(F32), sinks[l].astype(F32),
            cast_weights=[w_out[l], w_q_c[l], w_o_c[l], w_k_c[l], w_v_c[l]])
        o_gla, (wup, wdown) = _gla(gq, gk, gv, gr, la, row(gla_gain[l]),
                                   cast_weights=[w_up[l], w_down[l]])
        kc, vc = _mem_kv(mem, row(norm_mem[l]), wk, wv)
        h = _out_cross(h, o_gla, o_swa, kc, vc, wout, row(norm_cross[l]), wq, wo)
        h = _ffn(h, row(norm_ffn[l]), wup, wdown, conv_w[l].astype(F32), row(conv_b[l]),
                 row(norm_final), final_norm=(l == depth - 1))
    return h
```

```python
import functools
import math

import numpy as np
import jax
import jax.numpy as jnp
from jax import lax
from jax.experimental import pallas as pl
from jax.experimental.pallas import tpu as pltpu

F32 = jnp.float32
BF16 = jnp.bfloat16

D_MODEL = 1024
MEM_LEN = 256
EPS = 1e-6
GLA_HEADS = 4
GLA_DK = 64
GLA_DV = 128
GLA_RANK = 16
GLA_TAU = 16.0
GLA_CHUNK = 64
SWA_HEADS = 8
SWA_KV_HEADS = 2
SWA_DH = 64
WINDOW = 128
REL_BUCKETS = 32
REL_MAX_DIST = 128
CROSS_HEADS = 4
CROSS_DH = D_MODEL // CROSS_HEADS
D_FF = 2816
CONV_WIDTH = 3

GLA_KW = GLA_HEADS * GLA_DK
GLA_VW = GLA_HEADS * GLA_DV
SWA_QW = SWA_HEADS * SWA_DH
SWA_KVW = SWA_KV_HEADS * SWA_DH
IN_SPLITS = (GLA_KW, GLA_KW, GLA_VW, GLA_VW, GLA_RANK, SWA_QW, SWA_KVW, SWA_KVW)

LOG2E = math.log2(math.e)
LANES = 128
BF16_SUBLANES = 16
RANK_PAD = LANES
_IN_OFF = tuple(int(v) for v in np.cumsum((0,) + IN_SPLITS))
_OFF = _IN_OFF[:5]
D_SWA = SWA_QW + 2 * SWA_KVW

F32_SUBLANES = 8
V7X_VMEM_BYTES = 64 * 1024 * 1024
VMEM_LIMIT = V7X_VMEM_BYTES * 7 // 8

TOK_TILE = 1024
FFN_TILE = 512
GLA_TILE = 2048
CROSS_ROW_BLOCKS = 2
SWA_ITEM_GROUPS = 4
GLA_GROUP_CHUNKS = 4
FFN_CHUNK = 256
N_FFN_CHUNKS = D_FF // FFN_CHUNK
FFN_DOWN_GROUP = 2
FFN_ROW_BLOCKS = 2
FFN_UBUFS = 4
CARRY_ROWS = F32_SUBLANES

_NT = (((1,), (1,)), ((), ()))
_TN = (((0,), (0,)), ((), ()))


def _rms(x, g):
    return x * lax.rsqrt(jnp.mean(x * x, axis=-1, keepdims=True) + EPS) * g


def _dot(a, b):
    return jnp.dot(a, b, preferred_element_type=F32)


def _split_bf16(x):
    hi = x.astype(BF16)
    return hi, (x - hi.astype(F32)).astype(BF16)


def _silu(x):
    return x * (1.0 / (1.0 + jnp.exp2(x * -LOG2E)))


def _const_spec(shape):
    zeros = (0,) * len(shape)
    return pl.BlockSpec(shape, lambda *_: zeros)


def _params(sem):
    return pltpu.CompilerParams(dimension_semantics=sem, vmem_limit_bytes=VMEM_LIMIT)


def _cast_specs(weights, grid):
    n_steps = grid[0] * grid[1]
    specs, shapes = [], []
    for w in weights:
        rows = w.shape[0] // n_steps
        assert rows * n_steps == w.shape[0] and rows % BF16_SUBLANES == 0, w.shape
        specs.append(pl.BlockSpec((rows, w.shape[1]), lambda b, i: (b * grid[1] + i, 0)))
        shapes.append(jax.ShapeDtypeStruct(w.shape, BF16))
    return specs, shapes


def _cast_blocks(src_refs, dst_refs):
    for src, dst in zip(src_refs, dst_refs):
        dst[...] = src[...].astype(BF16)


def _t5_bucket_table():
    L = WINDOW
    dist = (jnp.arange(L)[:, None] + L) - jnp.arange(2 * L)[None, :]
    n = jnp.maximum(dist, 0)
    max_exact = REL_BUCKETS // 2
    nf = jnp.maximum(n, 1).astype(F32)
    large = max_exact + (jnp.log(nf / max_exact) / math.log(REL_MAX_DIST / max_exact)
                         * (REL_BUCKETS - max_exact)).astype(jnp.int32)
    large = jnp.minimum(large, REL_BUCKETS - 1)
    bucket = jnp.where(n < max_exact, n, large)
    return jnp.where((dist >= 0) & (dist < WINDOW), bucket, -1).astype(jnp.int32)


def _inproj_swa_kernel(sinks_ref, relb_ref, bucket_ref, x_ref, g_ref, wg_ref, wa_ref, ws_ref,
                       wal_ref, bal_ref, *refs, n_cast):
    cast_src, refs = refs[:n_cast], refs[n_cast:]
    gq_ref, gk_ref, gv_ref, gr_ref, la_ref, o_ref = refs[:6]
    cast_dst, (bias_ref, kv_prev_ref, s_ref, pr_ref) = refs[6:6 + n_cast], refs[6 + n_cast:]
    _cast_blocks(cast_src, cast_dst)

    L = WINDOW
    tm = x_ref.shape[0]
    n_blk = tm // L
    pair_w = 2 * SWA_DH
    n_pairs = SWA_QW // pair_w
    pairs_per_kv = n_pairs // SWA_KV_HEADS
    neg_inf = float("-inf")

    @pl.when((pl.program_id(0) == 0) & (pl.program_id(1) == 0))
    def _():
        bucket = bucket_ref[...]
        for h in range(SWA_HEADS):
            def body(b, acc):
                return jnp.where(bucket == b, relb_ref[b, h] * LOG2E, acc)
            bias_ref[h] = lax.fori_loop(0, REL_BUCKETS, body, jnp.full((L, 2 * L), neg_inf, F32))

    @pl.when(pl.program_id(1) == 0)
    def _():
        kv_prev_ref[...] = jnp.zeros_like(kv_prev_ref)

    x = x_ref[...]
    xg = (x * g_ref[...]).astype(BF16)
    inv_rms = jnp.broadcast_to(lax.rsqrt(jnp.mean(x * x, axis=-1, keepdims=True) + EPS),
                               (tm, LANES))

    def scaled_dot(w):
        res = _dot(xg, w)
        return jnp.concatenate([res[:, c:c + LANES] * inv_rms
                                for c in range(0, res.shape[1], LANES)], axis=1)

    def proj_cols(start, stop):
        return scaled_dot(wg_ref[:, start:stop])

    def proj(lo, hi):
        return proj_cols(_OFF[lo], _OFF[hi])

    swa = scaled_dot(ws_ref[...])
    sq = (swa[:, :SWA_QW] * (SWA_DH ** -0.5 * LOG2E)).astype(BF16)
    k_cat = jnp.concatenate([kv_prev_ref[0], swa[:, SWA_QW:SWA_QW + SWA_KVW]], axis=0)
    v_cat = jnp.concatenate([kv_prev_ref[1], swa[:, SWA_QW + SWA_KVW:]], axis=0)
    kv_prev_ref[0] = k_cat[tm:, :]
    kv_prev_ref[1] = v_cat[tm:, :]

    def dup_heads(cat):
        low = lax.broadcasted_iota(jnp.int32, cat.shape, 1) < SWA_DH
        rolled = pltpu.roll(cat, SWA_DH, 1)
        return low, (jnp.where(low, cat, rolled), jnp.where(low, rolled, cat))

    _, kdup = dup_heads(k_cat)
    low_kv, vdup = dup_heads(v_cat)
    kd = [kg.astype(BF16) for kg in kdup]
    vd = [[jnp.where(low_kv, vg, 1.0).astype(BF16), jnp.where(low_kv, 1.0, vg).astype(BF16)]
          for vg in vdup]

    low = lax.broadcasted_iota(jnp.int32, (L, pair_w), 1) < SWA_DH
    no_prev = jnp.where((pl.program_id(1) == 0)
                        & (lax.broadcasted_iota(jnp.int32, (L, 2 * L), 1) < L), neg_inf, 0.0)
    items = [(j, p) for j in range(n_blk) for p in range(n_pairs)]
    sinks = [sinks_ref[h] * LOG2E for h in range(SWA_HEADS)]
    row_max = {}

    def biased_scores(t, e):
        j, p = items[t]
        s = s_ref[t, e * L:(e + 1) * L, :] + bias_ref[2 * p + e]
        return s + no_prev if j == 0 else s

    def swa_score_dots(ts):
        for t in ts:
            j, p = items[t]
            qp = sq[j * L:(j + 1) * L, p * pair_w:(p + 1) * pair_w]
            zero = jnp.zeros_like(qp)
            q2 = jnp.concatenate([jnp.where(low, qp, zero), jnp.where(low, zero, qp)], axis=0)
            keys = kd[p // pairs_per_kv][j * L:(j + 2) * L, :]
            s_ref[t] = lax.dot_general(q2, keys, _NT, preferred_element_type=F32)

    def swa_row_max(ts):
        for t in ts:
            for e in range(2):
                s = biased_scores(t, e)
                s_ref[t, e * L:(e + 1) * L, :] = s
                row_max[t, e] = jnp.maximum(jnp.max(s, axis=-1, keepdims=True),
                                            sinks[2 * items[t][1] + e])

    def swa_exp(ts):
        for t in ts:
            for e in range(2):
                s = s_ref[t, e * L:(e + 1) * L, :]
                pr_ref[2 * t + e] = jnp.exp2(s - row_max[t, e]).astype(BF16)

    def swa_out(ts):
        for t in ts:
            j, p = items[t]
            vals = vd[p // pairs_per_kv]
            outs = [_dot(pr_ref[2 * t + e], vals[e][j * L:(j + 2) * L, :]) for e in range(2)]
            sink_terms = [jnp.exp2(sinks[2 * p + e] - row_max[t, e]) for e in range(2)]
            numer = jnp.where(low, outs[0], outs[1])
            denom = (pltpu.roll(jnp.where(low, outs[1], outs[0]), SWA_DH, 1)
                     + jnp.where(low, sink_terms[0], sink_terms[1]))
            o_ref[j * L:(j + 1) * L, p * pair_w:(p + 1) * pair_w] = (
                numer * (1.0 / denom)).astype(o_ref.dtype)

    a_low = scaled_dot(wa_ref[...])
    a_hi, a_lo = _split_bf16(a_low)
    w_hi, w_lo = _split_bf16(wal_ref[...])
    z = _dot(a_hi, w_hi) + _dot(a_lo, w_hi) + _dot(a_hi, w_lo) + bal_ref[...]
    log_a = -(jnp.maximum(-z, 0.0) + jnp.log1p(jnp.exp(-jnp.abs(z)))) * (1.0 / GLA_TAU)
    la_ref[...] = jnp.concatenate(_split_bf16(log_a), axis=1)

    half_v = GLA_VW // 2

    def gla_piece(i):
        if i == 0:
            gq_ref[...] = (proj(0, 1) * (GLA_DK ** -0.5)).astype(BF16)
        elif i == 1:
            gk_ref[...] = proj(1, 2).astype(BF16)
        else:
            lo = (i % 2) * half_v
            if i < 4:
                gv_ref[:, lo:lo + half_v] = proj_cols(_OFF[2] + lo,
                                                      _OFF[2] + lo + half_v).astype(BF16)
            else:
                r = proj_cols(_OFF[3] + lo, _OFF[3] + lo + half_v)
                gr_ref[:, lo:lo + half_v] = _silu(r).astype(BF16)

    n_groups = SWA_ITEM_GROUPS
    per_group = len(items) // n_groups
    groups = [range(g * per_group, (g + 1) * per_group) for g in range(n_groups)]
    phases = (swa_score_dots, swa_row_max, swa_exp, swa_out)
    n_pieces = 6
    for stage in range(n_groups + len(phases) - 1):
        if stage < n_pieces:
            gla_piece(stage)
        for depth, phase in enumerate(phases):
            if 0 <= stage - depth < n_groups:
                phase(groups[stage - depth])
    for piece in range(n_groups + len(phases) - 1, n_pieces):
        gla_piece(piece)


def _in_proj_swa(h, g, w_parts, wal_pad, bal, rel_bias, sinks, cast_weights):
    B, T, D = h.shape
    tm = TOK_TILE
    L = WINDOW
    per = tm // L

    def tok(width):
        return pl.BlockSpec((None, tm, width), lambda b, i: (b, i, 0))

    def out(width, dt):
        return jax.ShapeDtypeStruct((B, T, width), dt)

    smem = pl.BlockSpec(memory_space=pltpu.SMEM)
    grid = (B, T // tm)
    cast_specs, cast_shapes = _cast_specs(cast_weights, grid)
    outs = pl.pallas_call(
        functools.partial(_inproj_swa_kernel, n_cast=len(cast_weights)),
        grid=grid,
        in_specs=[smem, smem, _const_spec((L, 2 * L)),
                  tok(D), _const_spec((1, D)), _const_spec((D, _OFF[-1])),
                  _const_spec((D, RANK_PAD)), _const_spec((D, D_SWA)),
                  _const_spec((RANK_PAD, GLA_KW)), _const_spec((1, GLA_KW))] + cast_specs,
        out_specs=[tok(GLA_KW), tok(GLA_KW), tok(GLA_VW), tok(GLA_VW), tok(2 * GLA_KW), tok(SWA_QW)]
        + cast_specs,
        out_shape=[out(GLA_KW, BF16), out(GLA_KW, BF16), out(GLA_VW, BF16), out(GLA_VW, BF16),
                   out(2 * GLA_KW, BF16), out(SWA_QW, BF16)] + cast_shapes,
        scratch_shapes=[pltpu.VMEM((SWA_HEADS, L, 2 * L), F32),
                        pltpu.VMEM((2, L, SWA_KVW), F32),
                        pltpu.VMEM((per * SWA_HEADS // 2, 2 * L, 2 * L), F32),
                        pltpu.VMEM((per * SWA_HEADS, L, 2 * L), BF16)],
        compiler_params=_params(("arbitrary", "arbitrary")),
        name="in_proj_swa",
    )(sinks, rel_bias, _t5_bucket_table(), h, g, *w_parts, wal_pad, bal, *cast_weights)
    return outs[:6], outs[6:]


def _gla_kernel(q_ref, k_ref, v_ref, r_ref, la_ref, gain_ref, *refs, n_cast):
    cast_src, o_ref, refs = refs[:n_cast], refs[n_cast], refs[n_cast + 1:]
    cast_dst, (st_ref, q4_ref, kv_ref, dec_ref, oi_ref) = refs[:n_cast], refs[n_cast:]
    _cast_blocks(cast_src, cast_dst)

    C = GLA_CHUNK
    H = GLA_HEADS
    n_chunks = q_ref.shape[0] // C

    @pl.when(pl.program_id(1) == 0)
    def _():
        st_ref[...] = jnp.zeros_like(st_ref)

    tril = (lax.broadcasted_iota(jnp.int32, (C, C), 0)
            >= lax.broadcasted_iota(jnp.int32, (C, C), 1)).astype(BF16)
    tril2 = jnp.concatenate([tril, tril], axis=1)
    causal4 = ((lax.broadcasted_iota(jnp.int32, (H * C, C), 0) & (C - 1))
               >= lax.broadcasted_iota(jnp.int32, (H * C, C), 1))
    head_of_lane = lax.broadcasted_iota(jnp.int32, (C, GLA_KW), 1) // GLA_DK

    def head_rows(x):
        return jnp.concatenate([jnp.where(head_of_lane == h, x, jnp.zeros_like(x))
                                for h in range(H)], axis=0)
    gain = gain_ref[...]

    rows_of = [slice(c * C, (c + 1) * C) for c in range(n_chunks)]

    k_inv, k_end = {}, {}

    def decays(chunks):
        bcum = {}
        for c in chunks:
            la_parts = la_ref[rows_of[c], :]
            bcum[c] = _dot(tril2, jnp.concatenate([la_parts[:, :GLA_KW], la_parts[:, GLA_KW:]],
                                                  axis=0))

        for c in chunks:
            b_last = bcum[c][C - 1:C, :]
            dec_ref[c] = jnp.exp(b_last)
            q = q_ref[rows_of[c], :].astype(F32)
            k = k_ref[rows_of[c], :].astype(F32)
            q_dec = (q * jnp.exp(bcum[c])).astype(BF16)
            k_inv[c] = (k * jnp.exp(-bcum[c])).astype(BF16)
            k_end[c] = head_rows((k * jnp.exp(b_last - bcum[c])).astype(BF16))
            q4_ref[c] = head_rows(q_dec)

    def chunk_dots(chunks):
        att = {c: lax.dot_general(q4_ref[c], k_inv[c], _NT, preferred_element_type=F32)
               for c in chunks}
        for c in chunks:
            v_rows = jnp.concatenate([v_ref[rows_of[c], h * GLA_DV:(h + 1) * GLA_DV]
                                      for h in range(H)], axis=0)
            kv_ref[c] = lax.dot_general(v_rows, k_end[c], _TN, preferred_element_type=F32)

        for c in chunks:
            att_c = jnp.where(causal4, att[c], 0.0).astype(BF16)
            for h in range(H):
                cols = slice(h * GLA_DV, (h + 1) * GLA_DV)
                oi_ref[rows_of[c], cols] = _dot(att_c[h * C:(h + 1) * C, :],
                                                v_ref[rows_of[c], cols])

    def recurrence(chunks, st):
        inter = {}
        for c in chunks:
            inter[c] = lax.dot_general(q4_ref[c], st.astype(BF16), _NT,
                                       preferred_element_type=F32)
            st = st * dec_ref[c] + kv_ref[c]
        return inter, st

    def finish(chunks, inter):
        for c in chunks:
            for h in range(H):
                cols = slice(h * GLA_DV, (h + 1) * GLA_DV)
                o = oi_ref[rows_of[c], cols] + inter[c][h * C:(h + 1) * C, :]
                o = o * lax.rsqrt(jnp.mean(o * o, axis=-1, keepdims=True) + EPS) * gain
                o_ref[rows_of[c], cols] = (
                    o * r_ref[rows_of[c], cols].astype(F32)).astype(o_ref.dtype)

    per_group = GLA_GROUP_CHUNKS
    groups = [range(g, min(g + per_group, n_chunks)) for g in range(0, n_chunks, per_group)]
    st = st_ref[...]
    inter_prev = None
    decays(groups[0])
    for gi, group in enumerate(groups):
        chunk_dots(group)
        if inter_prev is not None:
            finish(groups[gi - 1], inter_prev)
        if gi + 1 < len(groups):
            decays(groups[gi + 1])
        inter_prev, st = recurrence(group, st)
    st_ref[...] = st
    finish(groups[-1], inter_prev)


def _gla(gq, gk, gv, gr, la, gain, cast_weights):
    B, T, _ = gq.shape
    tm = GLA_TILE

    def tok(width):
        return pl.BlockSpec((None, tm, width), lambda b, i: (b, i, 0))

    grid = (B, T // tm)
    cast_specs, cast_shapes = _cast_specs(cast_weights, grid)
    outs = pl.pallas_call(
        functools.partial(_gla_kernel, n_cast=len(cast_weights)),
        grid=grid,
        in_specs=[tok(GLA_KW), tok(GLA_KW), tok(GLA_VW), tok(GLA_VW), tok(2 * GLA_KW),
                  _const_spec((1, GLA_DV))] + cast_specs,
        out_specs=[tok(GLA_VW)] + cast_specs,
        out_shape=[jax.ShapeDtypeStruct((B, T, GLA_VW), BF16)] + cast_shapes,
        scratch_shapes=[pltpu.VMEM((GLA_DV, GLA_KW), F32),
                        pltpu.VMEM((tm // GLA_CHUNK, GLA_HEADS * GLA_CHUNK, GLA_KW), BF16),
                        pltpu.VMEM((tm // GLA_CHUNK, GLA_DV, GLA_KW), F32),
                        pltpu.VMEM((tm // GLA_CHUNK, 1, GLA_KW), F32),
                        pltpu.VMEM((tm, GLA_VW), F32)],
        compiler_params=_params(("parallel", "arbitrary")),
        name="gla",
    )(gq, gk, gv, gr, la, gain, *cast_weights)
    return outs[0], outs[1:]


def _memkv_kernel(mem_ref, g_ref, wk_ref, wv_ref, k_ref, v_ref):
    mn = _rms(mem_ref[...], g_ref[...]).astype(BF16)
    k_ref[...] = _dot(mn, wk_ref[...]).astype(BF16)
    v_ref[...] = _dot(mn, wv_ref[...]).astype(BF16)


def _mem_kv(mem, g, wk, wv):
    B, M, D = mem.shape
    rows = B * M
    tm = min(TOK_TILE, rows)
    blk = pl.BlockSpec((tm, D), lambda i: (i, 0))
    k, v = pl.pallas_call(
        _memkv_kernel,
        grid=(rows // tm,),
        in_specs=[blk, _const_spec((1, D)), _const_spec((D, D)), _const_spec((D, D))],
        out_specs=[blk, blk],
        out_shape=[jax.ShapeDtypeStruct((rows, D), BF16)] * 2,
        compiler_params=_params(("parallel",)),
        name="mem_kv",
    )(mem.reshape(rows, D), g, wk, wv)
    return k.reshape(B, M, D), v.reshape(B, M, D)


def _outcross_kernel(h_ref, og_ref, os_ref, kc_ref, vc_ref, wout_ref, g_ref,
                     wq_ref, wo_ref, o_ref):
    mix = jnp.concatenate([og_ref[...], os_ref[...]], axis=1)
    h1 = h_ref[...] + _dot(mix, wout_ref[...])
    inv_rms = lax.rsqrt(jnp.mean(h1 * h1, axis=-1, keepdims=True) + EPS)
    hg = (h1 * g_ref[...]).astype(BF16)
    q = (_dot(hg, wq_ref[...]) * (inv_rms * (CROSS_DH ** -0.5 * LOG2E))).astype(BF16)
    tm = h1.shape[0]
    blk = tm // CROSS_ROW_BLOCKS
    items = [(slice(rb * blk, (rb + 1) * blk), slice(hd * CROSS_DH, (hd + 1) * CROSS_DH))
             for rb in range(CROSS_ROW_BLOCKS) for hd in range(CROSS_HEADS)]
    scores, probs, attn = {}, {}, {}
    for stage in range(len(items) + 2):
        if stage < len(items):
            rows, c = items[stage]
            scores[stage] = lax.dot_general(q[rows, c], kc_ref[:, c], _NT,
                                            preferred_element_type=F32)
        if 0 <= stage - 1 < len(items):
            s = scores.pop(stage - 1)
            p = jnp.exp2(s - jnp.max(s, axis=-1, keepdims=True))
            probs[stage - 1] = (p.astype(BF16), 1.0 / jnp.sum(p, axis=-1, keepdims=True))
        if 0 <= stage - 2 < len(items):
            i = stage - 2
            rows, c = items[i]
            p, inv_denom = probs.pop(i)
            o = (_dot(p, vc_ref[:, c]) * inv_denom).astype(BF16)
            part = _dot(o, wo_ref[c, :])
            rb = i // CROSS_HEADS
            attn[rb] = part if i % CROSS_HEADS == 0 else attn[rb] + part
            if i % CROSS_HEADS == CROSS_HEADS - 1:
                o_ref[rows, :] = h1[rows, :] + attn.pop(rb)


def _out_cross(h, og, osw, kc, vc, wout, g, wq, wo):
    B, T, D = h.shape
    tm = TOK_TILE

    def tok(width):
        return pl.BlockSpec((None, tm, width), lambda b, i: (b, i, 0))

    memblk = pl.BlockSpec((None, MEM_LEN, D), lambda b, i: (b, 0, 0))
    return pl.pallas_call(
        _outcross_kernel,
        grid=(B, T // tm),
        in_specs=[tok(D), tok(GLA_VW), tok(SWA_QW), memblk, memblk,
                  _const_spec((GLA_VW + SWA_QW, D)), _const_spec((1, D)),
                  _const_spec((D, D)), _const_spec((D, D))],
        out_specs=tok(D),
        out_shape=jax.ShapeDtypeStruct((B, T, D), F32),
        compiler_params=_params(("parallel", "parallel")),
        name="out_cross",
    )(h, og, osw, kc, vc, wout, g, wq, wo)


def _ffn_kernel(h_ref, g_ref, wu_ref, wd_ref, cw_ref, cb_ref, gf_ref,
                o_ref, hn_ref, acc_ref, ubuf_ref, carry_ref, *, final_norm):
    tm = h_ref.shape[0]
    n_slab = 2 * FFN_CHUNK // LANES
    half = n_slab // 2

    @pl.when(pl.program_id(1) == 0)
    def _():
        carry_ref[...] = jnp.zeros_like(carry_ref)

    hn_ref[...] = _rms(h_ref[...], g_ref[...]).astype(BF16)

    def cols_of(c, s):
        start = (s // half) * D_FF + c * FFN_CHUNK + (s % half) * LANES
        return slice(start, start + LANES)

    def up(c):
        hn = hn_ref[...]
        for part in range(2):
            start = part * D_FF + c * FFN_CHUNK
            u = _dot(hn, wu_ref[:, start:start + FFN_CHUNK])
            for j in range(half):
                s = part * half + j
                us = u[:, j * LANES:(j + 1) * LANES]
                ubuf_ref[c % FFN_UBUFS, s, 0:CARRY_ROWS, :] = carry_ref[c, s]
                ubuf_ref[c % FFN_UBUFS, s, CARRY_ROWS:, :] = us
                carry_ref[c, s] = us[tm - CARRY_ROWS:, :]

    def conv(c, s, rows):
        lanes = cols_of(c, s)
        taps = [ubuf_ref[c % FFN_UBUFS, s,
                         CARRY_ROWS - d + rows.start:CARRY_ROWS - d + rows.stop, :]
                for d in range(CONV_WIDTH)]
        return (cw_ref[2:3, lanes] * taps[0]
                + (cw_ref[1:2, lanes] * taps[1] + (cw_ref[0:1, lanes] * taps[2] + cb_ref[:, lanes])))

    lookahead = FFN_UBUFS - 1
    for c in range(lookahead):
        up(c)
    last = N_FFN_CHUNKS - 1

    def down(first_chunk, last_chunk, rows, act):
        res = _dot(act, wd_ref[first_chunk * FFN_CHUNK:(last_chunk + 1) * FFN_CHUNK, :])
        base = h_ref[rows, :] if first_chunk == 0 else acc_ref[rows, :]
        if last_chunk < last:
            acc_ref[rows, :] = base + res
        else:
            out = base + res
            if final_norm:
                out = _rms(out, gf_ref[...])
            o_ref[rows, :] = out

    row_blocks = [slice(r, r + tm // FFN_ROW_BLOCKS) for r in range(0, tm, tm // FFN_ROW_BLOCKS)]
    pending = []
    for c in range(N_FFN_CHUNKS):
        if c + lookahead <= last:
            up(c + lookahead)
        pending.append([[(_silu(conv(c, s, rows)) * conv(c, half + s, rows)).astype(BF16)
                         for s in range(half)] for rows in row_blocks])
        if len(pending) == FFN_DOWN_GROUP or c == last:
            for b, rows in enumerate(row_blocks):
                act = jnp.concatenate([a for chunk_acts in pending for a in chunk_acts[b]], axis=1)
                down(c + 1 - len(pending), c, rows, act)
            pending = []


def _ffn(h, g, wu, wd, cw, cb, gf, final_norm):
    B, T, D = h.shape
    tm = FFN_TILE
    nc, fc = N_FFN_CHUNKS, FFN_CHUNK
    n_slab = 2 * fc // LANES
    tok = pl.BlockSpec((None, tm, D), lambda b, i: (b, i, 0))

    def once(shape):
        zeros = (0,) * len(shape)
        return pl.BlockSpec(shape, lambda *_: zeros, pipeline_mode=pl.Buffered(1))

    return pl.pallas_call(
        functools.partial(_ffn_kernel, final_norm=final_norm),
        grid=(B, T // tm),
        in_specs=[tok, _const_spec((1, D)), once((D, 2 * D_FF)), once((D_FF, D)),
                  _const_spec((CONV_WIDTH, 2 * D_FF)), _const_spec((1, 2 * D_FF)),
                  _const_spec((1, D))],
        out_specs=tok,
        out_shape=jax.ShapeDtypeStruct((B, T, D), F32),
        scratch_shapes=[pltpu.VMEM((tm, D), BF16), pltpu.VMEM((tm, D), F32),
                        pltpu.VMEM((FFN_UBUFS, n_slab, CARRY_ROWS + tm, LANES), F32),
                        pltpu.VMEM((nc, n_slab, CARRY_ROWS, LANES), F32)],
        compiler_params=_params(("parallel", "arbitrary")),
        name="ffn",
    )(h, g, wu, wd, cw, cb, gf)


def _split_in_proj(w_in):
    w_rank = jnp.pad(w_in[:, _IN_OFF[4]:_IN_OFF[5]], ((0, 0), (0, RANK_PAD - GLA_RANK)))
    return (w_in[:, :_IN_OFF[4]].astype(BF16), w_rank.astype(BF16),
            w_in[:, _IN_OFF[5]:].astype(BF16))


def kernel(x, mem, norm_mix, w_in, w_alpha, b_alpha, gla_gain, rel_bias, sinks, w_out, norm_cross,
           norm_mem, w_q_c, w_k_c, w_v_c, w_o_c, norm_ffn, w_up, conv_w, conv_b, w_down, norm_final):
    depth = w_in.shape[0]
    row = lambda v: v.reshape(1, -1).astype(F32)
    h = x
    for l in range(depth):
        wal_pad = jnp.pad(w_alpha[l], ((0, RANK_PAD - GLA_RANK), (0, 0)))
        (gq, gk, gv, gr, la, o_swa), (wout, wq, wo, wk, wv) = _in_proj_swa(
            h, row(norm_mix[l]), _split_in_proj(w_in[l]), wal_pad, row(b_alpha[l]),
            rel_bias.astype(F32), sinks[l].astype(F32),
            cast_weights=[w_out[l], w_q_c[l], w_o_c[l], w_k_c[l], w_v_c[l]])
        o_gla, (wup, wdown) = _gla(gq, gk, gv, gr, la, row(gla_gain[l]),
                                   cast_weights=[w_up[l], w_down[l]])
        kc, vc = _mem_kv(mem, row(norm_mem[l]), wk, wv)
        h = _out_cross(h, o_gla, o_swa, kc, vc, wout, row(norm_cross[l]), wq, wo)
        h = _ffn(h, row(norm_ffn[l]), wup, wdown, conv_w[l].astype(F32), row(conv_b[l]),
                 row(norm_final), final_norm=(l == depth - 1))
    return h
```
